```python
import math
import jax, jax.numpy as jnp
from jax import lax
import numpy as np

D_MODEL = 1024
BATCH = 8
SEQ = 2048
DEPTH = 4
DEC_BATCH = 32
DEC_SEQ = 8
PAST_LEN = 8192
PAGE_SIZE = 128

F32 = jnp.float32
HEAD_DIM = 64
N_HEADS_GROUP = 4
GROUP_WIDTH = N_HEADS_GROUP * HEAD_DIM
N_MIXERS = 4
MIX_WIDTH = N_MIXERS * GROUP_WIDTH
QUERY_BLOCK = 128
NSA_BLOCK = 64
NSA_TOPN = 8
NSA_WINDOW = 512
NSA_ROWS = 4
NSA_FORCE = 1.0e4
ROPE_THETA = 500000.0
ROPE_DIMS = HEAD_DIM // 4
RET_THETA = 10000.0
RET_CHUNK = 128
GDN_CHUNK = 64
CONV_K = 4
N_EXPERTS = 64
TOP_K = 8
N_GROUPS = 8
TOPK_GROUPS = 4
EXPERT_FF = 256
SHARED_FF = 256
ROUTED_SCALE = 2.5
MOE_BLOCK = 128
DN_ALPHA = (2 * DEPTH) ** 0.25
DN_BETA = (8 * DEPTH) ** -0.25
LN_EPS = 1e-5
NORM_EPS = 1e-6
NEG_BIG = -1e30
IN_SPLITS = (GROUP_WIDTH, 6 * HEAD_DIM, 3 * N_HEADS_GROUP,
             GROUP_WIDTH, GROUP_WIDTH, GROUP_WIDTH, GROUP_WIDTH,
             GROUP_WIDTH, GROUP_WIDTH, GROUP_WIDTH, N_HEADS_GROUP,
             3 * GROUP_WIDTH, N_HEADS_GROUP, N_HEADS_GROUP, GROUP_WIDTH)
IN_WIDTH = sum(IN_SPLITS)

kernel_name = 'hybrid_nsa_ret_fox_gdn_moe_step'


def layer_norm(x, g=None, b=None):
    xf = x.astype(F32)
    mu = jnp.mean(xf, -1, keepdims=True)
    var = jnp.mean(jnp.square(xf - mu), -1, keepdims=True)
    y = (xf - mu) * lax.rsqrt(var + LN_EPS)
    if g is not None:
        y = y * g.astype(F32) + b.astype(F32)
    return y.astype(x.dtype)


def l2norm(x):
    return x * lax.rsqrt(jnp.sum(x * x, -1, keepdims=True) + NORM_EPS)


def rope(x, pos, n_rot, theta):
    half = n_rot // 2
    inv = theta ** (-jnp.arange(half, dtype=F32) / half)
    ang = pos.astype(F32)[:, None] * inv[None, :]
    bshape = (ang.shape[0],) + (1,) * (x.ndim - 3) + (half,)
    cos, sin = jnp.cos(ang).reshape(bshape), jnp.sin(ang).reshape(bshape)
    xf = x.astype(F32)
    x1, x2, rest = xf[..., :half], xf[..., half:n_rot], xf[..., n_rot:]
    out = jnp.concatenate([x1 * cos - x2 * sin, x2 * cos + x1 * sin, rest], axis=-1)
    return out.astype(x.dtype)


def masked_softmax(s, mask):
    s = jnp.where(mask, s.astype(F32), NEG_BIG)
    e = jnp.where(mask, jnp.exp(s - jnp.max(s, -1, keepdims=True)), 0.0)
    return e / jnp.maximum(jnp.sum(e, -1, keepdims=True), 1e-30)


def query_block(t):
    return QUERY_BLOCK if t % QUERY_BLOCK == 0 else t


def to_chunks(x, c):
    b, t = x.shape[:2]
    pad = (-t) % c
    x = jnp.pad(x, [(0, 0), (0, pad)] + [(0, 0)] * (x.ndim - 2))
    x = x.reshape((b, (t + pad) // c, c) + x.shape[2:])
    return jnp.moveaxis(x, 1, 0)


def from_chunks(y, t):
    n, b, c = y.shape[:3]
    return jnp.moveaxis(y, 0, 1).reshape((b, n * c) + y.shape[3:])[:, :t]


def nsa_mixer(q, kv_new, gate_logits, kv_past, win_hist, pool_logits, p0):
    b, t, h, d = q.shape
    pos = p0 + jnp.arange(t)
    q_rot = rope(q, pos, ROPE_DIMS, ROPE_THETA)
    k_sel = rope(kv_new[:, :, 2], pos, ROPE_DIMS, ROPE_THETA)
    k_win = rope(kv_new[:, :, 4], pos, ROPE_DIMS, ROPE_THETA)
    new_rows = jnp.stack([kv_new[:, :, 0], kv_new[:, :, 1], k_sel, kv_new[:, :, 3]], axis=2)
    full = jnp.concatenate([kv_past.astype(new_rows.dtype), new_rows], axis=1)
    n_keys = full.shape[1]
    nb = -(-n_keys // NSA_BLOCK)
    full = jnp.pad(full, ((0, 0), (0, nb * NSA_BLOCK - n_keys), (0, 0), (0, 0)))
    blk = full.reshape(b, nb, NSA_BLOCK, NSA_ROWS, d)
    pool = jax.nn.softmax(pool_logits.astype(F32), axis=-1).astype(blk.dtype)
    comp_k = jnp.einsum('bnld,l->bnd', blk[..., 0, :], pool[0])
    comp_v = jnp.einsum('bnld,l->bnd', blk[..., 1, :], pool[1])
    ks_blk, vs_blk = blk[..., 2, :], blk[..., 3, :]
    new_win = jnp.stack([k_win, kv_new[:, :, 5]], axis=2)
    wb = win_hist.shape[1]
    hist_all = jnp.concatenate([win_hist.astype(new_win.dtype), new_win], axis=1)
    new_hist = hist_all[:, hist_all.shape[1] - min(NSA_WINDOW, wb + t):]
    ext = jnp.pad(hist_all, ((0, 0), (NSA_WINDOW - wb, 0), (0, 0), (0, 0)))
    gates = jax.nn.sigmoid(gate_logits.astype(F32))
    n_sel = min(NSA_TOPN, nb)
    qb = query_block(t)
    scale = d ** -0.5
    blk_ids = jnp.arange(nb)

    def block(i):
        s0 = i * qb
        qn = lax.dynamic_slice_in_dim(q, s0, qb, 1)
        qr = lax.dynamic_slice_in_dim(q_rot, s0, qb, 1)
        gi = lax.dynamic_slice_in_dim(gates, s0, qb, 1)
        qp = p0 + s0 + jnp.arange(qb)
        sc = jnp.einsum('bqhd,bnd->bqhn', qn, comp_k) * scale
        cmask = blk_ids[None, :] < ((qp + 1) // NSA_BLOCK)[:, None]
        pc = masked_softmax(sc, cmask[None, :, None, :])
        o_c = jnp.einsum('bqhn,bnd->bqhd', pc.astype(comp_v.dtype), comp_v)
        cur = (qp // NSA_BLOCK)[:, None]
        imp = jnp.sum(pc, axis=2)
        imp = jnp.where((blk_ids == cur) | (blk_ids == 0), NSA_FORCE, imp)
        imp = jnp.where(blk_ids <= cur, imp, -1.0)
        _, idx = lax.top_k(imp, n_sel)
        ksel = jax.vmap(lambda kb_, ib_: kb_[ib_])(ks_blk, idx)
        vsel = jax.vmap(lambda vb_, ib_: vb_[ib_])(vs_blk, idx)
        kpos = idx[..., None] * NSA_BLOCK + jnp.arange(NSA_BLOCK)
        smask = (kpos <= qp[None, :, None, None]).reshape(b, qb, 1, n_sel * NSA_BLOCK)
        ss = jnp.einsum('bqhd,bqnld->bqhnl', qr, ksel) * scale
        ps = masked_softmax(ss.reshape(b, qb, h, n_sel * NSA_BLOCK), smask)
        o_s = jnp.einsum('bqhm,bqmd->bqhd', ps.astype(vsel.dtype), vsel.reshape(b, qb, n_sel * NSA_BLOCK, d))
        kw = lax.dynamic_slice_in_dim(ext, s0, NSA_WINDOW + qb, 1)
        wpos = p0 - NSA_WINDOW + s0 + jnp.arange(NSA_WINDOW + qb)
        wmask = (wpos[None, :] >= 0) & (wpos[None, :] <= qp[:, None]) & (wpos[None, :] > qp[:, None] - NSA_WINDOW)
        sw = jnp.einsum('bqhd,bkd->bqhk', qr, kw[:, :, 0]) * scale
        pw = masked_softmax(sw, wmask[None, :, None, :])
        o_w = jnp.einsum('bqhk,bkd->bqhd', pw.astype(kw.dtype), kw[:, :, 1])
        out = gi[..., 0:1] * o_c + gi[..., 1:2] * o_s + gi[..., 2:3] * o_w
        return out.astype(q.dtype)

    outs = lax.map(block, jnp.arange(t // qb))
    return jnp.moveaxis(outs, 0, 1).reshape(b, t, h * d), new_rows, new_hist


def chunked_decay_attention(q, k, v, log_decay, state0, chunk):
    t = q.shape[1]
    c = min(chunk, t)
    tri = jnp.tril(jnp.ones((c, c), bool))
    xs = tuple(to_chunks(a.astype(F32), c) for a in (q, k, v, log_decay))

    def step(s, inp):
        qc, kc, vc, lc = inp
        a = jnp.cumsum(lc, axis=1)
        diff = a[:, :, None, :] - a[:, None, :, :]
        dec = jnp.exp(jnp.where(tri[None, :, :, None], diff, -jnp.inf))
        att = jnp.einsum('bthd,bshd->btsh', qc, kc) * dec
        o = jnp.einsum('btsh,bshe->bthe', att, vc) + jnp.einsum('bthd,bhde->bthe', qc * jnp.exp(a)[..., None], s)
        a_last = a[:, -1]
        s = jnp.exp(a_last)[..., None, None] * s + jnp.einsum('bshd,bshe->bhde', kc * jnp.exp(a_last[:, None] - a)[..., None], vc)
        return s, o

    s, o = lax.scan(step, state0.astype(F32), xs)
    return from_chunks(o, t), s


def retention_mixer(q, k, v, z, state0, gn_g, gn_b, p0):
    b, t, h, d = q.shape
    pos = p0 + jnp.arange(t)
    q = rope(q, pos, HEAD_DIM, RET_THETA)
    k = rope(k, pos, HEAD_DIM, RET_THETA) * d ** -0.5
    log_gamma = jnp.log1p(-jnp.exp2(-5.0 - jnp.arange(h, dtype=F32)))
    ld = jnp.broadcast_to(log_gamma, (b, t, h))
    o, state = chunked_decay_attention(q, k, v, ld, state0, RET_CHUNK)
    mu = jnp.mean(o, -1, keepdims=True)
    var = jnp.mean(jnp.square(o - mu), -1, keepdims=True)
    on = ((o - mu) * lax.rsqrt(var + NORM_EPS)).reshape(b, t, h * d) * gn_g.astype(F32) + gn_b.astype(F32)
    out = on * jax.nn.silu(z.astype(F32))
    return out.astype(z.dtype), state


def fox_mixer(q, k, v, f_logit, f_bias, kv_past, logf_past):
    b, t, h, d = q.shape
    logf = jax.nn.log_sigmoid(f_logit.astype(F32) + f_bias.astype(F32))
    new_kv = jnp.stack([k, v], axis=2)
    kv = jnp.concatenate([kv_past.astype(new_kv.dtype), new_kv], axis=1)
    lf = jnp.concatenate([logf_past.astype(F32), logf], axis=1)
    n_keys = kv.shape[1]
    cum = jnp.moveaxis(jnp.cumsum(lf, axis=1), 1, 2)
    kk, vv = kv[:, :, 0], kv[:, :, 1]
    qb = query_block(t)
    scale = d ** -0.5
    kidx = jnp.arange(n_keys)

    def block(i):
        s0 = i * qb
        qi = lax.dynamic_slice_in_dim(q, s0, qb, 1)
        qrow = n_keys - t + s0 + jnp.arange(qb)
        cq = lax.dynamic_slice_in_dim(cum, n_keys - t + s0, qb, 2)
        s = jnp.einsum('bqhd,bkhd->bhqk', qi, kk).astype(F32) * scale + cq[..., None] - cum[:, :, None, :]
        p = masked_softmax(s, (kidx[None, :] <= qrow[:, None])[None, None])
        return jnp.einsum('bhqk,bkhd->bqhd', p.astype(vv.dtype), vv)

    outs = lax.map(block, jnp.arange(t // qb))
    return jnp.moveaxis(outs, 0, 1).reshape(b, t, h * d), new_kv, logf


def chunked_gated_delta(q, k, v, g, beta, state0, chunk):
    t = q.shape[1]
    dv = v.shape[-1]
    c = min(chunk, t)
    incl = jnp.tril(jnp.ones((c, c), bool))
    strict = jnp.tril(jnp.ones((c, c), bool), -1)
    eye = jnp.eye(c, dtype=F32)
    xs = tuple(to_chunks(a.astype(F32), c) for a in (q, k, v, g, beta))

    def step(s, inp):
        qc, kc, vc, gc, bc = inp
        a = jnp.cumsum(gc, axis=1)
        at = jnp.moveaxis(a, 1, 2)
        diff = at[..., :, None] - at[..., None, :]
        kb = kc * bc[..., None]
        m = jnp.einsum('bthd,bshd->bhts', kb, kc) * jnp.exp(jnp.where(strict, diff, -jnp.inf))
        rhs = jnp.concatenate([jnp.moveaxis(vc * bc[..., None], 1, 2),
                               jnp.moveaxis(kb * jnp.exp(a)[..., None], 1, 2)], axis=-1)
        sol = lax.linalg.triangular_solve(eye + m, rhs, left_side=True, lower=True, unit_diagonal=True)
        v_new = sol[..., :dv] - jnp.einsum('bhtd,bhde->bhte', sol[..., dv:], s)
        att = jnp.einsum('bthd,bshd->bhts', qc, kc) * jnp.exp(jnp.where(incl, diff, -jnp.inf))
        o = jnp.einsum('bthd,bhde->bthe', qc * jnp.exp(a)[..., None], s) + jnp.einsum('bhts,bhse->bthe', att, v_new)
        s = jnp.exp(at[..., -1])[..., None, None] * s + jnp.einsum('bshd,bhse->bhde', kc * jnp.exp(a[:, -1:] - a)[..., None], v_new)
        return s, o

    s, o = lax.scan(step, state0.astype(F32), xs)
    return from_chunks(o, t), s


def gdn_mixer(qkv, beta_logit, a_logit, z, conv_buf, state0, conv_w, a_log, dt_bias, norm_g):
    b, t, _ = qkv.shape
    xp = jnp.concatenate([conv_buf.astype(qkv.dtype), qkv], axis=1)
    new_buf = xp[:, t:]
    conv = sum(xp[:, j:j + t] * conv_w[j] for j in range(CONV_K))
    conv = jax.nn.silu(conv.astype(F32))
    q, k, v = [a.reshape(b, t, N_HEADS_GROUP, HEAD_DIM) for a in jnp.split(conv, 3, axis=-1)]
    q = l2norm(q) * HEAD_DIM ** -0.5
    k = l2norm(k)
    beta = jax.nn.sigmoid(beta_logit.astype(F32))
    g = -jnp.exp(a_log.astype(F32)) * jax.nn.softplus(a_logit.astype(F32) + dt_bias.astype(F32))
    o, state = chunked_gated_delta(q, k, v, g, beta, state0, GDN_CHUNK)
    o = o * lax.rsqrt(jnp.mean(o * o, -1, keepdims=True) + NORM_EPS) * norm_g.astype(F32)
    out = o.reshape(b, t, GROUP_WIDTH) * jax.nn.silu(z.astype(F32))
    return out.astype(qkv.dtype), state, new_buf


def token_mixers(u, p0, past, mix_w):
    nsa_past, win_hist, fox_kv_past, fox_logf_past, ret_s0, gdn_s0, conv_buf = past
    (w_in, w_out, nsa_pool, ret_gn_g, ret_gn_b, fox_f_bias,
     gdn_conv_w, gdn_a_log, gdn_dt_bias, gdn_norm_g) = mix_w
    b, t, _ = u.shape
    cuts = [int(c) for c in np.cumsum(IN_SPLITS)[:-1]]
    (q_a, kv_a, g_a, q_b, k_b, v_b, z_b, q_c, k_c, v_c, f_c,
     qkv_d, beta_d, a_d, z_d) = jnp.split(u @ w_in, cuts, axis=-1)
    heads = lambda a: a.reshape(b, t, N_HEADS_GROUP, HEAD_DIM)
    o_a, nsa_rows, win_new = nsa_mixer(heads(q_a), kv_a.reshape(b, t, 6, HEAD_DIM),
                                       g_a.reshape(b, t, N_HEADS_GROUP, 3), nsa_past, win_hist, nsa_pool, p0)
    o_b, ret_s = retention_mixer(heads(q_b), heads(k_b), heads(v_b), z_b, ret_s0, ret_gn_g, ret_gn_b, p0)
    o_c, fox_rows, fox_logf = fox_mixer(heads(q_c), heads(k_c), heads(v_c), f_c, fox_f_bias, fox_kv_past, fox_logf_past)
    o_d, gdn_s, conv_new = gdn_mixer(qkv_d, beta_d, a_d, z_d, conv_buf, gdn_s0,
                                     gdn_conv_w, gdn_a_log, gdn_dt_bias, gdn_norm_g)
    y = jnp.concatenate([o_a, o_b, o_c, o_d], axis=-1) @ w_out
    return y, (nsa_rows, fox_rows, fox_logf, win_new, ret_s, gdn_s, conv_new)


def swiglu(x, w_gu, w_dn):
    g, u = jnp.split(x @ w_gu, 2, axis=-1)
    return (jax.nn.silu(g) * u) @ w_dn


def routed_experts(xf, eidx, wts, w_gu, w_dn):
    n_tok, k = eidx.shape
    n_asg = n_tok * k
    flat_e = eidx.reshape(-1)
    order = jnp.argsort(flat_e)
    se = flat_e[order]
    counts = jnp.zeros((N_EXPERTS,), jnp.int32).at[flat_e].add(1)
    padded = (counts + MOE_BLOCK - 1) // MOE_BLOCK * MOE_BLOCK
    pad_end = jnp.cumsum(padded)
    pad_start = pad_end - padded
    grp_start = jnp.cumsum(counts) - counts
    dest = pad_start[se] + jnp.arange(n_asg, dtype=jnp.int32) - grp_start[se]
    n_blk = -(-n_asg // MOE_BLOCK) + N_EXPERTS
    row_tok = jnp.full((n_blk * MOE_BLOCK,), n_tok, jnp.int32).at[dest].set((order // k).astype(jnp.int32))
    row_w = jnp.zeros((n_blk * MOE_BLOCK,), xf.dtype).at[dest].set(wts.reshape(-1)[order].astype(xf.dtype))
    blk_e = jnp.minimum(jnp.searchsorted(pad_end, jnp.arange(n_blk) * MOE_BLOCK, side='right'), N_EXPERTS - 1)
    x_pad = jnp.concatenate([xf, jnp.zeros((1, xf.shape[1]), xf.dtype)], axis=0)

    def body(acc, inp):
        e, tok, w = inp
        y = swiglu(x_pad[tok], w_gu[e], w_dn[e]) * w[:, None]
        return acc.at[tok].add(y.astype(acc.dtype)), None

    acc, _ = lax.scan(body, jnp.zeros_like(x_pad),
                      (blk_e, row_tok.reshape(n_blk, MOE_BLOCK), row_w.reshape(n_blk, MOE_BLOCK)))
    return acc[:n_tok]


def moe_ffn(u, router_w, router_b, exp_w_gu, exp_w_down, sh_w_gu, sh_w_down):
    b, t, d = u.shape
    xf = u.reshape(b * t, d)
    scores = jax.nn.sigmoid((xf @ router_w).astype(F32))
    biased = scores + router_b.astype(F32)
    grp_score = jnp.sum(lax.top_k(biased.reshape(-1, N_GROUPS, N_EXPERTS // N_GROUPS), 2)[0], axis=-1)
    _, gidx = lax.top_k(grp_score, TOPK_GROUPS)
    gmask = jnp.any(gidx[..., None] == jnp.arange(N_GROUPS), axis=1)
    emask = jnp.repeat(gmask, N_EXPERTS // N_GROUPS, axis=1)
    _, eidx = lax.top_k(jnp.where(emask, biased, -jnp.inf), TOP_K)
    w = jnp.take_along_axis(scores, eidx, axis=1)
    w = w / jnp.sum(w, -1, keepdims=True) * ROUTED_SCALE
    y = routed_experts(xf, eidx, w, exp_w_gu, exp_w_down) + swiglu(xf, sh_w_gu, sh_w_down)
    return y.reshape(b, t, d)


def trunk_layer(x, c, p0, past, mix_w, ffn_w):
    (w_mod, b_mod, ln1_g, ln1_b, ln2_g, ln2_b, router_w, router_b,
     exp_w_gu, exp_w_down, sh_w_gu, sh_w_down) = ffn_w
    mod = (c @ w_mod + b_mod)[:, None, :]
    sh1, sc1, g1, sh2, sc2, g2 = jnp.split(mod, 6, axis=-1)
    u = layer_norm(x) * (1 + sc1) + sh1
    y, st = token_mixers(u, p0, past, mix_w)
    x = layer_norm(DN_ALPHA * x + g1 * y, ln1_g, ln1_b)
    u2 = layer_norm(x) * (1 + sc2) + sh2
    y2 = moe_ffn(u2, router_w, router_b, exp_w_gu, exp_w_down, sh_w_gu, sh_w_down)
    x = layer_norm(DN_ALPHA * x + g2 * y2, ln2_g, ln2_b)
    return x, st


def setup_inputs(seed: int = 0) -> dict:
    key = jax.random.key(seed)
    ks = iter(jax.random.split(key, 48))
    nrm = lambda shape, s=1.0: jax.random.normal(next(ks), shape, F32) * s
    n_pages = PAST_LEN // PAGE_SIZE
    n_pool = (DEC_BATCH * n_pages * 5) // 4
    wb = min(NSA_WINDOW, PAST_LEN)
    page_table = jax.random.permutation(next(ks), n_pool)[:DEC_BATCH * n_pages].reshape(DEC_BATCH, n_pages).astype(jnp.int32)
    dt = jnp.exp(jax.random.uniform(next(ks), (DEPTH, N_HEADS_GROUP), F32, math.log(1e-3), math.log(1e-1)))
    a_init = jax.random.uniform(next(ks), (DEPTH, N_HEADS_GROUP), F32, 1.0, 16.0)
    return {
        'x_prompt': nrm((BATCH, SEQ, D_MODEL)),
        'x_sample': nrm((DEC_BATCH, DEC_SEQ, D_MODEL)),
        'cache_nsa': nrm((n_pool, DEPTH, PAGE_SIZE, NSA_ROWS, HEAD_DIM)),
        'cache_fox_kv': nrm((n_pool, DEPTH, PAGE_SIZE, 2, N_HEADS_GROUP, HEAD_DIM)),
        'cache_fox_logf': jax.nn.log_sigmoid(3.0 + nrm((n_pool, DEPTH, PAGE_SIZE, N_HEADS_GROUP))),
        'state_nsa_win': nrm((DEPTH, DEC_BATCH, wb, 2, HEAD_DIM)),
        'state_ret': nrm((DEPTH, DEC_BATCH, N_HEADS_GROUP, HEAD_DIM, HEAD_DIM), 0.3),
        'state_gdn': nrm((DEPTH, DEC_BATCH, N_HEADS_GROUP, HEAD_DIM, HEAD_DIM), 0.1),
        'state_gdn_conv': nrm((DEPTH, DEC_BATCH, CONV_K - 1, 3 * GROUP_WIDTH)),
        'page_table': page_table,
        'c_prompt': nrm((BATCH, D_MODEL)),
        'c_sample': nrm((DEC_BATCH, D_MODEL)),
        'w_mod': nrm((DEPTH, D_MODEL, 6 * D_MODEL), 0.5 * D_MODEL ** -0.5),
        'b_mod': nrm((DEPTH, 6 * D_MODEL), 0.01),
        'w_in': nrm((DEPTH, D_MODEL, IN_WIDTH), D_MODEL ** -0.5),
        'w_out': nrm((DEPTH, MIX_WIDTH, D_MODEL), MIX_WIDTH ** -0.5 * DN_BETA),
        'nsa_pool': nrm((DEPTH, 2, NSA_BLOCK), 0.1),
        'ret_gn_g': 1.0 + nrm((DEPTH, GROUP_WIDTH), 0.02),
        'ret_gn_b': nrm((DEPTH, GROUP_WIDTH), 0.01),
        'fox_f_bias': jnp.linspace(1.0, 6.0, N_HEADS_GROUP, dtype=F32)[None, :] + nrm((DEPTH, N_HEADS_GROUP), 0.1),
        'gdn_conv_w': nrm((DEPTH, CONV_K, 3 * GROUP_WIDTH), CONV_K ** -0.5),
        'gdn_A_log': jnp.log(a_init),
        'gdn_dt_bias': dt + jnp.log(-jnp.expm1(-dt)),
        'gdn_norm_g': 1.0 + nrm((DEPTH, HEAD_DIM), 0.02),
        'ln1_g': 1.0 + nrm((DEPTH, D_MODEL), 0.02),
        'ln1_b': nrm((DEPTH, D_MODEL), 0.01),
        'ln2_g': 1.0 + nrm((DEPTH, D_MODEL), 0.02),
        'ln2_b': nrm((DEPTH, D_MODEL), 0.01),
        'router_w': nrm((DEPTH, D_MODEL, N_EXPERTS), D_MODEL ** -0.5),
        'router_b': nrm((DEPTH, N_EXPERTS), 0.01),
        'exp_w_gu': nrm((DEPTH, N_EXPERTS, D_MODEL, 2 * EXPERT_FF), D_MODEL ** -0.5),
        'exp_w_down': nrm((DEPTH, N_EXPERTS, EXPERT_FF, D_MODEL), EXPERT_FF ** -0.5 * DN_BETA),
        'sh_w_gu': nrm((DEPTH, D_MODEL, 2 * SHARED_FF), D_MODEL ** -0.5),
        'sh_w_down': nrm((DEPTH, SHARED_FF, D_MODEL), SHARED_FF ** -0.5 * DN_BETA),
    }


def reference(x_prompt, x_sample, cache_nsa, cache_fox_kv, cache_fox_logf, state_nsa_win, state_ret,
              state_gdn, state_gdn_conv, page_table, c_prompt, c_sample, w_mod, b_mod, w_in, w_out,
              nsa_pool, ret_gn_g, ret_gn_b, fox_f_bias, gdn_conv_w, gdn_A_log, gdn_dt_bias, gdn_norm_g,
              ln1_g, ln1_b, ln2_g, ln2_b, router_w, router_b, exp_w_gu, exp_w_down, sh_w_gu, sh_w_down):
    b = x_prompt.shape[0]
    db = x_sample.shape[0]
    past_len = page_table.shape[1] * cache_nsa.shape[2]
    mix_w = (w_in, w_out, nsa_pool, ret_gn_g, ret_gn_b, fox_f_bias, gdn_conv_w, gdn_A_log, gdn_dt_bias, gdn_norm_g)
    ffn_w = (w_mod, b_mod, ln1_g, ln1_b, ln2_g, ln2_b, router_w, router_b, exp_w_gu, exp_w_down, sh_w_gu, sh_w_down)

    def prompt_past(l):
        dt = x_prompt.dtype
        return (jnp.zeros((b, 0, NSA_ROWS, HEAD_DIM), dt), jnp.zeros((b, 0, 2, HEAD_DIM), dt),
                jnp.zeros((b, 0, 2, N_HEADS_GROUP, HEAD_DIM), dt), jnp.zeros((b, 0, N_HEADS_GROUP), F32),
                jnp.zeros((b, N_HEADS_GROUP, HEAD_DIM, HEAD_DIM), F32),
                jnp.zeros((b, N_HEADS_GROUP, HEAD_DIM, HEAD_DIM), F32),
                jnp.zeros((b, CONV_K - 1, 3 * GROUP_WIDTH), dt))

    def sample_past(l):
        return (cache_nsa[page_table, l].reshape(db, past_len, NSA_ROWS, HEAD_DIM),
                state_nsa_win[l],
                cache_fox_kv[page_table, l].reshape(db, past_len, 2, N_HEADS_GROUP, HEAD_DIM),
                cache_fox_logf[page_table, l].reshape(db, past_len, N_HEADS_GROUP),
                state_ret[l], state_gdn[l], state_gdn_conv[l])

    def run_trunk(x, c, p0, get_past):
        per_layer = []
        for l in range(DEPTH):
            x, st = trunk_layer(x, c, p0, get_past(l), tuple(w[l] for w in mix_w), tuple(w[l] for w in ffn_w))
            per_layer.append(st)
        nsa, fkv, flf, win, ret, gdn, conv = zip(*per_layer)
        return x, (jnp.stack(nsa, 1), jnp.stack(fkv, 1), jnp.stack(flf, 1), jnp.stack(win, 0),
                   jnp.stack(ret, 0), jnp.stack(gdn, 0), jnp.stack(conv, 0))

    y_prompt, (nsa_p, fkv_p, flf_p, win_p, ret_p, gdn_p, conv_p) = run_trunk(x_prompt, c_prompt, 0, prompt_past)
    y_sample, (nsa_s, fkv_s, flf_s, win_s, ret_s, gdn_s, conv_s) = run_trunk(x_sample, c_sample, past_len, sample_past)
    return (y_prompt, y_sample, nsa_p, nsa_s, fkv_p, fkv_s, flf_p, flf_s, win_p, win_s,
            ret_p, ret_s, gdn_p, gdn_s, conv_p, conv_s)
```

```python
import functools
import math

import numpy as np
import jax
import jax.numpy as jnp
from jax import lax
from jax.experimental import pallas as pl
from jax.experimental.pallas import tpu as pltpu

F32 = jnp.float32
BF16 = jnp.bfloat16
HIGHEST = lax.Precision.HIGHEST

D_MODEL = 1024
DEPTH = 4
HEAD_DIM = 64
N_HEADS = 4
GROUP_WIDTH = N_HEADS * HEAD_DIM
NSA_BLOCK = 64
NSA_TOPN = 8
NSA_WINDOW = 512
NSA_FORCE = 1.0e4
ROPE_THETA = 500000.0
ROPE_DIMS = HEAD_DIM // 4
RET_THETA = 10000.0
RET_CHUNK = 128
GDN_CHUNK = 64
CONV_K = 4
N_EXPERTS = 64
TOP_K = 8
N_GROUPS = 8
TOPK_GROUPS = 4
EXPERT_FF = 256
ROUTED_SCALE = 2.5
DN_ALPHA = (2 * DEPTH) ** 0.25
LN_EPS = 1e-5
NORM_EPS = 1e-6
NEG_BIG = -1e30
SCALE = HEAD_DIM ** -0.5
QUERY_BLOCK = 128
LANES = 128
VMEM_LIMIT = 56 * 1024 * 1024

IN_SPLITS = (GROUP_WIDTH, 6 * HEAD_DIM, 3 * N_HEADS,
             GROUP_WIDTH, GROUP_WIDTH, GROUP_WIDTH, GROUP_WIDTH,
             GROUP_WIDTH, GROUP_WIDTH, GROUP_WIDTH, N_HEADS,
             3 * GROUP_WIDTH, N_HEADS, N_HEADS, GROUP_WIDTH)
IN_WIDTH = sum(IN_SPLITS)

P_AQ, P_BQ, P_BK, P_BV, P_BZ = 0, 256, 512, 768, 1024
P_CQ, P_CK, P_CV, P_DZ, P_DQKV = 1280, 1536, 1792, 2048, 2304
P_AKC, P_AKS, P_AKW, P_AG, P_CF, P_DBA = 3072, 3200, 3328, 3456, 3584, 3712
P_WIDTH = 3840


def _proj_perm():
    src = np.cumsum((0,) + IN_SPLITS)
    (q_a, kv_a, g_a, q_b, k_b, v_b, z_b, q_c, k_c, v_c, f_c, qkv_d, beta_d, a_d, z_d) = [int(s) for s in src[:-1]]
    perm = -np.ones((P_WIDTH,), np.int64)

    def put(dst, start, width):
        perm[dst:dst + width] = np.arange(start, start + width)

    put(P_AQ, q_a, 256)
    put(P_AKC, kv_a, 128)
    put(P_AKS, kv_a + 128, 128)
    put(P_AKW, kv_a + 256, 128)
    for h in range(N_HEADS):
        for j in range(3):
            perm[P_AG + j * N_HEADS + h] = g_a + h * 3 + j
    put(P_BQ, q_b, 256)
    put(P_BK, k_b, 256)
    put(P_BV, v_b, 256)
    put(P_BZ, z_b, 256)
    put(P_CQ, q_c, 256)
    put(P_CK, k_c, 256)
    put(P_CV, v_c, 256)
    put(P_CF, f_c, 4)
    put(P_DQKV, qkv_d, 768)
    put(P_DZ, z_d, 256)
    put(P_DBA, beta_d, 4)
    put(P_DBA + 4, a_d, 4)
    return perm


def _rope_tables(pos, n_rot, theta, n_heads, pad_identity=0):
    half = n_rot // 2
    inv = theta ** (-np.arange(half, dtype=np.float64) / half)
    ang = np.asarray(pos, np.float64)[:, None] * inv[None, :]
    t = ang.shape[0]
    c = np.ones((t, HEAD_DIM)); sa = np.zeros((t, HEAD_DIM)); sb = np.zeros((t, HEAD_DIM))
    c[:, :half] = np.cos(ang); c[:, half:n_rot] = np.cos(ang)
    sa[:, :half] = -np.sin(ang)
    sb[:, half:n_rot] = np.sin(ang)
    c = np.tile(c, (1, n_heads)); sa = np.tile(sa, (1, n_heads)); sb = np.tile(sb, (1, n_heads))
    if pad_identity:
        c = np.concatenate([c, np.ones((t, pad_identity))], 1)
        sa = np.concatenate([sa, np.zeros((t, pad_identity))], 1)
        sb = np.concatenate([sb, np.zeros((t, pad_identity))], 1)
    return tuple(jnp.asarray(a, F32) for a in (c, sa, sb))


def _dot(a, b, prec=None):
    return jnp.dot(a, b, preferred_element_type=F32, precision=prec)


def _dot_nt(a, b, prec=None):
    return lax.dot_general(a, b, (((1,), (1,)), ((), ())), preferred_element_type=F32, precision=prec)


def _dot_tn(a, b, prec=None):
    return lax.dot_general(a, b, (((0,), (0,)), ((), ())), preferred_element_type=F32, precision=prec)


def _iota(shape, axis):
    return lax.broadcasted_iota(jnp.int32, shape, axis)


def _ln(x):
    mu = jnp.mean(x, -1, keepdims=True)
    xc = x - mu
    var = jnp.mean(xc * xc, -1, keepdims=True)
    return xc * lax.rsqrt(var + LN_EPS)


def _sigmoid(x):
    return 1.0 / (1.0 + jnp.exp(-x))


def _silu(x):
    return x * _sigmoid(x)


def _softplus(x):
    return jnp.maximum(x, 0.0) + jnp.log1p(jnp.exp(-jnp.abs(x)))


def _log_sigmoid(x):
    return -_softplus(-x)


def _rope(x, c, sa, sb, half):
    w = x.shape[-1]
    return x * c + pltpu.roll(x, w - half, 1) * sa + pltpu.roll(x, half, 1) * sb


def _masked_softmax(s, mask, axis):
    s = jnp.where(mask, s, NEG_BIG)
    e = jnp.where(mask, jnp.exp(s - jnp.max(s, axis, keepdims=True)), 0.0)
    return e / jnp.maximum(jnp.sum(e, axis, keepdims=True), 1e-30)


def _topk_mask(vals, k, axis):
    n = vals.shape[axis]
    idx = _iota(vals.shape, axis).astype(F32)
    sel = jnp.zeros(vals.shape, F32)
    work = vals
    for _ in range(k):
        m = jnp.max(work, axis, keepdims=True)
        first = jnp.min(jnp.where(work == m, idx, float(n)), axis, keepdims=True)
        pick = idx == first
        sel = jnp.where(pick, 1.0, sel)
        work = jnp.where(pick, -jnp.inf, work)
    return sel


def _pool_weights(pool_ref, n_rep):
    pl_t = pool_ref[...]
    e = jnp.exp(pl_t - jnp.max(pl_t, -1, keepdims=True))
    return e / (jnp.sum(e, -1, keepdims=True) / float(n_rep))


def _params(sem):
    return pltpu.CompilerParams(dimension_semantics=sem, vmem_limit_bytes=VMEM_LIMIT)


def _mod_kernel(c_ref, w_ref, b_ref, o_ref):
    o_ref[0] = _dot(c_ref[...].astype(BF16), w_ref[0].astype(BF16)) + b_ref[0]


def _mod_call(c_all, w_mod, b_mod):
    n = c_all.shape[0]
    tn = 1536
    return pl.pallas_call(
        _mod_kernel,
        out_shape=jax.ShapeDtypeStruct((DEPTH, n, 6 * D_MODEL), F32),
        grid=(DEPTH, 6 * D_MODEL // tn),
        in_specs=[pl.BlockSpec((n, D_MODEL), lambda l, j: (0, 0)),
                  pl.BlockSpec((1, D_MODEL, tn), lambda l, j: (l, 0, j)),
                  pl.BlockSpec((1, 1, tn), lambda l, j: (l, 0, j))],
        out_specs=pl.BlockSpec((1, n, tn), lambda l, j: (l, 0, j)),
        compiler_params=_params(("parallel", "parallel")),
        name="mod",
    )(c_all, w_mod, b_mod.reshape(DEPTH, 1, 6 * D_MODEL))


def _proj_kernel(x_ref, sc_ref, sh_ref, w_ref, o_ref):
    u = _ln(x_ref[0]) * (1.0 + sc_ref[0]) + sh_ref[0]
    o_ref[0] = _dot(u.astype(BF16), w_ref[...])


def _mod_spec(m, tm):
    if m.shape[1] == 1:
        return pl.BlockSpec((1, 1, D_MODEL), lambda g, i: (g, 0, 0))
    return pl.BlockSpec((1, tm, D_MODEL), lambda g, i: (g, i, 0))


def _proj_call(x, sc, sh, w):
    g, r, _ = x.shape
    tm = min(256, r)
    return pl.pallas_call(
        _proj_kernel,
        out_shape=jax.ShapeDtypeStruct((g, r, P_WIDTH), F32),
        grid=(g, r // tm),
        in_specs=[pl.BlockSpec((1, tm, D_MODEL), lambda g_, i: (g_, i, 0)),
                  _mod_spec(sc, tm), _mod_spec(sh, tm),
                  pl.BlockSpec((D_MODEL, P_WIDTH), lambda g_, i: (0, 0))],
        out_specs=pl.BlockSpec((1, tm, P_WIDTH), lambda g_, i: (g_, i, 0)),
        compiler_params=_params(("parallel", "parallel")),
        name="proj",
    )(x, sc, sh, w)


def _nsa_prompt_kernel(q_ref, g_ref, kc_ref, ks_ref, kw_ref, pool_ref,
                       qc_ref, qa_ref, qb_ref, kc_t_ref, ka_t_ref, kb_t_ref,
                       oa_ref, rows_ref, win_ref,
                       comp_ref, ksb_ref, vsb_ref, kwp_ref, vwp_ref):
    qi = pl.program_id(1)
    t = kc_ref.shape[1]
    nb = t // NSA_BLOCK
    qb = q_ref.shape[1]
    wnd = NSA_WINDOW

    @pl.when(qi == 0)
    def _prep():
        kcvc = kc_ref[0]
        ks_rot = _rope(ks_ref[0], kc_t_ref[...], ka_t_ref[...], kb_t_ref[...], ROPE_DIMS // 2)
        kw_rot = _rope(kw_ref[0], kc_t_ref[...], ka_t_ref[...], kb_t_ref[...], ROPE_DIMS // 2)
        rows_ref[0, :, 0:128] = kcvc
        rows_ref[0, :, 128:256] = ks_rot
        win_ref[0] = kw_rot
        ksb_ref[...] = ks_rot[:, 0:64].astype(BF16)
        vsb_ref[...] = ks_rot[:, 64:128].astype(BF16)
        kwp_ref[0:wnd, :] = jnp.zeros((wnd, HEAD_DIM), BF16)
        vwp_ref[0:wnd, :] = jnp.zeros((wnd, HEAD_DIM), BF16)
        kwp_ref[wnd:wnd + t, :] = kw_rot[:, 0:64].astype(BF16)
        vwp_ref[wnd:wnd + t, :] = kw_rot[:, 64:128].astype(BF16)
        wts = _pool_weights(pool_ref, nb)
        same = (_iota((nb, t), 1) >> 6) == _iota((nb, t), 0)
        pk = jnp.where(same, wts[0:1, :], 0.0)
        pv = jnp.where(same, wts[1:2, :], 0.0)
        ck = _dot(pk, kcvc, HIGHEST)
        cv = _dot(pv, kcvc, HIGHEST)
        comp_ref[...] = jnp.where(_iota((nb, 128), 1) < HEAD_DIM, ck, cv)

    s0 = pl.multiple_of(qi * qb, qb)
    q = q_ref[0]
    qr = _rope(q, qc_ref[...], qa_ref[...], qb_ref[...], ROPE_DIMS // 2)
    gates = _sigmoid(g_ref[0])
    comp = comp_ref[...]
    compk = comp[:, 0:HEAD_DIM]
    compv = comp[:, HEAD_DIM:128]
    qp = s0 + _iota((qb, 1), 0)
    blk = _iota((1, nb), 1)
    cmask = blk < ((qp + 1) >> 6)
    imp = jnp.zeros((qb, nb), F32)
    o_cmp = []
    for h in range(N_HEADS):
        qh = q[:, h * HEAD_DIM:(h + 1) * HEAD_DIM]
        pc = _masked_softmax(_dot_nt(qh, compk, HIGHEST) * SCALE, cmask, -1)
        imp = imp + pc
        o_cmp.append(_dot(pc, compv))
    cur = qp >> 6
    imp = jnp.where((blk == cur) | (blk == 0), NSA_FORCE, imp)
    imp = jnp.where(blk <= cur, imp, -1.0)
    sel = _topk_mask(imp, min(NSA_TOPN, nb), 1)
    expand = ((_iota((nb, t), 1) >> 6) == _iota((nb, t), 0)).astype(BF16)
    selk = _dot(sel.astype(BF16), expand)
    key = _iota((1, t), 1)
    smask = (selk > 0.5) & (key <= qp)
    kw = kwp_ref[pl.ds(s0, wnd + qb), :]
    vw = vwp_ref[pl.ds(s0, wnd + qb), :]
    wpos = s0 - wnd + _iota((1, wnd + qb), 1)
    wmask = (wpos >= 0) & (wpos <= qp) & (wpos > qp - wnd)
    ksb = ksb_ref[...]
    vsb = vsb_ref[...]
    for h in range(N_HEADS):
        qrh = qr[:, h * HEAD_DIM:(h + 1) * HEAD_DIM].astype(BF16)
        ps = _masked_softmax(_dot_nt(qrh, ksb) * SCALE, smask, -1)
        o_sel = _dot(ps.astype(BF16), vsb)
        pw = _masked_softmax(_dot_nt(qrh, kw) * SCALE, wmask, -1)
        o_win = _dot(pw.astype(BF16), vw)
        out = (gates[:, h:h + 1] * o_cmp[h] + gates[:, N_HEADS + h:N_HEADS + h + 1] * o_sel
               + gates[:, 2 * N_HEADS + h:2 * N_HEADS + h + 1] * o_win)
        oa_ref[0, :, h * HEAD_DIM:(h + 1) * HEAD_DIM] = out.astype(oa_ref.dtype)


def _nsa_prompt_call(p, pool_l):
    b, t, _ = p.shape
    qb = QUERY_BLOCK
    nb = t // NSA_BLOCK
    pos = np.arange(t)
    q_tabs = _rope_tables(pos, ROPE_DIMS, ROPE_THETA, N_HEADS)
    k_tabs = _rope_tables(pos, ROPE_DIMS, ROPE_THETA, 1, pad_identity=HEAD_DIM)
    pool_t = jnp.tile(pool_l, (1, nb))
    full = lambda col: pl.BlockSpec((1, t, 128), lambda b_, i: (b_, 0, col))
    qtab = pl.BlockSpec((qb, 256), lambda b_, i: (i, 0))
    ktab = pl.BlockSpec((t, 128), lambda b_, i: (0, 0))
    return pl.pallas_call(
        _nsa_prompt_kernel,
        out_shape=(jax.ShapeDtypeStruct((b, t, 256), BF16),
                   jax.ShapeDtypeStruct((b, t, 256), F32),
                   jax.ShapeDtypeStruct((b, t, 128), F32)),
        grid=(b, t // qb),
        in_specs=[pl.BlockSpec((1, qb, 256), lambda b_, i: (b_, i, P_AQ // 256)),
                  pl.BlockSpec((1, qb, 128), lambda b_, i: (b_, i, P_AG // 128)),
                  full(P_AKC // 128), full(P_AKS // 128), full(P_AKW // 128),
                  pl.BlockSpec((2, t), lambda b_, i: (0, 0)),
                  qtab, qtab, qtab, ktab, ktab, ktab],
        out_specs=(pl.BlockSpec((1, qb, 256), lambda b_, i: (b_, i, 0)),
                   pl.BlockSpec((1, t, 256), lambda b_, i: (b_, 0, 0)),
                   pl.BlockSpec((1, t, 128), lambda b_, i: (b_, 0, 0))),
        scratch_shapes=[pltpu.VMEM((nb, 128), F32),
                        pltpu.VMEM((t, HEAD_DIM), BF16), pltpu.VMEM((t, HEAD_DIM), BF16),
                        pltpu.VMEM((NSA_WINDOW + t, HEAD_DIM), BF16), pltpu.VMEM((NSA_WINDOW + t, HEAD_DIM), BF16)],
        compiler_params=_params(("parallel", "arbitrary")),
        name="nsa_prompt",
    )(p, p, p, p, p, pool_t, *q_tabs, *k_tabs)


def _nsa_sample_kernel(pt_ref, page_ref, q_ref, g_ref, kc_ref, ks_ref, kw_ref, hist_ref, pool_ref,
                       qc_ref, qa_ref, qb_ref, kc_t_ref, ka_t_ref, kb_t_ref,
                       oa_ref, rows_ref, win_ref,
                       comp_ref, ksb_ref, vsb_ref, *, past):
    p = pl.program_id(1)
    n_pages = pl.num_programs(1)
    t = q_ref.shape[1]
    page_sz = page_ref.shape[2]
    cr = comp_ref.shape[0]
    nks = ksb_ref.shape[0]
    wts = _pool_weights(pool_ref, page_sz // NSA_BLOCK)
    r16 = _iota((16, page_sz), 0)
    half16 = _iota((16, page_sz), 1) >> 6
    pkv = jnp.where((r16 == half16), wts[0:1, :], 0.0) + jnp.where((r16 - 8 == half16), wts[1:2, :], 0.0)
    lane_lo = _iota((8, 128), 1) < HEAD_DIM

    @pl.when(p == 0)
    def _init():
        ksb_ref[past:nks, :] = jnp.zeros((nks - past, HEAD_DIM), BF16)
        vsb_ref[past:nks, :] = jnp.zeros((nks - past, HEAD_DIM), BF16)
        comp_ref[cr - 16:cr, :] = jnp.zeros((16, 128), F32)

    page = page_ref[0, 0]
    res = _dot(pkv, page[:, 0:128], HIGHEST)
    row0 = pl.multiple_of(p * 8, 8)
    comp_ref[pl.ds(row0, 8), :] = jnp.where(lane_lo, res[0:8], res[8:16])
    key0 = pl.multiple_of(p * page_sz, page_sz)
    ksb_ref[pl.ds(key0, page_sz), :] = page[:, 128:192].astype(BF16)
    vsb_ref[pl.ds(key0, page_sz), :] = page[:, 192:256].astype(BF16)

    @pl.when(p == n_pages - 1)
    def _finish():
        kcvc = kc_ref[0]
        ks_rot = _rope(ks_ref[0], kc_t_ref[...], ka_t_ref[...], kb_t_ref[...], ROPE_DIMS // 2)
        kw_rot = _rope(kw_ref[0], kc_t_ref[...], ka_t_ref[...], kb_t_ref[...], ROPE_DIMS // 2)
        rows_ref[0, :, 0:128] = kcvc
        rows_ref[0, :, 128:256] = ks_rot
        ksb_ref[past:past + t, :] = ks_rot[:, 0:64].astype(BF16)
        vsb_ref[past:past + t, :] = ks_rot[:, 64:128].astype(BF16)
        res_n = _dot(pkv[:, 0:t], kcvc, HIGHEST)
        comp_ref[cr - 16:cr - 8, :] = jnp.where(lane_lo, res_n[0:8], res_n[8:16])

        q = q_ref[0]
        qr = _rope(q, qc_ref[...], qa_ref[...], qb_ref[...], ROPE_DIMS // 2)
        zpad = jnp.zeros((32 - t, HEAD_DIM), F32)
        stack = lambda x: jnp.concatenate(
            [piece for h in range(N_HEADS) for piece in (x[:, h * HEAD_DIM:(h + 1) * HEAD_DIM], zpad)], 0)
        q_all = stack(q)
        qr_all = stack(qr)
        lane = _iota((1, 128), 1)
        qp = past + (lane & 31)

        comp = comp_ref[...]
        compk = comp[:, 0:HEAD_DIM]
        compv = comp[:, HEAD_DIM:128]
        ri = _iota((cr, 1), 0)
        blk = 2 * (ri >> 3) + (ri & 7)
        valid = ((ri & 7) < 2) & (blk * NSA_BLOCK < past + t)
        cmask = valid & (blk < ((qp + 1) >> 6))
        pc = _masked_softmax(_dot_nt(compk, q_all, HIGHEST) * SCALE, cmask, 0)
        o_cmp = _dot_tn(pc, compv)
        imp = pc + pltpu.roll(pc, 32, 1) + pltpu.roll(pc, 64, 1) + pltpu.roll(pc, 96, 1)
        cur = qp >> 6
        imp = jnp.where((blk == cur) | (blk == 0), NSA_FORCE, imp)
        imp = jnp.where(blk <= cur, imp, -1.0)
        imp = jnp.where(valid, imp, -2.0)
        sel = _topk_mask(imp, NSA_TOPN, 0)

        kt = _iota((nks, cr), 0)
        crow = 8 * (kt >> 7) + ((kt >> 6) & 1)
        expand = (crow == _iota((nks, cr), 1)).astype(BF16)
        selk = _dot(expand, sel.astype(BF16))
        kpos = _iota((nks, 1), 0)
        smask = (selk > 0.5) & (kpos <= qp)
        qr_bf = qr_all.astype(BF16)
        ps = _masked_softmax(_dot_nt(ksb_ref[...], qr_bf) * SCALE, smask, 0)
        o_sel = _dot_tn(ps.astype(BF16), vsb_ref[...])

        ext = jnp.concatenate([hist_ref[0], kw_rot], 0)
        win_ref[0] = ext[t:, :]
        wb = hist_ref.shape[1]
        wpos = past - wb + _iota((wb + t, 1), 0)
        wmask = (wpos >= 0) & (wpos <= qp) & (wpos > qp - NSA_WINDOW)
        pw = _masked_softmax(_dot_nt(ext[:, 0:HEAD_DIM], qr_all) * SCALE, wmask, 0)
        o_win = _dot_tn(pw, ext[:, HEAD_DIM:128])

        gates = _sigmoid(g_ref[0])
        for h in range(N_HEADS):
            r = slice(h * 32, h * 32 + t)
            out = (gates[:, h:h + 1] * o_cmp[r] + gates[:, N_HEADS + h:N_HEADS + h + 1] * o_sel[r]
                   + gates[:, 2 * N_HEADS + h:2 * N_HEADS + h + 1] * o_win[r])
            oa_ref[0, :, h * HEAD_DIM:(h + 1) * HEAD_DIM] = out.astype(oa_ref.dtype)


def _nsa_sample_call(p, cache, page_table, layer, hist, pool_l, past):
    b, t, _ = p.shape
    n_pages = page_table.shape[1]
    page_sz = cache.shape[2]
    assert page_sz == 128 and t <= 16 and past % NSA_BLOCK == 0 and past >= NSA_WINDOW
    cache2 = cache.reshape(cache.shape[0], cache.shape[1], page_sz, 256)
    hist2 = hist.reshape(b, hist.shape[1], 128)
    wb = hist2.shape[1]
    pos = past + np.arange(t)
    q_tabs = _rope_tables(pos, ROPE_DIMS, ROPE_THETA, N_HEADS)
    k_tabs = _rope_tables(pos, ROPE_DIMS, ROPE_THETA, 1, pad_identity=HEAD_DIM)
    pool_t = jnp.tile(pool_l, (1, page_sz // NSA_BLOCK))
    new = lambda col: pl.BlockSpec((1, t, 128), lambda b_, i, pt: (b_, 0, col))
    const = lambda shape: pl.BlockSpec(shape, lambda b_, i, pt: (0,) * len(shape))
    grid_spec = pltpu.PrefetchScalarGridSpec(
        num_scalar_prefetch=1,
        grid=(b, n_pages),
        in_specs=[pl.BlockSpec((1, 1, page_sz, 256), lambda b_, i, pt: (pt[b_, i], layer, 0, 0)),
                  pl.BlockSpec((1, t, 256), lambda b_, i, pt: (b_, 0, P_AQ // 256)),
                  new(P_AG // 128), new(P_AKC // 128), new(P_AKS // 128), new(P_AKW // 128),
                  pl.BlockSpec((1, wb, 128), lambda b_, i, pt: (b_, 0, 0)),
                  const((2, page_sz)),
                  const((t, 256)), const((t, 256)), const((t, 256)),
                  const((t, 128)), const((t, 128)), const((t, 128))],
        out_specs=(pl.BlockSpec((1, t, 256), lambda b_, i, pt: (b_, 0, 0)),
                   pl.BlockSpec((1, t, 256), lambda b_, i, pt: (b_, 0, 0)),
                   pl.BlockSpec((1, wb, 128), lambda b_, i, pt: (b_, 0, 0))),
        scratch_shapes=[pltpu.VMEM((8 * n_pages + 16, 128), F32),
                        pltpu.VMEM((past + NSA_BLOCK, HEAD_DIM), BF16),
                        pltpu.VMEM((past + NSA_BLOCK, HEAD_DIM), BF16)])
    return pl.pallas_call(
        functools.partial(_nsa_sample_kernel, past=past),
        out_shape=(jax.ShapeDtypeStruct((b, t, 256), BF16),
                   jax.ShapeDtypeStruct((b, t, 256), F32),
                   jax.ShapeDtypeStruct((b, wb, 128), F32)),
        grid_spec=grid_spec,
        compiler_params=_params(("parallel", "arbitrary")),
        name="nsa_sample",
    )(page_table, cache2, p, p, p, p, p, hist2, pool_t, *q_tabs, *k_tabs)


def _ret_kernel(q_ref, k_ref, v_ref, z_ref, s0_ref, c_ref, sa_ref, sb_ref, gng_ref, gnb_ref,
                o_ref, st_ref):
    ci = pl.program_id(1)
    c = q_ref.shape[1]

    @pl.when(ci == 0)
    def _init():
        st_ref[...] = s0_ref[...]

    tabs = (c_ref[...], sa_ref[...], sb_ref[...])
    q = _rope(q_ref[0], *tabs, HEAD_DIM // 2)
    k = _rope(k_ref[0], *tabs, HEAD_DIM // 2) * SCALE
    v = v_ref[0]
    z = z_ref[0]
    ii = _iota((c, c), 0)
    jj = _iota((c, c), 1)
    rowi = _iota((c, 1), 0).astype(F32)
    for h in range(N_HEADS):
        sl = slice(h * HEAD_DIM, (h + 1) * HEAD_DIM)
        lg = math.log1p(-2.0 ** (-5.0 - h))
        qh, kh, vh = q[:, sl], k[:, sl], v[:, sl]
        a = (rowi + 1.0) * lg
        dec = jnp.where(jj <= ii, jnp.exp(jnp.minimum((ii - jj).astype(F32) * lg, 0.0)), 0.0)
        s = st_ref[0, h]
        att = _dot_nt(qh, kh) * dec
        o = _dot(att, vh) + _dot(qh * jnp.exp(a), s)
        a_last = c * lg
        st_ref[0, h] = math.exp(a_last) * s + _dot_tn(kh * jnp.exp(a_last - a), vh)
        mu = jnp.mean(o, -1, keepdims=True)
        oc = o - mu
        var = jnp.mean(oc * oc, -1, keepdims=True)
        on = oc * lax.rsqrt(var + NORM_EPS) * gng_ref[:, sl] + gnb_ref[:, sl]
        o_ref[0, :, sl] = (on * _silu(z[:, sl])).astype(o_ref.dtype)


def _ret_call(p, state0, gn_g, gn_b, p0):
    b, t, _ = p.shape
    c = min(RET_CHUNK, t)
    assert t % c == 0
    tabs = _rope_tables(p0 + np.arange(t), HEAD_DIM, RET_THETA, N_HEADS)
    blk = lambda col: pl.BlockSpec((1, c, 256), lambda b_, i: (b_, i, col))
    tab = pl.BlockSpec((c, 256), lambda b_, i: (i, 0))
    st = pl.BlockSpec((1, N_HEADS, HEAD_DIM, HEAD_DIM), lambda b_, i: (b_, 0, 0, 0))
    vec = pl.BlockSpec((1, 256), lambda b_, i: (0, 0))
    return pl.pallas_call(
        _ret_kernel,
        out_shape=(jax.ShapeDtypeStruct((b, t, 256), BF16),
                   jax.ShapeDtypeStruct((b, N_HEADS, HEAD_DIM, HEAD_DIM), F32)),
        grid=(b, t // c),
        in_specs=[blk(P_BQ // 256), blk(P_BK // 256), blk(P_BV // 256), blk(P_BZ // 256), st,
                  tab, tab, tab, vec, vec],
        out_specs=(pl.BlockSpec((1, c, 256), lambda b_, i: (b_, i, 0)), st),
        compiler_params=_params(("parallel", "arbitrary")),
        name="ret",
    )(p, p, p, p, state0, *tabs, gn_g.reshape(1, 256), gn_b.reshape(1, 256))


def _fox_prompt_kernel(q_ref, k_ref, v_ref, f_ref, fb_ref, o_ref, lf_ref, cum_ref, cumt_ref, kb_ref, vb_ref):
    qi = pl.program_id(1)
    t = k_ref.shape[1]
    qb = q_ref.shape[1]

    @pl.when(qi == 0)
    def _prep():
        lf = _log_sigmoid(f_ref[0] + fb_ref[...])
        lf_ref[0] = lf
        tri = (_iota((qb, qb), 1) <= _iota((qb, qb), 0)).astype(F32)
        carry = jnp.zeros((1, 128), F32)
        for c in range(t // qb):
            blk = _dot(tri, lf[c * qb:(c + 1) * qb], HIGHEST) + carry
            cum_ref[c * qb:(c + 1) * qb, :] = blk
            carry = blk[qb - 1:qb, :]
        cumt_ref[...] = cum_ref[...].T
        for h in range(N_HEADS):
            kb_ref[h] = k_ref[0, :, h * HEAD_DIM:(h + 1) * HEAD_DIM].astype(BF16)
            vb_ref[h] = v_ref[0, :, h * HEAD_DIM:(h + 1) * HEAD_DIM].astype(BF16)

    s0 = pl.multiple_of(qi * qb, qb)
    q = q_ref[0]
    cq = cum_ref[pl.ds(s0, qb), :]
    qrow = s0 + _iota((qb, 1), 0)
    mask = _iota((1, t), 1) <= qrow
    for h in range(N_HEADS):
        qh = q[:, h * HEAD_DIM:(h + 1) * HEAD_DIM].astype(BF16)
        s = _dot_nt(qh, kb_ref[h]) * SCALE + cq[:, h:h + 1] - cumt_ref[h:h + 1, :]
        pr = _masked_softmax(s, mask, -1)
        o_ref[0, :, h * HEAD_DIM:(h + 1) * HEAD_DIM] = _dot(pr.astype(BF16), vb_ref[h]).astype(o_ref.dtype)


def _fox_prompt_call(p, f_bias):
    b, t, _ = p.shape
    qb = QUERY_BLOCK
    fb = jnp.zeros((1, 128), F32).at[0, :N_HEADS].set(f_bias)
    return pl.pallas_call(
        _fox_prompt_kernel,
        out_shape=(jax.ShapeDtypeStruct((b, t, 256), BF16),
                   jax.ShapeDtypeStruct((b, t, 128), F32)),
        grid=(b, t // qb),
        in_specs=[pl.BlockSpec((1, qb, 256), lambda b_, i: (b_, i, P_CQ // 256)),
                  pl.BlockSpec((1, t, 256), lambda b_, i: (b_, 0, P_CK // 256)),
                  pl.BlockSpec((1, t, 256), lambda b_, i: (b_, 0, P_CV // 256)),
                  pl.BlockSpec((1, t, 128), lambda b_, i: (b_, 0, P_CF // 128)),
                  pl.BlockSpec((1, 128), lambda b_, i: (0, 0))],
        out_specs=(pl.BlockSpec((1, qb, 256), lambda b_, i: (b_, i, 0)),
                   pl.BlockSpec((1, t, 128), lambda b_, i: (b_, 0, 0))),
        scratch_shapes=[pltpu.VMEM((t, 128), F32), pltpu.VMEM((128, t), F32),
                        pltpu.VMEM((N_HEADS, t, HEAD_DIM), BF16), pltpu.VMEM((N_HEADS, t, HEAD_DIM), BF16)],
        compiler_params=_params(("parallel", "arbitrary")),
        name="fox_prompt",
    )(p, p, p, p, fb)


def _head_lanes(x):
    grp = _iota((x.shape[0], 128), 1) >> 5
    out = jnp.where(grp == 0, x[:, 0:1], x[:, 1:2])
    out = jnp.where(grp == 2, x[:, 2:3], out)
    return jnp.where(grp == 3, x[:, 3:4], out)


def _fox_sample_kernel(pt_ref, kv_ref, lfp_ref, q_ref, k_ref, v_ref, f_ref, fb_ref,
                       o_ref, lf_ref, qbd_ref, m_ref, l_ref, acc_ref, carry_ref, cnew_ref):
    p = pl.program_id(1)
    n_pages = pl.num_programs(1)
    t = q_ref.shape[1]
    page_sz = kv_ref.shape[2]
    lane = _iota((1, 128), 1)

    @pl.when(p == 0)
    def _init():
        q = q_ref[0]
        col_head = _iota((t, 256), 1) >> 6
        zpad = jnp.zeros((32 - t, 256), F32)
        qbd = jnp.concatenate(
            [piece for h in range(N_HEADS) for piece in (jnp.where(col_head == h, q, 0.0), zpad)], 0)
        qbd_ref[...] = qbd
        lf = _log_sigmoid(f_ref[0] + fb_ref[...])
        lf_ref[0] = lf
        tri = (_iota((t, t), 1) <= _iota((t, t), 0)).astype(F32)
        cs = _head_lanes(_dot(tri, lf, HIGHEST))
        cnew = jnp.sum(jnp.where(_iota((t, 128), 0) == (lane & 31), cs, 0.0), 0, keepdims=True)
        cnew_ref[...] = cnew
        carry_ref[...] = jnp.zeros((1, 128), F32)
        s = _dot_nt(k_ref[0], qbd) * SCALE + cnew - cs
        mask = _iota((t, 1), 0) <= (lane & 31)
        s = jnp.where(mask, s, NEG_BIG)
        m = jnp.max(s, 0, keepdims=True)
        e = jnp.where(mask, jnp.exp(s - m), 0.0)
        m_ref[...] = m
        l_ref[...] = jnp.sum(e, 0, keepdims=True)
        acc_ref[...] = _dot_tn(v_ref[0], e)

    kv = kv_ref[0, 0]
    lf128 = _head_lanes(lfp_ref[0, 0])
    later = (_iota((page_sz, page_sz), 1) > _iota((page_sz, page_sz), 0)).astype(F32)
    suffix = _dot(later, lf128, HIGHEST) + carry_ref[...]
    carry_ref[...] = carry_ref[...] + jnp.sum(lf128, 0, keepdims=True)
    s = _dot_nt(kv[:, 0:256].astype(BF16), qbd_ref[...].astype(BF16)) * SCALE + suffix + cnew_ref[...]
    m_old = m_ref[...]
    m_new = jnp.maximum(m_old, jnp.max(s, 0, keepdims=True))
    alpha = jnp.exp(m_old - m_new)
    e = jnp.exp(s - m_new)
    m_ref[...] = m_new
    l_ref[...] = alpha * l_ref[...] + jnp.sum(e, 0, keepdims=True)
    acc_ref[...] = alpha * acc_ref[...] + _dot_tn(kv[:, 256:512].astype(BF16), e.astype(BF16))

    @pl.when(p == n_pages - 1)
    def _finish():
        o_t = acc_ref[...] / jnp.maximum(l_ref[...], 1e-30)
        o = o_t.T
        for h in range(N_HEADS):
            sl = slice(h * HEAD_DIM, (h + 1) * HEAD_DIM)
            o_ref[0, :, sl] = o[h * 32:h * 32 + t, sl].astype(o_ref.dtype)


def _fox_sample_call(p, cache_kv, cache_lf, page_table, layer, f_bias):
    b, t, _ = p.shape
    n_pages = page_table.shape[1]
    page_sz = cache_kv.shape[2]
    assert t <= 32
    kv2 = cache_kv.reshape(cache_kv.shape[0], cache_kv.shape[1], page_sz, 512)
    fb = jnp.zeros((1, 128), F32).at[0, :N_HEADS].set(f_bias)
    rev = lambda b_, i, pt: (pt[b_, n_pages - 1 - i], layer, 0, 0)
    new = lambda width, col: pl.BlockSpec((1, t, width), lambda b_, i, pt: (b_, 0, col))
    grid_spec = pltpu.PrefetchScalarGridSpec(
        num_scalar_prefetch=1,
        grid=(b, n_pages),
        in_specs=[pl.BlockSpec((1, 1, page_sz, 512), rev),
                  pl.BlockSpec((1, 1, page_sz, N_HEADS), rev),
                  new(256, P_CQ // 256), new(256, P_CK // 256), new(256, P_CV // 256), new(128, P_CF // 128),
                  pl.BlockSpec((1, 128), lambda b_, i, pt: (0, 0))],
        out_specs=(pl.BlockSpec((1, t, 256), lambda b_, i, pt: (b_, 0, 0)),
                   pl.BlockSpec((1, t, 128), lambda b_, i, pt: (b_, 0, 0))),
        scratch_shapes=[pltpu.VMEM((128, 256), F32), pltpu.VMEM((1, 128), F32), pltpu.VMEM((1, 128), F32),
                        pltpu.VMEM((256, 128), F32), pltpu.VMEM((1, 128), F32), pltpu.VMEM((1, 128), F32)])
    return pl.pallas_call(
        _fox_sample_kernel,
        out_shape=(jax.ShapeDtypeStruct((b, t, 256), BF16),
                   jax.ShapeDtypeStruct((b, t, 128), F32)),
        grid_spec=grid_spec,
        compiler_params=_params(("parallel", "arbitrary")),
        name="fox_sample",
    )(page_table, kv2, cache_lf, p, p, p, p, fb)


def _gdn_kernel(qkv_ref, z_ref, ba_ref, cw_ref, cb_ref, s0_ref, pa_ref, ng_ref, o_ref, st_ref, xb_ref):
    ci = pl.program_id(1)
    c = qkv_ref.shape[1]
    pad = 8

    @pl.when(ci == 0)
    def _init():
        st_ref[...] = s0_ref[...]
        xb_ref[pad - (CONV_K - 1):pad, :] = cb_ref[0]

    xb_ref[pad:pad + c, :] = qkv_ref[0]
    conv = xb_ref[pad - 3:pad - 3 + c, :] * cw_ref[0:1, :]
    for j in range(1, CONV_K):
        conv = conv + xb_ref[pad - 3 + j:pad - 3 + j + c, :] * cw_ref[j:j + 1, :]
    tail = xb_ref[pad + c - (CONV_K - 1):pad + c, :]
    xb_ref[pad - (CONV_K - 1):pad, :] = tail
    conv = _silu(conv)
    ba = ba_ref[0]
    beta = _sigmoid(ba)
    g = -jnp.exp(pa_ref[0:1, :]) * _softplus(ba + pa_ref[1:2, :])
    ii = _iota((c, c), 0)
    jj = _iota((c, c), 1)
    acum = _dot((jj <= ii).astype(F32), g, HIGHEST)
    eye = ii == jj
    z = z_ref[0]
    for h in range(N_HEADS):
        sl = slice(h * HEAD_DIM, (h + 1) * HEAD_DIM)
        qh = conv[:, h * HEAD_DIM:(h + 1) * HEAD_DIM]
        kh = conv[:, 256 + h * HEAD_DIM:256 + (h + 1) * HEAD_DIM]
        vh = conv[:, 512 + h * HEAD_DIM:512 + (h + 1) * HEAD_DIM]
        qh = qh * lax.rsqrt(jnp.sum(qh * qh, -1, keepdims=True) + NORM_EPS) * SCALE
        kh = kh * lax.rsqrt(jnp.sum(kh * kh, -1, keepdims=True) + NORM_EPS)
        bcol = beta[:, h:h + 1]
        acol = acum[:, N_HEADS + h:N_HEADS + h + 1]
        arow = jnp.sum(jnp.where(eye, acol, 0.0), 0, keepdims=True)
        decay = jnp.exp(jnp.minimum(acol - arow, 0.0))
        kb = kh * bcol
        m = _dot_nt(kb, kh) * jnp.where(jj < ii, decay, 0.0)
        inv = jnp.where(eye, 1.0, 0.0) - m
        pw = _dot(m, m, HIGHEST)
        n = 2
        while n < c:
            inv = inv + _dot(inv, pw, HIGHEST)
            n *= 2
            if n < c:
                pw = _dot(pw, pw, HIGHEST)
        ea = jnp.exp(acol)
        rhs = jnp.concatenate([vh * bcol, kb * ea], 1)
        sol = _dot(inv, rhs, HIGHEST)
        s = st_ref[0, h]
        v_new = sol[:, 0:HEAD_DIM] - _dot(sol[:, HEAD_DIM:128], s)
        att = _dot_nt(qh, kh) * jnp.where(jj <= ii, decay, 0.0)
        o = _dot(qh * ea, s) + _dot(att, v_new)
        a_last = acol[c - 1:c, :]
        st_ref[0, h] = jnp.exp(a_last) * s + _dot_tn(kh * jnp.exp(a_last - acol), v_new)
        o = o * lax.rsqrt(jnp.mean(o * o, -1, keepdims=True) + NORM_EPS) * ng_ref[...]
        o_ref[0, :, sl] = (o * _silu(z[:, sl])).astype(o_ref.dtype)


def _gdn_call(p, conv_buf, state0, conv_w, a_log, dt_bias, norm_g):
    b, t, _ = p.shape
    c = min(GDN_CHUNK, t)
    assert t % c == 0 and c >= CONV_K - 1
    pa = jnp.zeros((2, 128), F32).at[0, N_HEADS:2 * N_HEADS].set(a_log).at[1, N_HEADS:2 * N_HEADS].set(dt_bias)
    st = pl.BlockSpec((1, N_HEADS, HEAD_DIM, HEAD_DIM), lambda b_, i: (b_, 0, 0, 0))
    return pl.pallas_call(
        _gdn_kernel,
        out_shape=(jax.ShapeDtypeStruct((b, t, 256), BF16),
                   jax.ShapeDtypeStruct((b, N_HEADS, HEAD_DIM, HEAD_DIM), F32)),
        grid=(b, t // c),
        in_specs=[pl.BlockSpec((1, c, 768), lambda b_, i: (b_, i, P_DQKV // 768)),
                  pl.BlockSpec((1, c, 256), lambda b_, i: (b_, i, P_DZ // 256)),
                  pl.BlockSpec((1, c, 128), lambda b_, i: (b_, i, P_DBA // 128)),
                  pl.BlockSpec((CONV_K, 768), lambda b_, i: (0, 0)),
                  pl.BlockSpec((1, CONV_K - 1, 768), lambda b_, i: (b_, 0, 0)),
                  st,
                  pl.BlockSpec((2, 128), lambda b_, i: (0, 0)),
                  pl.BlockSpec((1, HEAD_DIM), lambda b_, i: (0, 0))],
        out_specs=(pl.BlockSpec((1, c, 256), lambda b_, i: (b_, i, 0)), st),
        scratch_shapes=[pltpu.VMEM((8 + c, 768), F32)],
        compiler_params=_params(("parallel", "arbitrary")),
        name="gdn",
    )(p, p, p, conv_w, conv_buf, state0, pa, norm_g.reshape(1, HEAD_DIM))


def _outproj_kernel(oa_ref, ob_ref, oc_ref, od_ref, x_ref, g1_ref, sc2_ref, sh2_ref, w_ref,
                    l1g_ref, l1b_ref, rw_ref, rb_ref, x1_ref, u2_ref, wc_ref):
    y = _dot(oa_ref[0], w_ref[0:256, :])
    y = y + _dot(ob_ref[0], w_ref[256:512, :])
    y = y + _dot(oc_ref[0], w_ref[512:768, :])
    y = y + _dot(od_ref[0], w_ref[768:1024, :])
    x1 = _ln(DN_ALPHA * x_ref[0] + g1_ref[0] * y) * l1g_ref[...] + l1b_ref[...]
    x1_ref[0] = x1
    u2 = _ln(x1) * (1.0 + sc2_ref[0]) + sh2_ref[0]
    u2_ref[0] = u2.astype(BF16)
    scores = _sigmoid(_dot_nt(u2, rw_ref[...], HIGHEST))
    biased = scores + rb_ref[...]
    tm = biased.shape[0]
    lane = _iota((tm, N_EXPERTS), 1)
    lane_f = lane.astype(F32)
    per_group = N_EXPERTS // N_GROUPS
    grp_scores = []
    for g in range(N_GROUPS):
        in_g = (lane // per_group) == g
        vals = jnp.where(in_g, biased, -jnp.inf)
        m1 = jnp.max(vals, -1, keepdims=True)
        first = jnp.min(jnp.where(vals == m1, lane_f, float(N_EXPERTS)), -1, keepdims=True)
        m2 = jnp.max(jnp.where(lane_f == first, -jnp.inf, vals), -1, keepdims=True)
        grp_scores.append(m1 + m2)
    emask = jnp.zeros((tm, N_EXPERTS), jnp.bool_)
    for g in range(N_GROUPS):
        rank = jnp.zeros((tm, 1), F32)
        for g2 in range(N_GROUPS):
            if g2 == g:
                continue
            ahead = (grp_scores[g2] > grp_scores[g]) | ((grp_scores[g2] == grp_scores[g]) & (g2 < g))
            rank = rank + jnp.where(ahead, 1.0, 0.0)
        emask = emask | (((lane // per_group) == g) & (rank < float(TOPK_GROUPS)))
    sel = _topk_mask(jnp.where(emask, biased, -jnp.inf), TOP_K, 1)
    w = sel * scores
    wc_ref[0] = w / jnp.sum(w, -1, keepdims=True) * ROUTED_SCALE


def _outproj_call(oa, ob, oc, od, x, g1, sc2, sh2, w_out, ln_g, ln_b, rw_t, rb):
    g, r, _ = x.shape
    tm = min(256, r)
    o_spec = pl.BlockSpec((1, tm, 256), lambda g_, i: (g_, i, 0))
    x_spec = pl.BlockSpec((1, tm, D_MODEL), lambda g_, i: (g_, i, 0))
    vec = pl.BlockSpec((1, D_MODEL), lambda g_, i: (0, 0))
    return pl.pallas_call(
        _outproj_kernel,
        out_shape=(jax.ShapeDtypeStruct((g, r, D_MODEL), F32),
                   jax.ShapeDtypeStruct((g, r, D_MODEL), BF16),
                   jax.ShapeDtypeStruct((g, r, N_EXPERTS), F32)),
        grid=(g, r // tm),
        in_specs=[o_spec, o_spec, o_spec, o_spec, x_spec,
                  _mod_spec(g1, tm), _mod_spec(sc2, tm), _mod_spec(sh2, tm),
                  pl.BlockSpec((D_MODEL, D_MODEL), lambda g_, i: (0, 0)),
                  vec, vec,
                  pl.BlockSpec((N_EXPERTS, D_MODEL), lambda g_, i: (0, 0)),
                  pl.BlockSpec((1, N_EXPERTS), lambda g_, i: (0, 0))],
        out_specs=(x_spec, x_spec, pl.BlockSpec((1, tm, N_EXPERTS), lambda g_, i: (g_, i, 0))),
        compiler_params=_params(("parallel", "parallel")),
        name="outproj",
    )(oa, ob, oc, od, x, g1, sc2, sh2, w_out, ln_g.reshape(1, -1), ln_b.reshape(1, -1), rw_t, rb.reshape(1, -1))


def _swiglu_act(hid):
    return _silu(hid[:, 0:EXPERT_FF]) * hid[:, EXPERT_FF:2 * EXPERT_FF]


def _moe_kernel(u_ref, wc_ref, x_ref, g2_ref, wgu_ref, wdn_ref, sgu_ref, sdn_ref, l2g_ref, l2b_ref,
                o_ref, acc_ref):
    e = pl.program_id(2)
    u = u_ref[0]

    @pl.when(e == 0)
    def _shared():
        acc_ref[...] = _dot(_swiglu_act(_dot(u, sgu_ref[...])).astype(BF16), sdn_ref[...])

    wc = wc_ref[0]
    col = jnp.sum(jnp.where(_iota(wc.shape, 1) == e, wc, 0.0), -1, keepdims=True)
    act = _swiglu_act(_dot(u, wgu_ref[0])) * col
    acc_ref[...] += _dot(act.astype(BF16), wdn_ref[0])

    @pl.when(e == pl.num_programs(2) - 1)
    def _finish():
        o_ref[0] = _ln(DN_ALPHA * x_ref[0] + g2_ref[0] * acc_ref[...]) * l2g_ref[...] + l2b_ref[...]


def _moe_call(u2, wc, x1, g2, wgu, wdn, sgu, sdn, ln_g, ln_b):
    g, r, _ = x1.shape
    tm = min(1024, r)
    tok = lambda width: pl.BlockSpec((1, tm, width), lambda g_, i, e: (g_, i, 0))
    if g2.shape[1] == 1:
        g2_spec = pl.BlockSpec((1, 1, D_MODEL), lambda g_, i, e: (g_, 0, 0))
    else:
        g2_spec = tok(D_MODEL)
    vec = pl.BlockSpec((1, D_MODEL), lambda g_, i, e: (0, 0))
    return pl.pallas_call(
        _moe_kernel,
        out_shape=jax.ShapeDtypeStruct((g, r, D_MODEL), F32),
        grid=(g, r // tm, N_EXPERTS),
        in_specs=[tok(D_MODEL), tok(N_EXPERTS), tok(D_MODEL), g2_spec,
                  pl.BlockSpec((1, D_MODEL, 2 * EXPERT_FF), lambda g_, i, e: (e, 0, 0)),
                  pl.BlockSpec((1, EXPERT_FF, D_MODEL), lambda g_, i, e: (e, 0, 0)),
                  pl.BlockSpec((D_MODEL, 2 * EXPERT_FF), lambda g_, i, e: (0, 0)),
                  pl.BlockSpec((EXPERT_FF, D_MODEL), lambda g_, i, e: (0, 0)),
                  vec, vec],
        out_specs=tok(D_MODEL),
        scratch_shapes=[pltpu.VMEM((tm, D_MODEL), F32)],
        compiler_params=_params(("parallel", "parallel", "arbitrary")),
        name="moe",
    )(u2, wc, x1, g2, wgu, wdn, sgu, sdn, ln_g.reshape(1, -1), ln_b.reshape(1, -1))


def _run_trunk(x, mod, p0, weights, past):
    b, t, _ = x.shape
    per_token = t < 128
    if per_token:
        grp = lambda a: a.reshape(1, b * t, a.shape[-1])
        mod_rows = lambda m: jnp.repeat(m, t, axis=0)[None]
    else:
        grp = lambda a: a
        mod_rows = lambda m: m[:, None, :]
    ungrp = lambda a: a.reshape(b, t, a.shape[-1])

    outs = []
    for l in range(DEPTH):
        w = {k: v[l] for k, v in weights.items()}
        sh1, sc1, g1, sh2, sc2, g2 = [mod_rows(m) for m in jnp.split(mod[l], 6, axis=-1)]
        p = ungrp(_proj_call(grp(x), sc1, sh1, w["w_in"]))
        if past is None:
            o_a, nsa_rows, win_rows = _nsa_prompt_call(p, w["nsa_pool"])
            win_new = win_rows[:, t - min(NSA_WINDOW, t):]
            o_c, logf = _fox_prompt_call(p, w["fox_f_bias"])
            ret_s0 = jnp.zeros((b, N_HEADS, HEAD_DIM, HEAD_DIM), F32)
            gdn_s0 = ret_s0
            conv_buf = jnp.zeros((b, CONV_K - 1, 3 * GROUP_WIDTH), F32)
        else:
            o_a, nsa_rows, win_new = _nsa_sample_call(p, past["cache_nsa"], past["page_table"], l,
                                                      past["state_nsa_win"][l], w["nsa_pool"], p0)
            o_c, logf = _fox_sample_call(p, past["cache_fox_kv"], past["cache_fox_logf"], past["page_table"], l,
                                         w["fox_f_bias"])
            ret_s0, gdn_s0, conv_buf = past["state_ret"][l], past["state_gdn"][l], past["state_gdn_conv"][l]
        o_b, ret_s = _ret_call(p, ret_s0, w["ret_gn_g"], w["ret_gn_b"], p0)
        o_d, gdn_s = _gdn_call(p, conv_buf, gdn_s0, w["gdn_conv_w"], w["gdn_A_log"], w["gdn_dt_bias"],
                               w["gdn_norm_g"])
        x1, u2, wc = _outproj_call(grp(o_a), grp(o_b), grp(o_c), grp(o_d), grp(x), g1, sc2, sh2,
                                   w["w_out"], w["ln1_g"], w["ln1_b"], w["router_w_t"], w["router_b"])
        x = ungrp(_moe_call(u2, wc, x1, g2, w["exp_w_gu"], w["exp_w_down"], w["sh_w_gu"], w["sh_w_down"],
                            w["ln2_g"], w["ln2_b"]))
        qkv = p[:, :, P_DQKV:P_DQKV + 768]
        conv_new = jnp.concatenate([conv_buf, qkv], axis=1)[:, t:]
        outs.append((nsa_rows.reshape(b, t, 4, HEAD_DIM),
                     p[:, :, P_CK:P_CK + 512].reshape(b, t, 2, N_HEADS, HEAD_DIM),
                     logf[:, :, :N_HEADS],
                     win_new.reshape(b, win_new.shape[1], 2, HEAD_DIM),
                     ret_s, gdn_s, conv_new))
    nsa, fkv, flf, win, ret, gdn, conv = zip(*outs)
    return x, (jnp.stack(nsa, 1), jnp.stack(fkv, 1), jnp.stack(flf, 1), jnp.stack(win, 0),
               jnp.stack(ret, 0), jnp.stack(gdn, 0), jnp.stack(conv, 0))


def kernel(x_prompt, x_sample, cache_nsa, cache_fox_kv, cache_fox_logf, state_nsa_win, state_ret, state_gdn, state_gdn_conv, page_table, c_prompt, c_sample, w_mod, b_mod, w_in, w_out, nsa_pool, ret_gn_g, ret_gn_b, fox_f_bias, gdn_conv_w, gdn_A_log, gdn_dt_bias, gdn_norm_g, ln1_g, ln1_b, ln2_g, ln2_b, router_w, router_b, exp_w_gu, exp_w_down, sh_w_gu, sh_w_down):
    b = x_prompt.shape[0]
    past_len = page_table.shape[1] * cache_nsa.shape[2]
    perm = _proj_perm()
    w_in_p = jnp.where(jnp.asarray(perm >= 0)[None, None, :],
                       jnp.take(w_in, jnp.asarray(np.maximum(perm, 0)), axis=2), 0.0).astype(BF16)
    weights = dict(
        w_in=w_in_p, w_out=w_out.astype(BF16), nsa_pool=nsa_pool, ret_gn_g=ret_gn_g, ret_gn_b=ret_gn_b,
        fox_f_bias=fox_f_bias, gdn_conv_w=gdn_conv_w, gdn_A_log=gdn_A_log, gdn_dt_bias=gdn_dt_bias,
        gdn_norm_g=gdn_norm_g, ln1_g=ln1_g, ln1_b=ln1_b, ln2_g=ln2_g, ln2_b=ln2_b,
        router_w_t=jnp.swapaxes(router_w, 1, 2), router_b=router_b,
        exp_w_gu=exp_w_gu.astype(BF16), exp_w_down=exp_w_down.astype(BF16),
        sh_w_gu=sh_w_gu.astype(BF16), sh_w_down=sh_w_down.astype(BF16))
    n_c = b + x_sample.shape[0]
    n_pad = -n_c % 8
    c_all = jnp.concatenate([c_prompt, c_sample, jnp.zeros((n_pad, D_MODEL), F32)], axis=0)
    mod = _mod_call(c_all, w_mod, b_mod)
    past = dict(cache_nsa=cache_nsa, cache_fox_kv=cache_fox_kv, cache_fox_logf=cache_fox_logf,
                state_nsa_win=state_nsa_win, state_ret=state_ret, state_gdn=state_gdn,
                state_gdn_conv=state_gdn_conv, page_table=page_table)
    y_p, (nsa_p, fkv_p, flf_p, win_p, ret_p, gdn_p, conv_p) = _run_trunk(x_prompt, mod[:, :b], 0, weights, None)
    y_s, (nsa_s, fkv_s, flf_s, win_s, ret_s, gdn_s, conv_s) = _run_trunk(x_sample, mod[:, b:n_c], past_len, weights, past)
    return (y_p, y_s, nsa_p, nsa_s, fkv_p, fkv_s, flf_p, flf_s, win_p, win_s,
            ret_p, ret_s, gdn_p, gdn_s, conv_p, conv_s)
```

```python
import functools
import math

import numpy as np
import jax
import jax.numpy as jnp
from jax import lax
from jax.experimental import pallas as pl
from jax.experimental.pallas import tpu as pltpu

F32 = jnp.float32
BF16 = jnp.bfloat16
HIGHEST = lax.Precision.HIGHEST

D_MODEL = 1024
DEPTH = 4
HEAD_DIM = 64
N_HEADS = 4
GROUP_WIDTH = N_HEADS * HEAD_DIM
NSA_BLOCK = 64
NSA_TOPN = 8
NSA_WINDOW = 512
NSA_FORCE = 1.0e4
ROPE_THETA = 500000.0
ROPE_DIMS = HEAD_DIM // 4
RET_THETA = 10000.0
RET_CHUNK = 128
GDN_CHUNK = 64
CONV_K = 4
N_EXPERTS = 64
TOP_K = 8
N_GROUPS = 8
TOPK_GROUPS = 4
EXPERT_FF = 256
ROUTED_SCALE = 2.5
DN_ALPHA = (2 * DEPTH) ** 0.25
LN_EPS = 1e-5
NORM_EPS = 1e-6
NEG_BIG = -1e30
SCALE = HEAD_DIM ** -0.5
QUERY_BLOCK = 128
LANES = 128
VMEM_LIMIT = 56 * 1024 * 1024

IN_SPLITS = (GROUP_WIDTH, 6 * HEAD_DIM, 3 * N_HEADS,
             GROUP_WIDTH, GROUP_WIDTH, GROUP_WIDTH, GROUP_WIDTH,
             GROUP_WIDTH, GROUP_WIDTH, GROUP_WIDTH, N_HEADS,
             3 * GROUP_WIDTH, N_HEADS, N_HEADS, GROUP_WIDTH)
IN_WIDTH = sum(IN_SPLITS)

P_AQ, P_BQ, P_BK, P_BV, P_BZ = 0, 256, 512, 768, 1024
P_CQ, P_CK, P_CV, P_DZ, P_DQKV = 1280, 1536, 1792, 2048, 2304
P_AKC, P_AKS, P_AKW, P_AG, P_CF, P_DBA = 3072, 3200, 3328, 3456, 3584, 3712
P_WIDTH = 3840


def _proj_perm():
    src = np.cumsum((0,) + IN_SPLITS)
    (q_a, kv_a, g_a, q_b, k_b, v_b, z_b, q_c, k_c, v_c, f_c, qkv_d, beta_d, a_d, z_d) = [int(s) for s in src[:-1]]
    perm = -np.ones((P_WIDTH,), np.int64)

    def put(dst, start, width):
        perm[dst:dst + width] = np.arange(start, start + width)

    put(P_AQ, q_a, 256)
    put(P_AKC, kv_a, 128)
    put(P_AKS, kv_a + 128, 128)
    put(P_AKW, kv_a + 256, 128)
    for h in range(N_HEADS):
        for j in range(3):
            perm[P_AG + j * N_HEADS + h] = g_a + h * 3 + j
    put(P_BQ, q_b, 256)
    put(P_BK, k_b, 256)
    put(P_BV, v_b, 256)
    put(P_BZ, z_b, 256)
    put(P_CQ, q_c, 256)
    put(P_CK, k_c, 256)
    put(P_CV, v_c, 256)
    put(P_CF, f_c, 4)
    put(P_DQKV, qkv_d, 768)
    put(P_DZ, z_d, 256)
    put(P_DBA, beta_d, 4)
    put(P_DBA + 4, a_d, 4)
    return perm


def _rope_tables(pos, n_rot, theta, n_heads, pad_identity=0):
    half = n_rot // 2
    inv = theta ** (-np.arange(half, dtype=np.float64) / half)
    ang = np.asarray(pos, np.float64)[:, None] * inv[None, :]
    t = ang.shape[0]
    c = np.ones((t, HEAD_DIM)); sa = np.zeros((t, HEAD_DIM)); sb = np.zeros((t, HEAD_DIM))
    c[:, :half] = np.cos(ang); c[:, half:n_rot] = np.cos(ang)
    sa[:, :half] = -np.sin(ang)
    sb[:, half:n_rot] = np.sin(ang)
    c = np.tile(c, (1, n_heads)); sa = np.tile(sa, (1, n_heads)); sb = np.tile(sb, (1, n_heads))
    if pad_identity:
        c = np.concatenate([c, np.ones((t, pad_identity))], 1)
        sa = np.concatenate([sa, np.zeros((t, pad_identity))], 1)
        sb = np.concatenate([sb, np.zeros((t, pad_identity))], 1)
    return tuple(jnp.asarray(a, F32) for a in (c, sa, sb))


def _dot(a, b, prec=None):
    return jnp.dot(a, b, preferred_element_type=F32, precision=prec)


def _dot_nt(a, b, prec=None):
    return lax.dot_general(a, b, (((1,), (1,)), ((), ())), preferred_element_type=F32, precision=prec)


def _dot_tn(a, b, prec=None):
    return lax.dot_general(a, b, (((0,), (0,)), ((), ())), preferred_element_type=F32, precision=prec)


def _iota(shape, axis):
    return lax.broadcasted_iota(jnp.int32, shape, axis)


def _ln(x):
    mu = jnp.mean(x, -1, keepdims=True)
    xc = x - mu
    var = jnp.mean(xc * xc, -1, keepdims=True)
    return xc * lax.rsqrt(var + LN_EPS)


def _sigmoid(x):
    return 1.0 / (1.0 + jnp.exp(-x))


def _silu(x):
    return x * _sigmoid(x)


def _softplus(x):
    return jnp.maximum(x, 0.0) + jnp.log1p(jnp.exp(-jnp.abs(x)))


def _log_sigmoid(x):
    return -_softplus(-x)


def _rope(x, c, sa, sb, half):
    w = x.shape[-1]
    return x * c + pltpu.roll(x, w - half, 1) * sa + pltpu.roll(x, half, 1) * sb


def _masked_softmax(s, mask, axis):
    s = jnp.where(mask, s, NEG_BIG)
    e = jnp.where(mask, jnp.exp(s - jnp.max(s, axis, keepdims=True)), 0.0)
    return e / jnp.maximum(jnp.sum(e, axis, keepdims=True), 1e-30)


def _topk_mask(vals, k, axis):
    n = vals.shape[axis]
    idx = _iota(vals.shape, axis).astype(F32)
    sel = jnp.zeros(vals.shape, F32)
    work = vals
    for _ in range(k):
        m = jnp.max(work, axis, keepdims=True)
        first = jnp.min(jnp.where(work == m, idx, float(n)), axis, keepdims=True)
        pick = idx == first
        sel = jnp.where(pick, 1.0, sel)
        work = jnp.where(pick, -jnp.inf, work)
    return sel


def _pool_weights(pool_ref, n_rep):
    pl_t = pool_ref[...]
    e = jnp.exp(pl_t - jnp.max(pl_t, -1, keepdims=True))
    return e / (jnp.sum(e, -1, keepdims=True) / float(n_rep))


CAUSAL_GROUPS = 4


def _causal_branches(qi, n_qblocks, t, body):
    groups = CAUSAL_GROUPS if n_qblocks % CAUSAL_GROUPS == 0 else 1
    per = n_qblocks // groups
    for r in range(groups):
        pl.when((qi >= r * per) & (qi < (r + 1) * per))(functools.partial(body, (r + 1) * (t // groups)))


def _params(sem):
    return pltpu.CompilerParams(dimension_semantics=sem, vmem_limit_bytes=VMEM_LIMIT)


def _mod_kernel(c_ref, w_ref, b_ref, o_ref):
    o_ref[0] = _dot(c_ref[...].astype(BF16), w_ref[0].astype(BF16)) + b_ref[0]


def _mod_call(c_all, w_mod, b_mod):
    n = c_all.shape[0]
    tn = 1536
    return pl.pallas_call(
        _mod_kernel,
        out_shape=jax.ShapeDtypeStruct((DEPTH, n, 6 * D_MODEL), F32),
        grid=(DEPTH, 6 * D_MODEL // tn),
        in_specs=[pl.BlockSpec((n, D_MODEL), lambda l, j: (0, 0)),
                  pl.BlockSpec((1, D_MODEL, tn), lambda l, j: (l, 0, j)),
                  pl.BlockSpec((1, 1, tn), lambda l, j: (l, 0, j))],
        out_specs=pl.BlockSpec((1, n, tn), lambda l, j: (l, 0, j)),
        compiler_params=_params(("parallel", "parallel")),
        name="mod",
    )(c_all, w_mod, b_mod.reshape(DEPTH, 1, 6 * D_MODEL))


def _proj_kernel(x_ref, sc_ref, sh_ref, w_ref, o_ref):
    u = _ln(x_ref[0]) * (1.0 + sc_ref[0]) + sh_ref[0]
    o_ref[0] = _dot(u.astype(BF16), w_ref[...])


def _mod_spec(m, tm):
    if m.shape[1] == 1:
        return pl.BlockSpec((1, 1, D_MODEL), lambda g, i: (g, 0, 0))
    return pl.BlockSpec((1, tm, D_MODEL), lambda g, i: (g, i, 0))


def _proj_call(x, sc, sh, w):
    g, r, _ = x.shape
    tm = min(256, r)
    return pl.pallas_call(
        _proj_kernel,
        out_shape=jax.ShapeDtypeStruct((g, r, P_WIDTH), F32),
        grid=(g, r // tm),
        in_specs=[pl.BlockSpec((1, tm, D_MODEL), lambda g_, i: (g_, i, 0)),
                  _mod_spec(sc, tm), _mod_spec(sh, tm),
                  pl.BlockSpec((D_MODEL, P_WIDTH), lambda g_, i: (0, 0))],
        out_specs=pl.BlockSpec((1, tm, P_WIDTH), lambda g_, i: (g_, i, 0)),
        compiler_params=_params(("parallel", "parallel")),
        name="proj",
    )(x, sc, sh, w)


def _nsa_prompt_kernel(q_ref, g_ref, kc_ref, ks_ref, kw_ref, pool_ref,
                       qc_ref, qa_ref, qb_ref, kc_t_ref, ka_t_ref, kb_t_ref,
                       oa_ref, rows_ref, win_ref,
                       comp_ref, ksb_ref, vsb_ref, kwp_ref, vwp_ref, osel_ref):
    qi = pl.program_id(1)
    t = kc_ref.shape[1]
    nb = t // NSA_BLOCK
    qb = q_ref.shape[1]
    wnd = NSA_WINDOW

    @pl.when(qi == 0)
    def _prep():
        kcvc = kc_ref[0]
        ks_rot = _rope(ks_ref[0], kc_t_ref[...], ka_t_ref[...], kb_t_ref[...], ROPE_DIMS // 2)
        kw_rot = _rope(kw_ref[0], kc_t_ref[...], ka_t_ref[...], kb_t_ref[...], ROPE_DIMS // 2)
        rows_ref[0, :, 0:128] = kcvc
        rows_ref[0, :, 128:256] = ks_rot
        win_ref[0] = kw_rot
        ksb_ref[...] = ks_rot[:, 0:64].astype(BF16)
        vsb_ref[...] = ks_rot[:, 64:128].astype(BF16)
        kwp_ref[0:wnd, :] = jnp.zeros((wnd, HEAD_DIM), BF16)
        vwp_ref[0:wnd, :] = jnp.zeros((wnd, HEAD_DIM), BF16)
        kwp_ref[wnd:wnd + t, :] = kw_rot[:, 0:64].astype(BF16)
        vwp_ref[wnd:wnd + t, :] = kw_rot[:, 64:128].astype(BF16)
        wts = _pool_weights(pool_ref, nb)
        same = (_iota((nb, t), 1) >> 6) == _iota((nb, t), 0)
        pk = jnp.where(same, wts[0:1, :], 0.0)
        pv = jnp.where(same, wts[1:2, :], 0.0)
        ck = _dot(pk, kcvc, HIGHEST)
        cv = _dot(pv, kcvc, HIGHEST)
        comp_ref[...] = jnp.where(_iota((nb, 128), 1) < HEAD_DIM, ck, cv)

    s0 = pl.multiple_of(qi * qb, qb)
    q = q_ref[0]
    qr = _rope(q, qc_ref[...], qa_ref[...], qb_ref[...], ROPE_DIMS // 2)
    gates = _sigmoid(g_ref[0])
    comp = comp_ref[...]
    compk = comp[:, 0:HEAD_DIM]
    compv = comp[:, HEAD_DIM:128]
    qp = s0 + _iota((qb, 1), 0)
    blk = _iota((1, nb), 1)
    cmask = blk < ((qp + 1) >> 6)
    imp = jnp.zeros((qb, nb), F32)
    o_cmp = []
    for h in range(N_HEADS):
        qh = q[:, h * HEAD_DIM:(h + 1) * HEAD_DIM]
        pc = _masked_softmax(_dot_nt(qh, compk, HIGHEST) * SCALE, cmask, -1)
        imp = imp + pc
        o_cmp.append(_dot(pc, compv))
    cur = qp >> 6
    imp = jnp.where((blk == cur) | (blk == 0), NSA_FORCE, imp)
    imp = jnp.where(blk <= cur, imp, -1.0)
    sel = _topk_mask(imp, min(NSA_TOPN, nb), 1).astype(BF16)
    qr_heads = [qr[:, h * HEAD_DIM:(h + 1) * HEAD_DIM].astype(BF16) for h in range(N_HEADS)]

    def _selected(ext):
        nbe = ext // NSA_BLOCK
        expand = ((_iota((nbe, ext), 1) >> 6) == _iota((nbe, ext), 0)).astype(BF16)
        selk = _dot(sel[:, 0:nbe], expand)
        smask = (selk > 0.5) & (_iota((1, ext), 1) <= qp)
        ksb = ksb_ref[0:ext, :]
        vsb = vsb_ref[0:ext, :]
        for h in range(N_HEADS):
            ps = _masked_softmax(_dot_nt(qr_heads[h], ksb) * SCALE, smask, -1)
            osel_ref[:, h * HEAD_DIM:(h + 1) * HEAD_DIM] = _dot(ps.astype(BF16), vsb)

    _causal_branches(qi, t // qb, t, _selected)
    kw = kwp_ref[pl.ds(s0, wnd + qb), :]
    vw = vwp_ref[pl.ds(s0, wnd + qb), :]
    wpos = s0 - wnd + _iota((1, wnd + qb), 1)
    wmask = (wpos >= 0) & (wpos <= qp) & (wpos > qp - wnd)
    for h in range(N_HEADS):
        pw = _masked_softmax(_dot_nt(qr_heads[h], kw) * SCALE, wmask, -1)
        o_win = _dot(pw.astype(BF16), vw)
        o_sel = osel_ref[:, h * HEAD_DIM:(h + 1) * HEAD_DIM]
        out = (gates[:, h:h + 1] * o_cmp[h] + gates[:, N_HEADS + h:N_HEADS + h + 1] * o_sel
               + gates[:, 2 * N_HEADS + h:2 * N_HEADS + h + 1] * o_win)
        oa_ref[0, :, h * HEAD_DIM:(h + 1) * HEAD_DIM] = out.astype(oa_ref.dtype)


def _nsa_prompt_call(p, pool_l):
    b, t, _ = p.shape
    qb = QUERY_BLOCK
    nb = t // NSA_BLOCK
    pos = np.arange(t)
    q_tabs = _rope_tables(pos, ROPE_DIMS, ROPE_THETA, N_HEADS)
    k_tabs = _rope_tables(pos, ROPE_DIMS, ROPE_THETA, 1, pad_identity=HEAD_DIM)
    pool_t = jnp.tile(pool_l, (1, nb))
    full = lambda col: pl.BlockSpec((1, t, 128), lambda b_, i: (b_, 0, col))
    qtab = pl.BlockSpec((qb, 256), lambda b_, i: (i, 0))
    ktab = pl.BlockSpec((t, 128), lambda b_, i: (0, 0))
    return pl.pallas_call(
        _nsa_prompt_kernel,
        out_shape=(jax.ShapeDtypeStruct((b, t, 256), BF16),
                   jax.ShapeDtypeStruct((b, t, 256), F32),
                   jax.ShapeDtypeStruct((b, t, 128), F32)),
        grid=(b, t // qb),
        in_specs=[pl.BlockSpec((1, qb, 256), lambda b_, i: (b_, i, P_AQ // 256)),
                  pl.BlockSpec((1, qb, 128), lambda b_, i: (b_, i, P_AG // 128)),
                  full(P_AKC // 128), full(P_AKS // 128), full(P_AKW // 128),
                  pl.BlockSpec((2, t), lambda b_, i: (0, 0)),
                  qtab, qtab, qtab, ktab, ktab, ktab],
        out_specs=(pl.BlockSpec((1, qb, 256), lambda b_, i: (b_, i, 0)),
                   pl.BlockSpec((1, t, 256), lambda b_, i: (b_, 0, 0)),
                   pl.BlockSpec((1, t, 128), lambda b_, i: (b_, 0, 0))),
        scratch_shapes=[pltpu.VMEM((nb, 128), F32),
                        pltpu.VMEM((t, HEAD_DIM), BF16), pltpu.VMEM((t, HEAD_DIM), BF16),
                        pltpu.VMEM((NSA_WINDOW + t, HEAD_DIM), BF16), pltpu.VMEM((NSA_WINDOW + t, HEAD_DIM), BF16),
                        pltpu.VMEM((qb, GROUP_WIDTH), F32)],
        compiler_params=_params(("parallel", "arbitrary")),
        name="nsa_prompt",
    )(p, p, p, p, p, pool_t, *q_tabs, *k_tabs)


def _softmax2(s1, mask1, s2, mask2):
    s1 = jnp.where(mask1, s1, NEG_BIG)
    s2 = jnp.where(mask2, s2, NEG_BIG)
    m = jnp.maximum(jnp.max(s1, -1, keepdims=True), jnp.max(s2, -1, keepdims=True))
    e1 = jnp.where(mask1, jnp.exp(s1 - m), 0.0)
    e2 = jnp.where(mask2, jnp.exp(s2 - m), 0.0)
    den = jnp.maximum(jnp.sum(e1, -1, keepdims=True) + jnp.sum(e2, -1, keepdims=True), 1e-30)
    return e1, e2, den


def _nsa_sample_kernel(pt_ref, *refs, past, pps):
    page_refs = refs[:pps]
    (q_ref, g_ref, kc_ref, ks_ref, kw_ref, hist_ref, pool_ref,
     qc_ref, qa_ref, qb_ref, kc_t_ref, ka_t_ref, kb_t_ref,
     oa_ref, rows_ref, win_ref, comp_ref, kst_ref, vst_ref) = refs[pps:]
    i = pl.program_id(1)
    t = q_ref.shape[1]
    page_sz = page_refs[0].shape[3]
    n_pages = past // page_sz
    cr = comp_ref.shape[0]
    wts = _pool_weights(pool_ref, page_sz // NSA_BLOCK)
    r16 = _iota((16, page_sz), 0)
    half16 = _iota((16, page_sz), 1) >> 6
    pkv = jnp.where((r16 == half16), wts[0:1, :], 0.0) + jnp.where((r16 - 8 == half16), wts[1:2, :], 0.0)
    lane_lo = _iota((8, 128), 1) < HEAD_DIM

    @pl.when(i == 0)
    def _init():
        comp_ref[cr - 16:cr, :] = jnp.zeros((16, 128), F32)

    for j in range(pps):
        page_t = page_refs[j][0, 0]
        res = _dot_nt(pkv, page_t[0:128, :], HIGHEST)
        pg = i * pps + j
        comp_ref[pl.ds(pl.multiple_of(pg * 8, 8), 8), :] = jnp.where(lane_lo, res[0:8], res[8:16])
        col0 = pl.multiple_of(pg * page_sz, page_sz)
        kst_ref[:, pl.ds(col0, page_sz)] = page_t[128:192, :].astype(BF16)
        vst_ref[:, pl.ds(col0, page_sz)] = page_t[192:256, :].astype(BF16)

    @pl.when(i == pl.num_programs(1) - 1)
    def _finish():
        kcvc = kc_ref[0]
        ks_rot = _rope(ks_ref[0], kc_t_ref[...], ka_t_ref[...], kb_t_ref[...], ROPE_DIMS // 2)
        kw_rot = _rope(kw_ref[0], kc_t_ref[...], ka_t_ref[...], kb_t_ref[...], ROPE_DIMS // 2)
        rows_ref[0, :, 0:128] = kcvc
        rows_ref[0, :, 128:256] = ks_rot
        res_n = _dot(pkv[:, 0:t], kcvc, HIGHEST)
        comp_ref[cr - 16:cr - 8, :] = jnp.where(lane_lo, res_n[0:8], res_n[8:16])

        q = q_ref[0]
        qr = _rope(q, qc_ref[...], qa_ref[...], qb_ref[...], ROPE_DIMS // 2)
        zpad = jnp.zeros((32 - t, HEAD_DIM), F32)
        stack = lambda x: jnp.concatenate(
            [piece for h in range(N_HEADS) for piece in (x[:, h * HEAD_DIM:(h + 1) * HEAD_DIM], zpad)], 0)
        q_all = stack(q)
        qr_all = stack(qr)
        qidx = _iota((128, 1), 0) & 31
        qp = past + qidx

        comp = comp_ref[...]
        compk = comp[:, 0:HEAD_DIM]
        compv = comp[:, HEAD_DIM:128]
        ci = _iota((1, cr), 1)
        blk = 2 * (ci >> 3) + (ci & 7)
        valid = ((ci & 7) < 2) & (blk * NSA_BLOCK < past + t)
        cmask = valid & (blk < ((qp + 1) >> 6))
        pc = _masked_softmax(_dot_nt(q_all, compk, HIGHEST) * SCALE, cmask, -1)
        o_cmp = _dot(pc, compv)
        imp = pc[0:32] + pc[32:64] + pc[64:96] + pc[96:128]
        cur = qp[0:32] >> 6
        imp = jnp.where((blk == cur) | (blk == 0), NSA_FORCE, imp)
        imp = jnp.where(blk <= cur, imp, -1.0)
        imp = jnp.where(valid, imp, -2.0)
        sel32 = _topk_mask(imp, NSA_TOPN, 1)
        sel = jnp.concatenate([sel32] * N_HEADS, 0)

        lane_half = _iota((128, page_sz), 1) < NSA_BLOCK
        selk = jnp.concatenate(
            [jnp.where(lane_half, sel[:, 8 * pg:8 * pg + 1], sel[:, 8 * pg + 1:8 * pg + 2]) for pg in range(n_pages)], 1)
        smask = (selk > 0.5) & (_iota((1, past), 1) <= qp)
        new_idx = _iota((1, t), 1)
        nmask = (sel[:, cr - 16:cr - 15] > 0.5) & (new_idx <= qidx)
        s_past = _dot(qr_all.astype(BF16), kst_ref[...]) * SCALE
        s_new = _dot_nt(qr_all, ks_rot[:, 0:HEAD_DIM]) * SCALE
        e1, e2, den = _softmax2(s_past, smask, s_new, nmask)
        o_sel = (_dot_nt(e1.astype(BF16), vst_ref[...]) + _dot(e2, ks_rot[:, HEAD_DIM:128])) / den

        hist_t = hist_ref[0]
        wb = hist_t.shape[1]
        wpos = past - wb + _iota((1, wb), 1)
        hmask = (wpos >= 0) & (wpos <= qp) & (wpos > qp - NSA_WINDOW)
        wmask = (new_idx <= qidx) & (past + new_idx > qp - NSA_WINDOW)
        s_hist = _dot(qr_all, hist_t[0:HEAD_DIM, :]) * SCALE
        s_wnew = _dot_nt(qr_all, kw_rot[:, 0:HEAD_DIM]) * SCALE
        e1, e2, den = _softmax2(s_hist, hmask, s_wnew, wmask)
        o_win = (_dot_nt(e1, hist_t[HEAD_DIM:128, :]) + _dot(e2, kw_rot[:, HEAD_DIM:128])) / den
        hist_tok = hist_t.T
        win_ref[0, 0:wb - t, :] = hist_tok[t:wb, :]
        win_ref[0, wb - t:wb, :] = kw_rot

        gates = _sigmoid(g_ref[0])
        for h in range(N_HEADS):
            r = slice(h * 32, h * 32 + t)
            out = (gates[:, h:h + 1] * o_cmp[r] + gates[:, N_HEADS + h:N_HEADS + h + 1] * o_sel[r]
                   + gates[:, 2 * N_HEADS + h:2 * N_HEADS + h + 1] * o_win[r])
            oa_ref[0, :, h * HEAD_DIM:(h + 1) * HEAD_DIM] = out.astype(oa_ref.dtype)


def _pages_per_step(n_pages):
    return max(d for d in (8, 4, 2, 1) if n_pages % d == 0)


def _nsa_sample_call(p, cache_t, page_table, layer, hist_t, pool_l, past):
    b, t, _ = p.shape
    n_pages = page_table.shape[1]
    page_sz = cache_t.shape[3]
    wb = hist_t.shape[2]
    assert page_sz == 128 and t == 8 and past == n_pages * page_sz and wb == NSA_WINDOW and past >= wb
    pps = _pages_per_step(n_pages)
    pos = past + np.arange(t)
    q_tabs = _rope_tables(pos, ROPE_DIMS, ROPE_THETA, N_HEADS)
    k_tabs = _rope_tables(pos, ROPE_DIMS, ROPE_THETA, 1, pad_identity=HEAD_DIM)
    pool_t = jnp.tile(pool_l, (1, page_sz // NSA_BLOCK))
    new = lambda col: pl.BlockSpec((1, t, 128), lambda b_, i, pt: (b_, 0, col))
    const = lambda shape: pl.BlockSpec(shape, lambda b_, i, pt: (0,) * len(shape))
    page_spec = lambda j: pl.BlockSpec((1, 1, 256, page_sz), lambda b_, i, pt: (pt[b_, i * pps + j], layer, 0, 0))
    grid_spec = pltpu.PrefetchScalarGridSpec(
        num_scalar_prefetch=1,
        grid=(b, n_pages // pps),
        in_specs=[page_spec(j) for j in range(pps)] + [
            pl.BlockSpec((1, t, 256), lambda b_, i, pt: (b_, 0, P_AQ // 256)),
            new(P_AG // 128), new(P_AKC // 128), new(P_AKS // 128), new(P_AKW // 128),
            pl.BlockSpec((1, 128, wb), lambda b_, i, pt: (b_, 0, 0)),
            const((2, page_sz)),
            const((t, 256)), const((t, 256)), const((t, 256)),
            const((t, 128)), const((t, 128)), const((t, 128))],
        out_specs=(pl.BlockSpec((1, t, 256), lambda b_, i, pt: (b_, 0, 0)),
                   pl.BlockSpec((1, t, 256), lambda b_, i, pt: (b_, 0, 0)),
                   pl.BlockSpec((1, wb, 128), lambda b_, i, pt: (b_, 0, 0))),
        scratch_shapes=[pltpu.VMEM((8 * n_pages + 16, 128), F32),
                        pltpu.VMEM((HEAD_DIM, past), BF16),
                        pltpu.VMEM((HEAD_DIM, past), BF16)])
    return pl.pallas_call(
        functools.partial(_nsa_sample_kernel, past=past, pps=pps),
        out_shape=(jax.ShapeDtypeStruct((b, t, 256), BF16),
                   jax.ShapeDtypeStruct((b, t, 256), F32),
                   jax.ShapeDtypeStruct((b, wb, 128), F32)),
        grid_spec=grid_spec,
        compiler_params=_params(("parallel", "arbitrary")),
        name="nsa_sample",
    )(page_table, *([cache_t] * pps), p, p, p, p, p, hist_t, pool_t, *q_tabs, *k_tabs)


def _ret_kernel(q_ref, k_ref, v_ref, z_ref, s0_ref, c_ref, sa_ref, sb_ref, gng_ref, gnb_ref,
                o_ref, st_ref):
    ci = pl.program_id(1)
    c = q_ref.shape[1]

    @pl.when(ci == 0)
    def _init():
        st_ref[...] = s0_ref[...]

    tabs = (c_ref[...], sa_ref[...], sb_ref[...])
    q = _rope(q_ref[0], *tabs, HEAD_DIM // 2)
    k = _rope(k_ref[0], *tabs, HEAD_DIM // 2) * SCALE
    v = v_ref[0]
    z = z_ref[0]
    ii = _iota((c, c), 0)
    jj = _iota((c, c), 1)
    rowi = _iota((c, 1), 0).astype(F32)
    for h in range(N_HEADS):
        sl = slice(h * HEAD_DIM, (h + 1) * HEAD_DIM)
        lg = math.log1p(-2.0 ** (-5.0 - h))
        qh, kh, vh = q[:, sl], k[:, sl], v[:, sl]
        a = (rowi + 1.0) * lg
        dec = jnp.where(jj <= ii, jnp.exp(jnp.minimum((ii - jj).astype(F32) * lg, 0.0)), 0.0)
        s = st_ref[0, h]
        att = _dot_nt(qh, kh) * dec
        o = _dot(att, vh) + _dot(qh * jnp.exp(a), s)
        a_last = c * lg
        st_ref[0, h] = math.exp(a_last) * s + _dot_tn(kh * jnp.exp(a_last - a), vh)
        mu = jnp.mean(o, -1, keepdims=True)
        oc = o - mu
        var = jnp.mean(oc * oc, -1, keepdims=True)
        on = oc * lax.rsqrt(var + NORM_EPS) * gng_ref[:, sl] + gnb_ref[:, sl]
        o_ref[0, :, sl] = (on * _silu(z[:, sl])).astype(o_ref.dtype)


def _ret_call(p, state0, gn_g, gn_b, p0):
    b, t, _ = p.shape
    c = min(RET_CHUNK, t)
    assert t % c == 0
    tabs = _rope_tables(p0 + np.arange(t), HEAD_DIM, RET_THETA, N_HEADS)
    blk = lambda col: pl.BlockSpec((1, c, 256), lambda b_, i: (b_, i, col))
    tab = pl.BlockSpec((c, 256), lambda b_, i: (i, 0))
    st = pl.BlockSpec((1, N_HEADS, HEAD_DIM, HEAD_DIM), lambda b_, i: (b_, 0, 0, 0))
    vec = pl.BlockSpec((1, 256), lambda b_, i: (0, 0))
    return pl.pallas_call(
        _ret_kernel,
        out_shape=(jax.ShapeDtypeStruct((b, t, 256), BF16),
                   jax.ShapeDtypeStruct((b, N_HEADS, HEAD_DIM, HEAD_DIM), F32)),
        grid=(b, t // c),
        in_specs=[blk(P_BQ // 256), blk(P_BK // 256), blk(P_BV // 256), blk(P_BZ // 256), st,
                  tab, tab, tab, vec, vec],
        out_specs=(pl.BlockSpec((1, c, 256), lambda b_, i: (b_, i, 0)), st),
        compiler_params=_params(("parallel", "arbitrary")),
        name="ret",
    )(p, p, p, p, state0, *tabs, gn_g.reshape(1, 256), gn_b.reshape(1, 256))


def _fox_prompt_kernel(q_ref, k_ref, v_ref, f_ref, fb_ref, o_ref, lf_ref, cum_ref, cumt_ref, kb_ref, vb_ref):
    qi = pl.program_id(1)
    t = k_ref.shape[1]
    qb = q_ref.shape[1]

    @pl.when(qi == 0)
    def _prep():
        lf = _log_sigmoid(f_ref[0] + fb_ref[...])
        lf_ref[0] = lf
        tri = (_iota((qb, qb), 1) <= _iota((qb, qb), 0)).astype(F32)
        carry = jnp.zeros((1, 128), F32)
        for c in range(t // qb):
            blk = _dot(tri, lf[c * qb:(c + 1) * qb], HIGHEST) + carry
            cum_ref[c * qb:(c + 1) * qb, :] = blk
            carry = blk[qb - 1:qb, :]
        cumt_ref[...] = cum_ref[...].T
        for h in range(N_HEADS):
            kb_ref[h] = k_ref[0, :, h * HEAD_DIM:(h + 1) * HEAD_DIM].astype(BF16)
            vb_ref[h] = v_ref[0, :, h * HEAD_DIM:(h + 1) * HEAD_DIM].astype(BF16)

    s0 = pl.multiple_of(qi * qb, qb)
    q = q_ref[0]
    cq = cum_ref[pl.ds(s0, qb), :]
    qrow = s0 + _iota((qb, 1), 0)

    def _attend(ext):
        mask = _iota((1, ext), 1) <= qrow
        for h in range(N_HEADS):
            qh = q[:, h * HEAD_DIM:(h + 1) * HEAD_DIM].astype(BF16)
            s = _dot_nt(qh, kb_ref[h, 0:ext, :]) * SCALE + cq[:, h:h + 1] - cumt_ref[h:h + 1, 0:ext]
            pr = _masked_softmax(s, mask, -1)
            o_ref[0, :, h * HEAD_DIM:(h + 1) * HEAD_DIM] = _dot(pr.astype(BF16), vb_ref[h, 0:ext, :]).astype(o_ref.dtype)

    _causal_branches(qi, t // qb, t, _attend)


def _fox_prompt_call(p, f_bias):
    b, t, _ = p.shape
    qb = QUERY_BLOCK
    fb = jnp.zeros((1, 128), F32).at[0, :N_HEADS].set(f_bias)
    return pl.pallas_call(
        _fox_prompt_kernel,
        out_shape=(jax.ShapeDtypeStruct((b, t, 256), BF16),
                   jax.ShapeDtypeStruct((b, t, 128), F32)),
        grid=(b, t // qb),
        in_specs=[pl.BlockSpec((1, qb, 256), lambda b_, i: (b_, i, P_CQ // 256)),
                  pl.BlockSpec((1, t, 256), lambda b_, i: (b_, 0, P_CK // 256)),
                  pl.BlockSpec((1, t, 256), lambda b_, i: (b_, 0, P_CV // 256)),
                  pl.BlockSpec((1, t, 128), lambda b_, i: (b_, 0, P_CF // 128)),
                  pl.BlockSpec((1, 128), lambda b_, i: (0, 0))],
        out_specs=(pl.BlockSpec((1, qb, 256), lambda b_, i: (b_, i, 0)),
                   pl.BlockSpec((1, t, 128), lambda b_, i: (b_, 0, 0))),
        scratch_shapes=[pltpu.VMEM((t, 128), F32), pltpu.VMEM((128, t), F32),
                        pltpu.VMEM((N_HEADS, t, HEAD_DIM), BF16), pltpu.VMEM((N_HEADS, t, HEAD_DIM), BF16)],
        compiler_params=_params(("parallel", "arbitrary")),
        name="fox_prompt",
    )(p, p, p, p, fb)


def _rows_per_head(x, rows):
    return jnp.concatenate([jnp.broadcast_to(x[h:h + 1, :], (rows, x.shape[1])) for h in range(N_HEADS)], 0)


def _fox_sample_kernel(pt_ref, *refs, pps):
    kv_refs = refs[:pps]
    lf_refs = refs[pps:2 * pps]
    (q_ref, k_ref, v_ref, f_ref, fb_ref, o_ref, lf_ref,
     qbd_ref, m_ref, l_ref, acc_ref, carry_ref, cnew_ref) = refs[2 * pps:]
    i = pl.program_id(1)
    t = q_ref.shape[1]
    page_sz = kv_refs[0].shape[3]
    qidx = _iota((128, 1), 0) & 31

    @pl.when(i == 0)
    def _init():
        q = q_ref[0]
        col_head = _iota((t, 256), 1) >> 6
        zpad = jnp.zeros((32 - t, 256), F32)
        qbd = jnp.concatenate(
            [piece for h in range(N_HEADS) for piece in (jnp.where(col_head == h, q, 0.0), zpad)], 0)
        qbd_ref[...] = qbd
        lf = _log_sigmoid(f_ref[0] + fb_ref[...])
        lf_ref[0] = lf
        tri = (_iota((t, t), 1) <= _iota((t, t), 0)).astype(F32)
        cs = _dot(tri, lf, HIGHEST)
        zcol = jnp.zeros((32 - t, 1), F32)
        cnew = jnp.concatenate([piece for h in range(N_HEADS) for piece in (cs[:, h:h + 1], zcol)], 0)
        cnew_ref[...] = jnp.broadcast_to(cnew, (128, 128))
        eye = _iota((t, t), 0) == _iota((t, t), 1)
        cs_rows = jnp.concatenate(
            [jnp.broadcast_to(jnp.sum(jnp.where(eye, cs[:, h:h + 1], 0.0), 0, keepdims=True), (32, t))
             for h in range(N_HEADS)], 0)
        s = _dot_nt(qbd, k_ref[0]) * SCALE + cnew - cs_rows
        mask = _iota((1, t), 1) <= qidx
        s = jnp.where(mask, s, NEG_BIG)
        m = jnp.max(s, -1, keepdims=True)
        e = jnp.where(mask, jnp.exp(s - m), 0.0)
        m_ref[...] = jnp.broadcast_to(m, (128, 128))
        l_ref[...] = jnp.broadcast_to(jnp.sum(e, -1, keepdims=True), (128, 128))
        acc_ref[...] = _dot(e, v_ref[0])
        carry_ref[...] = jnp.zeros((8, 128), F32)

    lane = _iota((N_HEADS, page_sz), 1)
    carry = carry_ref[0:N_HEADS, :]
    qbd_bf = qbd_ref[...].astype(BF16)
    cnew = cnew_ref[:, 0:1]
    tiles = []
    for j in range(pps):
        lf_t = lf_refs[j][0, 0]
        incl = lf_t
        d = 1
        while d < page_sz:
            incl = incl + jnp.where(lane < page_sz - d, pltpu.roll(incl, page_sz - d, 1), 0.0)
            d *= 2
        bias = _rows_per_head(incl - lf_t + carry, 32)
        carry = carry + incl[:, 0:1]
        tiles.append(_dot(qbd_bf, kv_refs[j][0, 0, 0:256, :].astype(BF16)) * SCALE + bias + cnew)
    carry_ref[0:N_HEADS, :] = carry
    s = jnp.concatenate(tiles, 1)
    m_old = m_ref[:, 0:1]
    m_new = jnp.maximum(m_old, jnp.max(s, -1, keepdims=True))
    alpha = jnp.exp(m_old - m_new)
    e = jnp.exp(s - m_new)
    m_ref[...] = jnp.broadcast_to(m_new, (128, 128))
    l_ref[...] = alpha * l_ref[...] + jnp.sum(e, -1, keepdims=True)
    acc = alpha * acc_ref[...]
    for j in range(pps):
        acc = acc + _dot_nt(e[:, j * page_sz:(j + 1) * page_sz].astype(BF16),
                            kv_refs[j][0, 0, 256:512, :].astype(BF16))
    acc_ref[...] = acc

    @pl.when(i == pl.num_programs(1) - 1)
    def _finish():
        o = acc_ref[...] / jnp.maximum(l_ref[:, 0:1], 1e-30)
        for h in range(N_HEADS):
            sl = slice(h * HEAD_DIM, (h + 1) * HEAD_DIM)
            o_ref[0, :, sl] = o[h * 32:h * 32 + t, sl].astype(o_ref.dtype)


def _fox_sample_call(p, kv_t, lf_t, page_table, layer, f_bias):
    b, t, _ = p.shape
    n_pages = page_table.shape[1]
    page_sz = kv_t.shape[3]
    assert t == 8 and page_sz == 128
    pps = _pages_per_step(n_pages)
    fb = jnp.zeros((1, 128), F32).at[0, :N_HEADS].set(f_bias)
    rev = lambda j: (lambda b_, i, pt: (pt[b_, n_pages - 1 - (i * pps + j)], layer, 0, 0))
    new = lambda width, col: pl.BlockSpec((1, t, width), lambda b_, i, pt: (b_, 0, col))
    grid_spec = pltpu.PrefetchScalarGridSpec(
        num_scalar_prefetch=1,
        grid=(b, n_pages // pps),
        in_specs=[pl.BlockSpec((1, 1, 512, page_sz), rev(j)) for j in range(pps)]
        + [pl.BlockSpec((1, 1, N_HEADS, page_sz), rev(j)) for j in range(pps)]
        + [new(256, P_CQ // 256), new(256, P_CK // 256), new(256, P_CV // 256), new(128, P_CF // 128),
           pl.BlockSpec((1, 128), lambda b_, i, pt: (0, 0))],
        out_specs=(pl.BlockSpec((1, t, 256), lambda b_, i, pt: (b_, 0, 0)),
                   pl.BlockSpec((1, t, 128), lambda b_, i, pt: (b_, 0, 0))),
        scratch_shapes=[pltpu.VMEM((128, 256), F32), pltpu.VMEM((128, 128), F32), pltpu.VMEM((128, 128), F32),
                        pltpu.VMEM((128, 256), F32), pltpu.VMEM((8, 128), F32), pltpu.VMEM((128, 128), F32)])
    return pl.pallas_call(
        functools.partial(_fox_sample_kernel, pps=pps),
        out_shape=(jax.ShapeDtypeStruct((b, t, 256), BF16),
                   jax.ShapeDtypeStruct((b, t, 128), F32)),
        grid_spec=grid_spec,
        compiler_params=_params(("parallel", "arbitrary")),
        name="fox_sample",
    )(page_table, *([kv_t] * pps), *([lf_t] * pps), p, p, p, p, fb)


def _mxu(x):
    return x.astype(BF16) if x.shape[-2] % 16 == 0 else x


def _bdot(a, b):
    return lax.dot_general(_mxu(a), _mxu(b), (((2,), (1,)), ((0,), (0,))), preferred_element_type=F32)


def _bdot_nt(a, b):
    return lax.dot_general(_mxu(a), _mxu(b), (((2,), (2,)), ((0,), (0,))), preferred_element_type=F32)


def _same_block(ii, jj, size):
    shift = size.bit_length() - 1
    return (ii >> shift) == (jj >> shift)


def _gdn_kernel(qkv_ref, z_ref, ba_ref, cw_ref, cb_ref, s0_ref, pa_ref, ng_ref, o_ref, st_ref, xb_ref):
    ci = pl.program_id(1)
    nb, c = qkv_ref.shape[0], qkv_ref.shape[1]
    pad = 8

    @pl.when(ci == 0)
    def _init():
        st_ref[...] = s0_ref[...]
        xb_ref[:, pad - (CONV_K - 1):pad, :] = cb_ref[...]

    ii = _iota((1, c, c), 1)
    jj = _iota((1, c, c), 2)
    tri = (_iota((c, c), 1) <= _iota((c, c), 0)).astype(F32)
    eye = ii == jj
    qs, ks, vs, bs, acs = [], [], [], [], []
    for bi in range(nb):
        xb_ref[bi, pad:pad + c, :] = qkv_ref[bi]
        conv = xb_ref[bi, pad - 3:pad - 3 + c, :] * cw_ref[0:1, :]
        for j in range(1, CONV_K):
            conv = conv + xb_ref[bi, pad - 3 + j:pad - 3 + j + c, :] * cw_ref[j:j + 1, :]
        tail = xb_ref[bi, pad + c - (CONV_K - 1):pad + c, :]
        xb_ref[bi, pad - (CONV_K - 1):pad, :] = tail
        conv = _silu(conv)
        ba = ba_ref[bi]
        beta = _sigmoid(ba)
        g = -jnp.exp(pa_ref[0:1, :]) * _softplus(ba + pa_ref[1:2, :])
        acum = _dot(tri, g, HIGHEST)
        for h in range(N_HEADS):
            qs.append(conv[:, h * HEAD_DIM:(h + 1) * HEAD_DIM])
            ks.append(conv[:, 256 + h * HEAD_DIM:256 + (h + 1) * HEAD_DIM])
            vs.append(conv[:, 512 + h * HEAD_DIM:512 + (h + 1) * HEAD_DIM])
            bs.append(beta[:, h:h + 1])
            acs.append(acum[:, N_HEADS + h:N_HEADS + h + 1])
    q = jnp.stack(qs, 0)
    k = jnp.stack(ks, 0)
    v = jnp.stack(vs, 0)
    bcol = jnp.stack(bs, 0)
    acol = jnp.stack(acs, 0)
    q = q * lax.rsqrt(jnp.sum(q * q, -1, keepdims=True) + NORM_EPS) * SCALE
    k = k * lax.rsqrt(jnp.sum(k * k, -1, keepdims=True) + NORM_EPS)
    arow = jnp.sum(jnp.where(eye, acol, 0.0), 1, keepdims=True)
    decay = jnp.exp(jnp.minimum(acol - arow, 0.0))
    kb = k * bcol
    m = _bdot_nt(kb, k) * jnp.where(jj < ii, decay, 0.0)
    base = min(8, c)
    md = jnp.where(_same_block(ii, jj, base), m, 0.0)
    e = -md
    pw = _bdot(md, md)
    n = 2
    while n < base:
        e = e + pw + _bdot(e, pw)
        n *= 2
        if n < base:
            pw = _bdot(pw, pw)
    size = base
    while size < c:
        off = jnp.where(_same_block(ii, jj, 2 * size) & ~_same_block(ii, jj, size), m, 0.0)
        t1 = off + _bdot(e, off)
        e = e - t1 - _bdot(t1, e)
        size *= 2
    ea = jnp.exp(acol)
    rhs = jnp.concatenate([v * bcol, kb * ea], 2)
    sol = rhs + _bdot(e, rhs)
    s = st_ref[...].reshape(nb * N_HEADS, HEAD_DIM, HEAD_DIM)
    v_new = sol[:, :, 0:HEAD_DIM] - _bdot(sol[:, :, HEAD_DIM:128], s)
    att = _bdot_nt(q, k) * jnp.where(jj <= ii, decay, 0.0)
    o = _bdot(q * ea, s) + _bdot(att, v_new)
    o = o * lax.rsqrt(jnp.mean(o * o, -1, keepdims=True) + NORM_EPS) * ng_ref[...]
    a_last = acol[:, c - 1:c, :]
    kd = k * jnp.exp(a_last - acol)
    for bi in range(nb):
        z = z_ref[bi]
        for h in range(N_HEADS):
            gi = bi * N_HEADS + h
            sl = slice(h * HEAD_DIM, (h + 1) * HEAD_DIM)
            st_ref[bi, h] = jnp.exp(a_last[gi]) * s[gi] + _dot_tn(_mxu(kd[gi]), _mxu(v_new[gi]))
            o_ref[bi, :, sl] = (o[gi] * _silu(z[:, sl])).astype(o_ref.dtype)


def _gdn_call(p, conv_buf, state0, conv_w, a_log, dt_bias, norm_g):
    b, t, _ = p.shape
    c = min(GDN_CHUNK, t)
    nb = 2 if b % 2 == 0 else 1
    assert t % c == 0 and c >= CONV_K - 1
    pa = jnp.zeros((2, 128), F32).at[0, N_HEADS:2 * N_HEADS].set(a_log).at[1, N_HEADS:2 * N_HEADS].set(dt_bias)
    st = pl.BlockSpec((nb, N_HEADS, HEAD_DIM, HEAD_DIM), lambda b_, i: (b_, 0, 0, 0))
    return pl.pallas_call(
        _gdn_kernel,
        out_shape=(jax.ShapeDtypeStruct((b, t, 256), BF16),
                   jax.ShapeDtypeStruct((b, N_HEADS, HEAD_DIM, HEAD_DIM), F32)),
        grid=(b // nb, t // c),
        in_specs=[pl.BlockSpec((nb, c, 768), lambda b_, i: (b_, i, P_DQKV // 768)),
                  pl.BlockSpec((nb, c, 256), lambda b_, i: (b_, i, P_DZ // 256)),
                  pl.BlockSpec((nb, c, 128), lambda b_, i: (b_, i, P_DBA // 128)),
                  pl.BlockSpec((CONV_K, 768), lambda b_, i: (0, 0)),
                  pl.BlockSpec((nb, CONV_K - 1, 768), lambda b_, i: (b_, 0, 0)),
                  st,
                  pl.BlockSpec((2, 128), lambda b_, i: (0, 0)),
                  pl.BlockSpec((1, HEAD_DIM), lambda b_, i: (0, 0))],
        out_specs=(pl.BlockSpec((nb, c, 256), lambda b_, i: (b_, i, 0)), st),
        scratch_shapes=[pltpu.VMEM((nb, 8 + c, 768), F32)],
        compiler_params=_params(("parallel", "arbitrary")),
        name="gdn",
    )(p, p, p, conv_w, conv_buf, state0, pa, norm_g.reshape(1, HEAD_DIM))


def _outproj_kernel(oa_ref, ob_ref, oc_ref, od_ref, x_ref, g1_ref, sc2_ref, sh2_ref, w_ref,
                    l1g_ref, l1b_ref, rw_ref, rb_ref, x1_ref, u2_ref, wc_ref):
    y = _dot(oa_ref[0], w_ref[0:256, :])
    y = y + _dot(ob_ref[0], w_ref[256:512, :])
    y = y + _dot(oc_ref[0], w_ref[512:768, :])
    y = y + _dot(od_ref[0], w_ref[768:1024, :])
    x1 = _ln(DN_ALPHA * x_ref[0] + g1_ref[0] * y) * l1g_ref[...] + l1b_ref[...]
    x1_ref[0] = x1
    u2 = _ln(x1) * (1.0 + sc2_ref[0]) + sh2_ref[0]
    u2_ref[0] = u2.astype(BF16)
    scores = _sigmoid(_dot_nt(u2, rw_ref[...], HIGHEST))
    biased = scores + rb_ref[...]
    tm = biased.shape[0]
    lane = _iota((tm, N_EXPERTS), 1)
    lane_f = lane.astype(F32)
    grp = lane >> ((N_EXPERTS // N_GROUPS).bit_length() - 1)
    grp_scores = []
    for g in range(N_GROUPS):
        in_g = grp == g
        vals = jnp.where(in_g, biased, -jnp.inf)
        m1 = jnp.max(vals, -1, keepdims=True)
        first = jnp.min(jnp.where(vals == m1, lane_f, float(N_EXPERTS)), -1, keepdims=True)
        m2 = jnp.max(jnp.where(lane_f == first, -jnp.inf, vals), -1, keepdims=True)
        grp_scores.append(m1 + m2)
    emask = jnp.zeros((tm, N_EXPERTS), jnp.bool_)
    for g in range(N_GROUPS):
        rank = jnp.zeros((tm, 1), F32)
        for g2 in range(N_GROUPS):
            if g2 == g:
                continue
            ahead = (grp_scores[g2] > grp_scores[g]) | ((grp_scores[g2] == grp_scores[g]) & (g2 < g))
            rank = rank + jnp.where(ahead, 1.0, 0.0)
        emask = emask | ((grp == g) & (rank < float(TOPK_GROUPS)))
    sel = _topk_mask(jnp.where(emask, biased, -jnp.inf), TOP_K, 1)
    w = sel * scores
    wc_ref[0] = w / jnp.sum(w, -1, keepdims=True) * ROUTED_SCALE


def _outproj_call(oa, ob, oc, od, x, g1, sc2, sh2, w_out, ln_g, ln_b, rw_t, rb):
    g, r, _ = x.shape
    tm = min(256, r)
    o_spec = pl.BlockSpec((1, tm, 256), lambda g_, i: (g_, i, 0))
    x_spec = pl.BlockSpec((1, tm, D_MODEL), lambda g_, i: (g_, i, 0))
    vec = pl.BlockSpec((1, D_MODEL), lambda g_, i: (0, 0))
    return pl.pallas_call(
        _outproj_kernel,
        out_shape=(jax.ShapeDtypeStruct((g, r, D_MODEL), F32),
                   jax.ShapeDtypeStruct((g, r, D_MODEL), BF16),
                   jax.ShapeDtypeStruct((g, r, N_EXPERTS), F32)),
        grid=(g, r // tm),
        in_specs=[o_spec, o_spec, o_spec, o_spec, x_spec,
                  _mod_spec(g1, tm), _mod_spec(sc2, tm), _mod_spec(sh2, tm),
                  pl.BlockSpec((D_MODEL, D_MODEL), lambda g_, i: (0, 0)),
                  vec, vec,
                  pl.BlockSpec((N_EXPERTS, D_MODEL), lambda g_, i: (0, 0)),
                  pl.BlockSpec((1, N_EXPERTS), lambda g_, i: (0, 0))],
        out_specs=(x_spec, x_spec, pl.BlockSpec((1, tm, N_EXPERTS), lambda g_, i: (g_, i, 0))),
        compiler_params=_params(("parallel", "parallel")),
        name="outproj",
    )(oa, ob, oc, od, x, g1, sc2, sh2, w_out, ln_g.reshape(1, -1), ln_b.reshape(1, -1), rw_t, rb.reshape(1, -1))


EXPERTS_PER_STEP = 2


def _swiglu_act(hid):
    return _silu(hid[:, 0:EXPERT_FF]) * hid[:, EXPERT_FF:2 * EXPERT_FF]


def _moe_kernel(u_ref, wc_ref, x_ref, g2_ref, wgu_ref, wdn_ref, sgu_ref, sdn_ref, l2g_ref, l2b_ref,
                o_ref, acc_ref):
    step = pl.program_id(2)
    eps = wgu_ref.shape[0]
    u = u_ref[0]

    @pl.when(step == 0)
    def _shared():
        acc_ref[...] = _dot(_swiglu_act(_dot(u, sgu_ref[...])).astype(BF16), sdn_ref[...])

    wc = wc_ref[0]
    lane = _iota(wc.shape, 1)
    acts = []
    for k in range(eps):
        col = jnp.sum(jnp.where(lane == step * eps + k, wc, 0.0), -1, keepdims=True)
        acts.append((_swiglu_act(_dot(u, wgu_ref[k])) * col).astype(BF16))
    act = jnp.concatenate(acts, 1)
    acc_ref[...] += _dot(act, wdn_ref[...].reshape(eps * EXPERT_FF, D_MODEL))

    @pl.when(step == pl.num_programs(2) - 1)
    def _finish():
        o_ref[0] = _ln(DN_ALPHA * x_ref[0] + g2_ref[0] * acc_ref[...]) * l2g_ref[...] + l2b_ref[...]


def _moe_call(u2, wc, x1, g2, wgu, wdn, sgu, sdn, ln_g, ln_b):
    g, r, _ = x1.shape
    tm = min(1024, r)
    eps = EXPERTS_PER_STEP
    tok = lambda width: pl.BlockSpec((1, tm, width), lambda g_, i, e: (g_, i, 0))
    if g2.shape[1] == 1:
        g2_spec = pl.BlockSpec((1, 1, D_MODEL), lambda g_, i, e: (g_, 0, 0))
    else:
        g2_spec = tok(D_MODEL)
    vec = pl.BlockSpec((1, D_MODEL), lambda g_, i, e: (0, 0))
    return pl.pallas_call(
        _moe_kernel,
        out_shape=jax.ShapeDtypeStruct((g, r, D_MODEL), F32),
        grid=(g, r // tm, N_EXPERTS // eps),
        in_specs=[tok(D_MODEL), tok(N_EXPERTS), tok(D_MODEL), g2_spec,
                  pl.BlockSpec((eps, D_MODEL, 2 * EXPERT_FF), lambda g_, i, e: (e, 0, 0)),
                  pl.BlockSpec((eps, EXPERT_FF, D_MODEL), lambda g_, i, e: (e, 0, 0)),
                  pl.BlockSpec((D_MODEL, 2 * EXPERT_FF), lambda g_, i, e: (0, 0)),
                  pl.BlockSpec((EXPERT_FF, D_MODEL), lambda g_, i, e: (0, 0)),
                  vec, vec],
        out_specs=tok(D_MODEL),
        scratch_shapes=[pltpu.VMEM((tm, D_MODEL), F32)],
        compiler_params=_params(("parallel", "parallel", "arbitrary")),
        name="moe",
    )(u2, wc, x1, g2, wgu, wdn, sgu, sdn, ln_g.reshape(1, -1), ln_b.reshape(1, -1))


def _run_trunk(x, mod, p0, weights, past):
    b, t, _ = x.shape
    per_token = t < 128
    if per_token:
        grp = lambda a: a.reshape(1, b * t, a.shape[-1])
        mod_rows = lambda m: jnp.repeat(m, t, axis=0)[None]
    else:
        grp = lambda a: a
        mod_rows = lambda m: m[:, None, :]
    ungrp = lambda a: a.reshape(b, t, a.shape[-1])

    outs = []
    for l in range(DEPTH):
        w = {k: v[l] for k, v in weights.items()}
        sh1, sc1, g1, sh2, sc2, g2 = [mod_rows(m) for m in jnp.split(mod[l], 6, axis=-1)]
        p = ungrp(_proj_call(grp(x), sc1, sh1, w["w_in"]))
        if past is None:
            o_a, nsa_rows, win_rows = _nsa_prompt_call(p, w["nsa_pool"])
            win_new = win_rows[:, t - min(NSA_WINDOW, t):]
            o_c, logf = _fox_prompt_call(p, w["fox_f_bias"])
            ret_s0 = jnp.zeros((b, N_HEADS, HEAD_DIM, HEAD_DIM), F32)
            gdn_s0 = ret_s0
            conv_buf = jnp.zeros((b, CONV_K - 1, 3 * GROUP_WIDTH), F32)
        else:
            o_a, nsa_rows, win_new = _nsa_sample_call(p, past["nsa_t"], past["page_table"], l,
                                                      past["win_t"][l], w["nsa_pool"], p0)
            o_c, logf = _fox_sample_call(p, past["fox_kv_t"], past["fox_lf_t"], past["page_table"], l,
                                         w["fox_f_bias"])
            ret_s0, gdn_s0, conv_buf = past["state_ret"][l], past["state_gdn"][l], past["state_gdn_conv"][l]
        o_b, ret_s = _ret_call(p, ret_s0, w["ret_gn_g"], w["ret_gn_b"], p0)
        o_d, gdn_s = _gdn_call(p, conv_buf, gdn_s0, w["gdn_conv_w"], w["gdn_A_log"], w["gdn_dt_bias"],
                               w["gdn_norm_g"])
        x1, u2, wc = _outproj_call(grp(o_a), grp(o_b), grp(o_c), grp(o_d), grp(x), g1, sc2, sh2,
                                   w["w_out"], w["ln1_g"], w["ln1_b"], w["router_w_t"], w["router_b"])
        x = ungrp(_moe_call(u2, wc, x1, g2, w["exp_w_gu"], w["exp_w_down"], w["sh_w_gu"], w["sh_w_down"],
                            w["ln2_g"], w["ln2_b"]))
        qkv = p[:, :, P_DQKV:P_DQKV + 768]
        conv_new = jnp.concatenate([conv_buf, qkv], axis=1)[:, t:]
        outs.append((nsa_rows.reshape(b, t, 4, HEAD_DIM),
                     p[:, :, P_CK:P_CK + 512].reshape(b, t, 2, N_HEADS, HEAD_DIM),
                     logf[:, :, :N_HEADS],
                     win_new.reshape(b, win_new.shape[1], 2, HEAD_DIM),
                     ret_s, gdn_s, conv_new))
    nsa, fkv, flf, win, ret, gdn, conv = zip(*outs)
    return x, (jnp.stack(nsa, 1), jnp.stack(fkv, 1), jnp.stack(flf, 1), jnp.stack(win, 0),
               jnp.stack(ret, 0), jnp.stack(gdn, 0), jnp.stack(conv, 0))


def kernel(x_prompt, x_sample, cache_nsa, cache_fox_kv, cache_fox_logf, state_nsa_win, state_ret, state_gdn, state_gdn_conv, page_table, c_prompt, c_sample, w_mod, b_mod, w_in, w_out, nsa_pool, ret_gn_g, ret_gn_b, fox_f_bias, gdn_conv_w, gdn_A_log, gdn_dt_bias, gdn_norm_g, ln1_g, ln1_b, ln2_g, ln2_b, router_w, router_b, exp_w_gu, exp_w_down, sh_w_gu, sh_w_down):
    b = x_prompt.shape[0]
    db = x_sample.shape[0]
    n_pool, _, page_sz = cache_nsa.shape[:3]
    past_len = page_table.shape[1] * page_sz
    perm = _proj_perm()
    w_in_p = jnp.where(jnp.asarray(perm >= 0)[None, None, :],
                       jnp.take(w_in, jnp.asarray(np.maximum(perm, 0)), axis=2), 0.0).astype(BF16)
    weights = dict(
        w_in=w_in_p, w_out=w_out.astype(BF16), nsa_pool=nsa_pool, ret_gn_g=ret_gn_g, ret_gn_b=ret_gn_b,
        fox_f_bias=fox_f_bias, gdn_conv_w=gdn_conv_w, gdn_A_log=gdn_A_log, gdn_dt_bias=gdn_dt_bias,
        gdn_norm_g=gdn_norm_g, ln1_g=ln1_g, ln1_b=ln1_b, ln2_g=ln2_g, ln2_b=ln2_b,
        router_w_t=jnp.swapaxes(router_w, 1, 2), router_b=router_b,
        exp_w_gu=exp_w_gu.astype(BF16), exp_w_down=exp_w_down.astype(BF16),
        sh_w_gu=sh_w_gu.astype(BF16), sh_w_down=sh_w_down.astype(BF16))
    n_c = b + db
    n_pad = -n_c % 8
    c_all = jnp.concatenate([c_prompt, c_sample, jnp.zeros((n_pad, D_MODEL), F32)], axis=0)
    mod = _mod_call(c_all, w_mod, b_mod)
    past = dict(
        nsa_t=jnp.transpose(cache_nsa, (0, 1, 3, 4, 2)).reshape(n_pool, DEPTH, 4 * HEAD_DIM, page_sz),
        fox_kv_t=jnp.transpose(cache_fox_kv, (0, 1, 3, 4, 5, 2)).reshape(n_pool, DEPTH, 2 * GROUP_WIDTH, page_sz),
        fox_lf_t=jnp.transpose(cache_fox_logf, (0, 1, 3, 2)),
        win_t=jnp.transpose(state_nsa_win, (0, 1, 3, 4, 2)).reshape(DEPTH, db, 2 * HEAD_DIM, state_nsa_win.shape[2]),
        state_ret=state_ret, state_gdn=state_gdn, state_gdn_conv=state_gdn_conv, page_table=page_table)
    y_p, (nsa_p, fkv_p, flf_p, win_p, ret_p, gdn_p, conv_p) = _run_trunk(x_prompt, mod[:, :b], 0, weights, None)
    y_s, (nsa_s, fkv_s, flf_s, win_s, ret_s, gdn_s, conv_s) = _run_trunk(x_sample, mod[:, b:n_c], past_len, weights, past)
    return (y_p, y_s, nsa_p, nsa_s, fkv_p, fkv_s, flf_p, flf_s, win_p, win_s,
            ret_p, ret_s, gdn_p, gdn_s, conv_p, conv_s)
```

```python
import functools
import math

import numpy as np
import jax
import jax.numpy as jnp
from jax import lax
from jax.experimental import pallas as pl
from jax.experimental.pallas import tpu as pltpu

F32 = jnp.float32
BF16 = jnp.bfloat16
HIGHEST = lax.Precision.HIGHEST

D_MODEL = 1024
DEPTH = 4
HEAD_DIM = 64
N_HEADS = 4
GROUP_WIDTH = N_HEADS * HEAD_DIM
NSA_BLOCK = 64
NSA_TOPN = 8
NSA_WINDOW = 512
NSA_FORCE = 1.0e4
ROPE_THETA = 500000.0
ROPE_DIMS = HEAD_DIM // 4
RET_THETA = 10000.0
RET_CHUNK = 128
GDN_CHUNK = 64
CONV_K = 4
N_EXPERTS = 64
TOP_K = 8
N_GROUPS = 8
TOPK_GROUPS = 4
EXPERT_FF = 256
ROUTED_SCALE = 2.5
DN_ALPHA = (2 * DEPTH) ** 0.25
LN_EPS = 1e-5
NORM_EPS = 1e-6
NEG_BIG = -1e30
SCALE = HEAD_DIM ** -0.5
QUERY_BLOCK = 128
LANES = 128
VMEM_LIMIT = 56 * 1024 * 1024

IN_SPLITS = (GROUP_WIDTH, 6 * HEAD_DIM, 3 * N_HEADS,
             GROUP_WIDTH, GROUP_WIDTH, GROUP_WIDTH, GROUP_WIDTH,
             GROUP_WIDTH, GROUP_WIDTH, GROUP_WIDTH, N_HEADS,
             3 * GROUP_WIDTH, N_HEADS, N_HEADS, GROUP_WIDTH)
IN_WIDTH = sum(IN_SPLITS)

P_AQ, P_BQ, P_BK, P_BV, P_BZ = 0, 256, 512, 768, 1024
P_CQ, P_CK, P_CV, P_DZ, P_DQKV = 1280, 1536, 1792, 2048, 2304
P_AKC, P_AKS, P_AKW, P_AG, P_CF, P_DBA = 3072, 3200, 3328, 3456, 3584, 3712
P_WIDTH = 3840


def _proj_perm():
    src = np.cumsum((0,) + IN_SPLITS)
    (q_a, kv_a, g_a, q_b, k_b, v_b, z_b, q_c, k_c, v_c, f_c, qkv_d, beta_d, a_d, z_d) = [int(s) for s in src[:-1]]
    perm = -np.ones((P_WIDTH,), np.int64)

    def put(dst, start, width):
        perm[dst:dst + width] = np.arange(start, start + width)

    put(P_AQ, q_a, 256)
    put(P_AKC, kv_a, 128)
    put(P_AKS, kv_a + 128, 128)
    put(P_AKW, kv_a + 256, 128)
    for h in range(N_HEADS):
        for j in range(3):
            perm[P_AG + j * N_HEADS + h] = g_a + h * 3 + j
    put(P_BQ, q_b, 256)
    put(P_BK, k_b, 256)
    put(P_BV, v_b, 256)
    put(P_BZ, z_b, 256)
    put(P_CQ, q_c, 256)
    put(P_CK, k_c, 256)
    put(P_CV, v_c, 256)
    put(P_CF, f_c, 4)
    put(P_DQKV, qkv_d, 768)
    put(P_DZ, z_d, 256)
    put(P_DBA, beta_d, 4)
    put(P_DBA + 4, a_d, 4)
    return perm


def _rope_tables(pos, n_rot, theta, n_heads, pad_identity=0):
    half = n_rot // 2
    inv = theta ** (-np.arange(half, dtype=np.float64) / half)
    ang = np.asarray(pos, np.float64)[:, None] * inv[None, :]
    t = ang.shape[0]
    c = np.ones((t, HEAD_DIM)); sa = np.zeros((t, HEAD_DIM)); sb = np.zeros((t, HEAD_DIM))
    c[:, :half] = np.cos(ang); c[:, half:n_rot] = np.cos(ang)
    sa[:, :half] = -np.sin(ang)
    sb[:, half:n_rot] = np.sin(ang)
    c = np.tile(c, (1, n_heads)); sa = np.tile(sa, (1, n_heads)); sb = np.tile(sb, (1, n_heads))
    if pad_identity:
        c = np.concatenate([c, np.ones((t, pad_identity))], 1)
        sa = np.concatenate([sa, np.zeros((t, pad_identity))], 1)
        sb = np.concatenate([sb, np.zeros((t, pad_identity))], 1)
    return tuple(jnp.asarray(a, F32) for a in (c, sa, sb))


def _dot(a, b, prec=None):
    return jnp.dot(a, b, preferred_element_type=F32, precision=prec)


def _dot_nt(a, b, prec=None):
    return lax.dot_general(a, b, (((1,), (1,)), ((), ())), preferred_element_type=F32, precision=prec)


def _dot_tn(a, b, prec=None):
    return lax.dot_general(a, b, (((0,), (0,)), ((), ())), preferred_element_type=F32, precision=prec)


def _iota(shape, axis):
    return lax.broadcasted_iota(jnp.int32, shape, axis)


def _ln(x):
    mu = jnp.mean(x, -1, keepdims=True)
    xc = x - mu
    var = jnp.mean(xc * xc, -1, keepdims=True)
    return xc * lax.rsqrt(var + LN_EPS)


def _sigmoid(x):
    return 1.0 / (1.0 + jnp.exp(-x))


def _silu(x):
    return x * _sigmoid(x)


def _softplus(x):
    return jnp.maximum(x, 0.0) + jnp.log1p(jnp.exp(-jnp.abs(x)))


def _log_sigmoid(x):
    return -_softplus(-x)


def _rope(x, c, sa, sb, half):
    w = x.shape[-1]
    return x * c + pltpu.roll(x, w - half, 1) * sa + pltpu.roll(x, half, 1) * sb


def _masked_softmax(s, mask, axis):
    s = jnp.where(mask, s, NEG_BIG)
    e = jnp.where(mask, jnp.exp(s - jnp.max(s, axis, keepdims=True)), 0.0)
    return e / jnp.maximum(jnp.sum(e, axis, keepdims=True), 1e-30)


def _softmax_pv(s, mask, v):
    s = jnp.where(mask, s, NEG_BIG)
    m = jnp.max(s, -1, keepdims=True)
    e = jnp.exp(s - m)
    den = jnp.sum(e, -1, keepdims=True)
    inv = jnp.where(m > 0.5 * NEG_BIG, 1.0 / jnp.maximum(den, 1e-30), 0.0)
    return _dot(e.astype(BF16), v) * inv


def _topk_mask(vals, k, axis):
    n = vals.shape[axis]
    idx = _iota(vals.shape, axis).astype(F32)
    sel = jnp.zeros(vals.shape, F32)
    work = vals
    for _ in range(k):
        m = jnp.max(work, axis, keepdims=True)
        first = jnp.min(jnp.where(work == m, idx, float(n)), axis, keepdims=True)
        pick = idx == first
        sel = jnp.where(pick, 1.0, sel)
        work = jnp.where(pick, -jnp.inf, work)
    return sel


def _pool_weights(pool_ref, n_rep):
    pl_t = pool_ref[...]
    e = jnp.exp(pl_t - jnp.max(pl_t, -1, keepdims=True))
    return e / (jnp.sum(e, -1, keepdims=True) / float(n_rep))


CAUSAL_GROUPS = 4


def _causal_branches(qi, n_qblocks, t, body):
    groups = CAUSAL_GROUPS if n_qblocks % CAUSAL_GROUPS == 0 else 1
    per = n_qblocks // groups
    for r in range(groups):
        pl.when((qi >= r * per) & (qi < (r + 1) * per))(functools.partial(body, (r + 1) * (t // groups)))


def _params(sem):
    return pltpu.CompilerParams(dimension_semantics=sem, vmem_limit_bytes=VMEM_LIMIT)


def _mod_kernel(c_ref, w_ref, b_ref, o_ref):
    o_ref[0] = _dot(c_ref[...].astype(BF16), w_ref[0].astype(BF16)) + b_ref[0]


def _mod_call(c_all, w_mod, b_mod):
    n = c_all.shape[0]
    tn = 1536
    return pl.pallas_call(
        _mod_kernel,
        out_shape=jax.ShapeDtypeStruct((DEPTH, n, 6 * D_MODEL), F32),
        grid=(DEPTH, 6 * D_MODEL // tn),
        in_specs=[pl.BlockSpec((n, D_MODEL), lambda l, j: (0, 0)),
                  pl.BlockSpec((1, D_MODEL, tn), lambda l, j: (l, 0, j)),
                  pl.BlockSpec((1, 1, tn), lambda l, j: (l, 0, j))],
        out_specs=pl.BlockSpec((1, n, tn), lambda l, j: (l, 0, j)),
        compiler_params=_params(("parallel", "parallel")),
        name="mod",
    )(c_all, w_mod, b_mod.reshape(DEPTH, 1, 6 * D_MODEL))


def _proj_kernel(x_ref, sc_ref, sh_ref, w_ref, o_ref):
    u = _ln(x_ref[0]) * (1.0 + sc_ref[0]) + sh_ref[0]
    o_ref[0] = _dot(u.astype(BF16), w_ref[...])


def _mod_spec(m, tm):
    if m.shape[1] == 1:
        return pl.BlockSpec((1, 1, D_MODEL), lambda g, i: (g, 0, 0))
    return pl.BlockSpec((1, tm, D_MODEL), lambda g, i: (g, i, 0))


def _proj_call(x, sc, sh, w):
    g, r, _ = x.shape
    tm = min(256, r)
    return pl.pallas_call(
        _proj_kernel,
        out_shape=jax.ShapeDtypeStruct((g, r, P_WIDTH), F32),
        grid=(g, r // tm),
        in_specs=[pl.BlockSpec((1, tm, D_MODEL), lambda g_, i: (g_, i, 0)),
                  _mod_spec(sc, tm), _mod_spec(sh, tm),
                  pl.BlockSpec((D_MODEL, P_WIDTH), lambda g_, i: (0, 0))],
        out_specs=pl.BlockSpec((1, tm, P_WIDTH), lambda g_, i: (g_, i, 0)),
        compiler_params=_params(("parallel", "parallel")),
        name="proj",
    )(x, sc, sh, w)


def _nsa_prompt_kernel(q_ref, g_ref, kc_ref, ks_ref, kw_ref, pool_ref,
                       qc_ref, qa_ref, qb_ref, kc_t_ref, ka_t_ref, kb_t_ref,
                       oa_ref, rows_ref, win_ref,
                       comp_ref, ksb_ref, vsb_ref, kwp_ref, vwp_ref, osel_ref):
    qi = pl.program_id(1)
    t = kc_ref.shape[1]
    nb = t // NSA_BLOCK
    qb = q_ref.shape[1]
    wnd = NSA_WINDOW

    @pl.when(qi == 0)
    def _prep():
        kcvc = kc_ref[0]
        ks_rot = _rope(ks_ref[0], kc_t_ref[...], ka_t_ref[...], kb_t_ref[...], ROPE_DIMS // 2)
        kw_rot = _rope(kw_ref[0], kc_t_ref[...], ka_t_ref[...], kb_t_ref[...], ROPE_DIMS // 2)
        rows_ref[0, :, 0:128] = kcvc
        rows_ref[0, :, 128:256] = ks_rot
        win_ref[0] = kw_rot
        ksb_ref[...] = ks_rot[:, 0:64].astype(BF16)
        vsb_ref[...] = ks_rot[:, 64:128].astype(BF16)
        kwp_ref[0:wnd, :] = jnp.zeros((wnd, HEAD_DIM), BF16)
        vwp_ref[0:wnd, :] = jnp.zeros((wnd, HEAD_DIM), BF16)
        kwp_ref[wnd:wnd + t, :] = kw_rot[:, 0:64].astype(BF16)
        vwp_ref[wnd:wnd + t, :] = kw_rot[:, 64:128].astype(BF16)
        wts = _pool_weights(pool_ref, nb)
        same = (_iota((nb, t), 1) >> 6) == _iota((nb, t), 0)
        pk = jnp.where(same, wts[0:1, :], 0.0)
        pv = jnp.where(same, wts[1:2, :], 0.0)
        ck = _dot(pk, kcvc, HIGHEST)
        cv = _dot(pv, kcvc, HIGHEST)
        comp_ref[...] = jnp.where(_iota((nb, 128), 1) < HEAD_DIM, ck, cv)

    s0 = pl.multiple_of(qi * qb, qb)
    q = q_ref[0] * SCALE
    qr = _rope(q, qc_ref[...], qa_ref[...], qb_ref[...], ROPE_DIMS // 2)
    gates = _sigmoid(g_ref[0])
    comp = comp_ref[...]
    compk = comp[:, 0:HEAD_DIM]
    compv = comp[:, HEAD_DIM:128]
    qp = s0 + _iota((qb, 1), 0)
    qp_row = s0 + _iota((1, qb), 1)
    blk = _iota((nb, 1), 0)
    cmask = blk < ((qp_row + 1) >> 6)
    imp = jnp.zeros((nb, qb), F32)
    o_cmp = []
    for h in range(N_HEADS):
        qh = q[:, h * HEAD_DIM:(h + 1) * HEAD_DIM]
        pc = _masked_softmax(_dot_nt(compk, qh, HIGHEST), cmask, 0)
        imp = imp + pc
        o_cmp.append(_dot_tn(pc, compv))
    cur = qp_row >> 6
    imp = jnp.where((blk == cur) | (blk == 0), NSA_FORCE, imp)
    imp = jnp.where(blk <= cur, imp, -1.0)
    sel = _topk_mask(imp, min(NSA_TOPN, nb), 0)
    qr_heads = [qr[:, h * HEAD_DIM:(h + 1) * HEAD_DIM].astype(BF16) for h in range(N_HEADS)]

    def _selected(ext):
        expand = ((_iota((nb, ext), 1) >> 6) == _iota((nb, ext), 0)).astype(F32)
        selk = _dot_tn(sel, expand)
        smask = (selk > 0.5) & (_iota((1, ext), 1) <= qp)
        ksb = ksb_ref[0:ext, :]
        vsb = vsb_ref[0:ext, :]
        for h in range(N_HEADS):
            osel_ref[:, h * HEAD_DIM:(h + 1) * HEAD_DIM] = _softmax_pv(_dot_nt(qr_heads[h], ksb), smask, vsb)

    _causal_branches(qi, t // qb, t, _selected)
    kw = kwp_ref[pl.ds(s0, wnd + qb), :]
    vw = vwp_ref[pl.ds(s0, wnd + qb), :]
    wpos = s0 - wnd + _iota((1, wnd + qb), 1)
    wmask = (wpos >= 0) & (wpos <= qp) & (wpos > qp - wnd)
    for h in range(N_HEADS):
        o_win = _softmax_pv(_dot_nt(qr_heads[h], kw), wmask, vw)
        o_sel = osel_ref[:, h * HEAD_DIM:(h + 1) * HEAD_DIM]
        out = (gates[:, h:h + 1] * o_cmp[h] + gates[:, N_HEADS + h:N_HEADS + h + 1] * o_sel
               + gates[:, 2 * N_HEADS + h:2 * N_HEADS + h + 1] * o_win)
        oa_ref[0, :, h * HEAD_DIM:(h + 1) * HEAD_DIM] = out.astype(oa_ref.dtype)


def _nsa_prompt_call(p, pool_l):
    b, t, _ = p.shape
    qb = QUERY_BLOCK
    nb = t // NSA_BLOCK
    pos = np.arange(t)
    q_tabs = _rope_tables(pos, ROPE_DIMS, ROPE_THETA, N_HEADS)
    k_tabs = _rope_tables(pos, ROPE_DIMS, ROPE_THETA, 1, pad_identity=HEAD_DIM)
    pool_t = jnp.tile(pool_l, (1, nb))
    full = lambda col: pl.BlockSpec((1, t, 128), lambda b_, i: (b_, 0, col))
    qtab = pl.BlockSpec((qb, 256), lambda b_, i: (i, 0))
    ktab = pl.BlockSpec((t, 128), lambda b_, i: (0, 0))
    return pl.pallas_call(
        _nsa_prompt_kernel,
        out_shape=(jax.ShapeDtypeStruct((b, t, 256), BF16),
                   jax.ShapeDtypeStruct((b, t, 256), F32),
                   jax.ShapeDtypeStruct((b, t, 128), F32)),
        grid=(b, t // qb),
        in_specs=[pl.BlockSpec((1, qb, 256), lambda b_, i: (b_, i, P_AQ // 256)),
                  pl.BlockSpec((1, qb, 128), lambda b_, i: (b_, i, P_AG // 128)),
                  full(P_AKC // 128), full(P_AKS // 128), full(P_AKW // 128),
                  pl.BlockSpec((2, t), lambda b_, i: (0, 0)),
                  qtab, qtab, qtab, ktab, ktab, ktab],
        out_specs=(pl.BlockSpec((1, qb, 256), lambda b_, i: (b_, i, 0)),
                   pl.BlockSpec((1, t, 256), lambda b_, i: (b_, 0, 0)),
                   pl.BlockSpec((1, t, 128), lambda b_, i: (b_, 0, 0))),
        scratch_shapes=[pltpu.VMEM((nb, 128), F32),
                        pltpu.VMEM((t, HEAD_DIM), BF16), pltpu.VMEM((t, HEAD_DIM), BF16),
                        pltpu.VMEM((NSA_WINDOW + t, HEAD_DIM), BF16), pltpu.VMEM((NSA_WINDOW + t, HEAD_DIM), BF16),
                        pltpu.VMEM((qb, GROUP_WIDTH), F32)],
        compiler_params=_params(("parallel", "arbitrary")),
        name="nsa_prompt",
    )(p, p, p, p, p, pool_t, *q_tabs, *k_tabs)


def _softmax2(s1, mask1, s2, mask2):
    s1 = jnp.where(mask1, s1, NEG_BIG)
    s2 = jnp.where(mask2, s2, NEG_BIG)
    m = jnp.maximum(jnp.max(s1, -1, keepdims=True), jnp.max(s2, -1, keepdims=True))
    e1 = jnp.where(mask1, jnp.exp(s1 - m), 0.0)
    e2 = jnp.where(mask2, jnp.exp(s2 - m), 0.0)
    den = jnp.maximum(jnp.sum(e1, -1, keepdims=True) + jnp.sum(e2, -1, keepdims=True), 1e-30)
    return e1, e2, den


def _nsa_sample_kernel(pt_ref, *refs, past, pps):
    page_refs = refs[:pps]
    (q_ref, g_ref, kc_ref, ks_ref, kw_ref, hist_ref, pool_ref,
     qc_ref, qa_ref, qb_ref, kc_t_ref, ka_t_ref, kb_t_ref,
     oa_ref, rows_ref, win_ref, comp_ref, kst_ref, vst_ref) = refs[pps:]
    i = pl.program_id(1)
    t = q_ref.shape[1]
    page_sz = page_refs[0].shape[3]
    n_pages = past // page_sz
    cr = comp_ref.shape[0]
    wts = _pool_weights(pool_ref, page_sz // NSA_BLOCK)
    r16 = _iota((16, page_sz), 0)
    half16 = _iota((16, page_sz), 1) >> 6
    pkv = jnp.where((r16 == half16), wts[0:1, :], 0.0) + jnp.where((r16 - 8 == half16), wts[1:2, :], 0.0)
    lane_lo = _iota((8, 128), 1) < HEAD_DIM

    @pl.when(i == 0)
    def _init():
        comp_ref[cr - 16:cr, :] = jnp.zeros((16, 128), F32)

    pkv_hi = pkv.astype(BF16)
    pkv_split = jnp.concatenate([pkv_hi, (pkv - pkv_hi.astype(F32)).astype(BF16)], 0)
    for j in range(pps):
        page_t = page_refs[j][0, 0]
        x = page_t[0:128, :]
        x_hi = x.astype(BF16)
        x_lo = (x - x_hi.astype(F32)).astype(BF16)
        r_hi = _dot_nt(pkv_split, x_hi)
        res = r_hi[0:16] + r_hi[16:32] + _dot_nt(pkv_hi, x_lo)
        pg = i * pps + j
        comp_ref[pl.ds(pl.multiple_of(pg * 8, 8), 8), :] = jnp.where(lane_lo, res[0:8], res[8:16])
        col0 = pl.multiple_of(pg * page_sz, page_sz)
        kst_ref[:, pl.ds(col0, page_sz)] = page_t[128:192, :].astype(BF16)
        vst_ref[:, pl.ds(col0, page_sz)] = page_t[192:256, :].astype(BF16)

    @pl.when(i == pl.num_programs(1) - 1)
    def _finish():
        kcvc = kc_ref[0]
        ks_rot = _rope(ks_ref[0], kc_t_ref[...], ka_t_ref[...], kb_t_ref[...], ROPE_DIMS // 2)
        kw_rot = _rope(kw_ref[0], kc_t_ref[...], ka_t_ref[...], kb_t_ref[...], ROPE_DIMS // 2)
        rows_ref[0, :, 0:128] = kcvc
        rows_ref[0, :, 128:256] = ks_rot
        res_n = _dot(pkv[:, 0:t], kcvc, HIGHEST)
        comp_ref[cr - 16:cr - 8, :] = jnp.where(lane_lo, res_n[0:8], res_n[8:16])

        q = q_ref[0]
        qr = _rope(q, qc_ref[...], qa_ref[...], qb_ref[...], ROPE_DIMS // 2)
        zpad = jnp.zeros((32 - t, HEAD_DIM), F32)
        stack = lambda x: jnp.concatenate(
            [piece for h in range(N_HEADS) for piece in (x[:, h * HEAD_DIM:(h + 1) * HEAD_DIM], zpad)], 0)
        q_all = stack(q)
        qr_all = stack(qr)
        qidx = _iota((128, 1), 0) & 31
        qp = past + qidx

        comp = comp_ref[...]
        compk = comp[:, 0:HEAD_DIM]
        compv = comp[:, HEAD_DIM:128]
        ci = _iota((1, cr), 1)
        blk = 2 * (ci >> 3) + (ci & 7)
        valid = ((ci & 7) < 2) & (blk * NSA_BLOCK < past + t)
        cmask = valid & (blk < ((qp + 1) >> 6))
        pc = _masked_softmax(_dot_nt(q_all, compk, HIGHEST) * SCALE, cmask, -1)
        o_cmp = _dot(pc, compv)
        imp = pc[0:32] + pc[32:64] + pc[64:96] + pc[96:128]
        cur = qp[0:32] >> 6
        imp = jnp.where((blk == cur) | (blk == 0), NSA_FORCE, imp)
        imp = jnp.where(blk <= cur, imp, -1.0)
        imp = jnp.where(valid, imp, -2.0)
        sel32 = _topk_mask(imp, NSA_TOPN, 1)
        sel = jnp.concatenate([sel32] * N_HEADS, 0)

        lane_half = _iota((128, page_sz), 1) < NSA_BLOCK
        selk = jnp.concatenate(
            [jnp.where(lane_half, sel[:, 8 * pg:8 * pg + 1], sel[:, 8 * pg + 1:8 * pg + 2]) for pg in range(n_pages)], 1)
        smask = (selk > 0.5) & (_iota((1, past), 1) <= qp)
        new_idx = _iota((1, t), 1)
        nmask = (sel[:, cr - 16:cr - 15] > 0.5) & (new_idx <= qidx)
        s_past = _dot(qr_all.astype(BF16), kst_ref[...]) * SCALE
        s_new = _dot_nt(qr_all, ks_rot[:, 0:HEAD_DIM]) * SCALE
        e1, e2, den = _softmax2(s_past, smask, s_new, nmask)
        o_sel = (_dot_nt(e1.astype(BF16), vst_ref[...]) + _dot(e2, ks_rot[:, HEAD_DIM:128])) / den

        hist_t = hist_ref[0]
        wb = hist_t.shape[1]
        wpos = past - wb + _iota((1, wb), 1)
        hmask = (wpos >= 0) & (wpos <= qp) & (wpos > qp - NSA_WINDOW)
        wmask = (new_idx <= qidx) & (past + new_idx > qp - NSA_WINDOW)
        s_hist = _dot(qr_all, hist_t[0:HEAD_DIM, :]) * SCALE
        s_wnew = _dot_nt(qr_all, kw_rot[:, 0:HEAD_DIM]) * SCALE
        e1, e2, den = _softmax2(s_hist, hmask, s_wnew, wmask)
        o_win = (_dot_nt(e1, hist_t[HEAD_DIM:128, :]) + _dot(e2, kw_rot[:, HEAD_DIM:128])) / den
        hist_tok = hist_t.T
        win_ref[0, 0:wb - t, :] = hist_tok[t:wb, :]
        win_ref[0, wb - t:wb, :] = kw_rot

        gates = _sigmoid(g_ref[0])
        for h in range(N_HEADS):
            r = slice(h * 32, h * 32 + t)
            out = (gates[:, h:h + 1] * o_cmp[r] + gates[:, N_HEADS + h:N_HEADS + h + 1] * o_sel[r]
                   + gates[:, 2 * N_HEADS + h:2 * N_HEADS + h + 1] * o_win[r])
            oa_ref[0, :, h * HEAD_DIM:(h + 1) * HEAD_DIM] = out.astype(oa_ref.dtype)


def _pages_per_step(n_pages):
    return max(d for d in (8, 4, 2, 1) if n_pages % d == 0)


def _nsa_sample_call(p, cache_t, page_table, layer, hist_t, pool_l, past):
    b, t, _ = p.shape
    n_pages = page_table.shape[1]
    page_sz = cache_t.shape[3]
    wb = hist_t.shape[2]
    assert page_sz == 128 and t == 8 and past == n_pages * page_sz and wb == NSA_WINDOW and past >= wb
    pps = _pages_per_step(n_pages)
    pos = past + np.arange(t)
    q_tabs = _rope_tables(pos, ROPE_DIMS, ROPE_THETA, N_HEADS)
    k_tabs = _rope_tables(pos, ROPE_DIMS, ROPE_THETA, 1, pad_identity=HEAD_DIM)
    pool_t = jnp.tile(pool_l, (1, page_sz // NSA_BLOCK))
    new = lambda col: pl.BlockSpec((1, t, 128), lambda b_, i, pt: (b_, 0, col))
    const = lambda shape: pl.BlockSpec(shape, lambda b_, i, pt: (0,) * len(shape))
    page_spec = lambda j: pl.BlockSpec((1, 1, 256, page_sz), lambda b_, i, pt: (pt[b_, i * pps + j], layer, 0, 0))
    grid_spec = pltpu.PrefetchScalarGridSpec(
        num_scalar_prefetch=1,
        grid=(b, n_pages // pps),
        in_specs=[page_spec(j) for j in range(pps)] + [
            pl.BlockSpec((1, t, 256), lambda b_, i, pt: (b_, 0, P_AQ // 256)),
            new(P_AG // 128), new(P_AKC // 128), new(P_AKS // 128), new(P_AKW // 128),
            pl.BlockSpec((1, 128, wb), lambda b_, i, pt: (b_, 0, 0)),
            const((2, page_sz)),
            const((t, 256)), const((t, 256)), const((t, 256)),
            const((t, 128)), const((t, 128)), const((t, 128))],
        out_specs=(pl.BlockSpec((1, t, 256), lambda b_, i, pt: (b_, 0, 0)),
                   pl.BlockSpec((1, t, 256), lambda b_, i, pt: (b_, 0, 0)),
                   pl.BlockSpec((1, wb, 128), lambda b_, i, pt: (b_, 0, 0))),
        scratch_shapes=[pltpu.VMEM((8 * n_pages + 16, 128), F32),
                        pltpu.VMEM((HEAD_DIM, past), BF16),
                        pltpu.VMEM((HEAD_DIM, past), BF16)])
    return pl.pallas_call(
        functools.partial(_nsa_sample_kernel, past=past, pps=pps),
        out_shape=(jax.ShapeDtypeStruct((b, t, 256), BF16),
                   jax.ShapeDtypeStruct((b, t, 256), F32),
                   jax.ShapeDtypeStruct((b, wb, 128), F32)),
        grid_spec=grid_spec,
        compiler_params=_params(("parallel", "arbitrary")),
        name="nsa_sample",
    )(page_table, *([cache_t] * pps), p, p, p, p, p, hist_t, pool_t, *q_tabs, *k_tabs)


def _ret_kernel(q_ref, k_ref, v_ref, z_ref, s0_ref, c_ref, sa_ref, sb_ref, gng_ref, gnb_ref,
                o_ref, st_ref):
    ci = pl.program_id(1)
    c = q_ref.shape[1]

    @pl.when(ci == 0)
    def _init():
        st_ref[...] = s0_ref[...]

    tabs = (c_ref[...], sa_ref[...], sb_ref[...])
    q = _rope(q_ref[0], *tabs, HEAD_DIM // 2)
    k = _rope(k_ref[0], *tabs, HEAD_DIM // 2) * SCALE
    v = v_ref[0]
    z = z_ref[0]
    ii = _iota((c, c), 0)
    jj = _iota((c, c), 1)
    rowi = _iota((c, 1), 0).astype(F32)
    for h in range(N_HEADS):
        sl = slice(h * HEAD_DIM, (h + 1) * HEAD_DIM)
        lg = math.log1p(-2.0 ** (-5.0 - h))
        qh, kh, vh = q[:, sl], k[:, sl], v[:, sl]
        a = (rowi + 1.0) * lg
        dec = jnp.where(jj <= ii, jnp.exp(jnp.minimum((ii - jj).astype(F32) * lg, 0.0)), 0.0)
        s = st_ref[0, h]
        att = _dot_nt(qh, kh) * dec
        o = _dot(att, vh) + _dot(qh * jnp.exp(a), s)
        a_last = c * lg
        st_ref[0, h] = math.exp(a_last) * s + _dot_tn(kh * jnp.exp(a_last - a), vh)
        mu = jnp.mean(o, -1, keepdims=True)
        oc = o - mu
        var = jnp.mean(oc * oc, -1, keepdims=True)
        on = oc * lax.rsqrt(var + NORM_EPS) * gng_ref[:, sl] + gnb_ref[:, sl]
        o_ref[0, :, sl] = (on * _silu(z[:, sl])).astype(o_ref.dtype)


def _ret_call(p, state0, gn_g, gn_b, p0):
    b, t, _ = p.shape
    c = min(RET_CHUNK, t)
    assert t % c == 0
    tabs = _rope_tables(p0 + np.arange(t), HEAD_DIM, RET_THETA, N_HEADS)
    blk = lambda col: pl.BlockSpec((1, c, 256), lambda b_, i: (b_, i, col))
    tab = pl.BlockSpec((c, 256), lambda b_, i: (i, 0))
    st = pl.BlockSpec((1, N_HEADS, HEAD_DIM, HEAD_DIM), lambda b_, i: (b_, 0, 0, 0))
    vec = pl.BlockSpec((1, 256), lambda b_, i: (0, 0))
    return pl.pallas_call(
        _ret_kernel,
        out_shape=(jax.ShapeDtypeStruct((b, t, 256), BF16),
                   jax.ShapeDtypeStruct((b, N_HEADS, HEAD_DIM, HEAD_DIM), F32)),
        grid=(b, t // c),
        in_specs=[blk(P_BQ // 256), blk(P_BK // 256), blk(P_BV // 256), blk(P_BZ // 256), st,
                  tab, tab, tab, vec, vec],
        out_specs=(pl.BlockSpec((1, c, 256), lambda b_, i: (b_, i, 0)), st),
        compiler_params=_params(("parallel", "arbitrary")),
        name="ret",
    )(p, p, p, p, state0, *tabs, gn_g.reshape(1, 256), gn_b.reshape(1, 256))


def _fox_prompt_kernel(q_ref, k_ref, v_ref, f_ref, fb_ref, o_ref, lf_ref, cum_ref, cumt_ref, kb_ref, vb_ref):
    qi = pl.program_id(1)
    t = k_ref.shape[1]
    qb = q_ref.shape[1]

    @pl.when(qi == 0)
    def _prep():
        lf = _log_sigmoid(f_ref[0] + fb_ref[...])
        lf_ref[0] = lf
        tri = (_iota((qb, qb), 1) <= _iota((qb, qb), 0)).astype(F32)
        carry = jnp.zeros((1, 128), F32)
        for c in range(t // qb):
            blk = _dot(tri, lf[c * qb:(c + 1) * qb], HIGHEST) + carry
            cum_ref[c * qb:(c + 1) * qb, :] = blk
            carry = blk[qb - 1:qb, :]
        cumt_ref[...] = cum_ref[...].T
        for h in range(N_HEADS):
            kb_ref[h] = k_ref[0, :, h * HEAD_DIM:(h + 1) * HEAD_DIM].astype(BF16)
            vb_ref[h] = v_ref[0, :, h * HEAD_DIM:(h + 1) * HEAD_DIM].astype(BF16)

    s0 = pl.multiple_of(qi * qb, qb)
    q = (q_ref[0] * SCALE).astype(BF16)
    cq = cum_ref[pl.ds(s0, qb), :]
    qrow = s0 + _iota((qb, 1), 0)

    def _attend(ext):
        mask = _iota((1, ext), 1) <= qrow
        for h in range(N_HEADS):
            qh = q[:, h * HEAD_DIM:(h + 1) * HEAD_DIM]
            s = _dot_nt(qh, kb_ref[h, 0:ext, :]) + (cq[:, h:h + 1] - cumt_ref[h:h + 1, 0:ext])
            o = _softmax_pv(s, mask, vb_ref[h, 0:ext, :])
            o_ref[0, :, h * HEAD_DIM:(h + 1) * HEAD_DIM] = o.astype(o_ref.dtype)

    _causal_branches(qi, t // qb, t, _attend)


def _fox_prompt_call(p, f_bias):
    b, t, _ = p.shape
    qb = QUERY_BLOCK
    fb = jnp.zeros((1, 128), F32).at[0, :N_HEADS].set(f_bias)
    return pl.pallas_call(
        _fox_prompt_kernel,
        out_shape=(jax.ShapeDtypeStruct((b, t, 256), BF16),
                   jax.ShapeDtypeStruct((b, t, 128), F32)),
        grid=(b, t // qb),
        in_specs=[pl.BlockSpec((1, qb, 256), lambda b_, i: (b_, i, P_CQ // 256)),
                  pl.BlockSpec((1, t, 256), lambda b_, i: (b_, 0, P_CK // 256)),
                  pl.BlockSpec((1, t, 256), lambda b_, i: (b_, 0, P_CV // 256)),
                  pl.BlockSpec((1, t, 128), lambda b_, i: (b_, 0, P_CF // 128)),
                  pl.BlockSpec((1, 128), lambda b_, i: (0, 0))],
        out_specs=(pl.BlockSpec((1, qb, 256), lambda b_, i: (b_, i, 0)),
                   pl.BlockSpec((1, t, 128), lambda b_, i: (b_, 0, 0))),
        scratch_shapes=[pltpu.VMEM((t, 128), F32), pltpu.VMEM((128, t), F32),
                        pltpu.VMEM((N_HEADS, t, HEAD_DIM), BF16), pltpu.VMEM((N_HEADS, t, HEAD_DIM), BF16)],
        compiler_params=_params(("parallel", "arbitrary")),
        name="fox_prompt",
    )(p, p, p, p, fb)


def _rows_per_head(x, rows):
    return jnp.concatenate([jnp.broadcast_to(x[h:h + 1, :], (rows, x.shape[1])) for h in range(N_HEADS)], 0)


def _fox_sample_kernel(pt_ref, *refs, pps, nbr):
    kv_refs = refs[:nbr * pps]
    lf_refs = refs[nbr * pps:2 * nbr * pps]
    (q_ref, k_ref, v_ref, f_ref, fb_ref, o_ref, lf_ref,
     qbd_ref, m_ref, l_ref, acc_ref, carry_ref, cnew_ref) = refs[2 * nbr * pps:]
    i = pl.program_id(1)
    t = q_ref.shape[1]
    page_sz = kv_refs[0].shape[3]
    qidx = _iota((128, 1), 0) & 31

    @pl.when(i == 0)
    def _init():
        for r in range(nbr):
            q = q_ref[r]
            col_head = _iota((t, 256), 1) >> 6
            zpad = jnp.zeros((32 - t, 256), F32)
            qbd = jnp.concatenate(
                [piece for h in range(N_HEADS) for piece in (jnp.where(col_head == h, q, 0.0), zpad)], 0)
            qbd_ref[r] = qbd * SCALE
            lf = _log_sigmoid(f_ref[r] + fb_ref[...])
            lf_ref[r] = lf
            tri = (_iota((t, t), 1) <= _iota((t, t), 0)).astype(F32)
            cs = _dot(tri, lf, HIGHEST)
            zcol = jnp.zeros((32 - t, 1), F32)
            cnew = jnp.concatenate([piece for h in range(N_HEADS) for piece in (cs[:, h:h + 1], zcol)], 0)
            cnew_ref[r] = jnp.broadcast_to(cnew, (128, 128))
            eye = _iota((t, t), 0) == _iota((t, t), 1)
            cs_rows = jnp.concatenate(
                [jnp.broadcast_to(jnp.sum(jnp.where(eye, cs[:, h:h + 1], 0.0), 0, keepdims=True), (32, t))
                 for h in range(N_HEADS)], 0)
            s = _dot_nt(qbd * SCALE, k_ref[r]) + cnew - cs_rows
            mask = _iota((1, t), 1) <= qidx
            s = jnp.where(mask, s, NEG_BIG)
            m = jnp.max(s, -1, keepdims=True)
            e = jnp.where(mask, jnp.exp(s - m), 0.0)
            m_ref[r] = jnp.broadcast_to(m, (128, 128))
            l_ref[r] = jnp.broadcast_to(jnp.sum(e, -1, keepdims=True), (128, 128))
            acc_ref[r] = _dot(e, v_ref[r])
            carry_ref[r] = jnp.zeros((8, 128), F32)

    lane = _iota((N_HEADS, page_sz), 1)
    for r in range(nbr):
        carry = carry_ref[r, 0:N_HEADS, :]
        qbd_bf = qbd_ref[r].astype(BF16)
        cnew = cnew_ref[r, :, 0:1]
        tiles = []
        for j in range(pps):
            lf_t = lf_refs[r * pps + j][0, 0]
            incl = lf_t
            d = 1
            while d < page_sz:
                incl = incl + jnp.where(lane < page_sz - d, pltpu.roll(incl, page_sz - d, 1), 0.0)
                d *= 2
            bias = _rows_per_head(incl - lf_t + carry, 32) + cnew
            carry = carry + incl[:, 0:1]
            tiles.append(_dot(qbd_bf, kv_refs[r * pps + j][0, 0, 0:256, :].astype(BF16)) + bias)
        carry_ref[r, 0:N_HEADS, :] = carry
        s = jnp.concatenate(tiles, 1)
        m_old = m_ref[r, :, 0:1]
        m_new = jnp.maximum(m_old, jnp.max(s, -1, keepdims=True))
        alpha = jnp.exp(m_old - m_new)
        e = jnp.exp(s - m_new)
        m_ref[r] = jnp.broadcast_to(m_new, (128, 128))
        l_ref[r] = alpha * l_ref[r] + jnp.sum(e, -1, keepdims=True)
        acc = alpha * acc_ref[r]
        for j in range(pps):
            acc = acc + _dot_nt(e[:, j * page_sz:(j + 1) * page_sz].astype(BF16),
                                kv_refs[r * pps + j][0, 0, 256:512, :].astype(BF16))
        acc_ref[r] = acc

    @pl.when(i == pl.num_programs(1) - 1)
    def _finish():
        for r in range(nbr):
            o = acc_ref[r] / jnp.maximum(l_ref[r, :, 0:1], 1e-30)
            for h in range(N_HEADS):
                sl = slice(h * HEAD_DIM, (h + 1) * HEAD_DIM)
                o_ref[r, :, sl] = o[h * 32:h * 32 + t, sl].astype(o_ref.dtype)


def _fox_sample_call(p, kv_t, lf_t, page_table, layer, f_bias):
    b, t, _ = p.shape
    n_pages = page_table.shape[1]
    page_sz = kv_t.shape[3]
    assert t == 8 and page_sz == 128
    pps = _pages_per_step(n_pages)
    nbr = 2 if b % 2 == 0 else 1
    fb = jnp.zeros((1, 128), F32).at[0, :N_HEADS].set(f_bias)
    rev = lambda r, j: (lambda b_, i, pt: (pt[b_ * nbr + r, n_pages - 1 - (i * pps + j)], layer, 0, 0))
    slots = [(r, j) for r in range(nbr) for j in range(pps)]
    new = lambda width, col: pl.BlockSpec((nbr, t, width), lambda b_, i, pt: (b_, 0, col))
    grid_spec = pltpu.PrefetchScalarGridSpec(
        num_scalar_prefetch=1,
        grid=(b // nbr, n_pages // pps),
        in_specs=[pl.BlockSpec((1, 1, 512, page_sz), rev(r, j)) for r, j in slots]
        + [pl.BlockSpec((1, 1, N_HEADS, page_sz), rev(r, j)) for r, j in slots]
        + [new(256, P_CQ // 256), new(256, P_CK // 256), new(256, P_CV // 256), new(128, P_CF // 128),
           pl.BlockSpec((1, 128), lambda b_, i, pt: (0, 0))],
        out_specs=(pl.BlockSpec((nbr, t, 256), lambda b_, i, pt: (b_, 0, 0)),
                   pl.BlockSpec((nbr, t, 128), lambda b_, i, pt: (b_, 0, 0))),
        scratch_shapes=[pltpu.VMEM((nbr, 128, 256), F32), pltpu.VMEM((nbr, 128, 128), F32),
                        pltpu.VMEM((nbr, 128, 128), F32), pltpu.VMEM((nbr, 128, 256), F32),
                        pltpu.VMEM((nbr, 8, 128), F32), pltpu.VMEM((nbr, 128, 128), F32)])
    return pl.pallas_call(
        functools.partial(_fox_sample_kernel, pps=pps, nbr=nbr),
        out_shape=(jax.ShapeDtypeStruct((b, t, 256), BF16),
                   jax.ShapeDtypeStruct((b, t, 128), F32)),
        grid_spec=grid_spec,
        compiler_params=_params(("parallel", "arbitrary")),
        name="fox_sample",
    )(page_table, *([kv_t] * (nbr * pps)), *([lf_t] * (nbr * pps)), p, p, p, p, fb)


def _mxu(x):
    return x.astype(BF16) if x.shape[-2] % 16 == 0 else x


def _bdot(a, b):
    return lax.dot_general(_mxu(a), _mxu(b), (((2,), (1,)), ((0,), (0,))), preferred_element_type=F32)


def _bdot_nt(a, b):
    return lax.dot_general(_mxu(a), _mxu(b), (((2,), (2,)), ((0,), (0,))), preferred_element_type=F32)


def _same_block(ii, jj, size):
    shift = size.bit_length() - 1
    return (ii >> shift) == (jj >> shift)


def _gdn_kernel(qkv_ref, z_ref, ba_ref, cw_ref, cb_ref, s0_ref, pa_ref, ng_ref, o_ref, st_ref, xb_ref):
    ci = pl.program_id(1)
    nb, c = qkv_ref.shape[0], qkv_ref.shape[1]
    pad = 8

    @pl.when(ci == 0)
    def _init():
        st_ref[...] = s0_ref[...]
        xb_ref[:, pad - (CONV_K - 1):pad, :] = cb_ref[...]

    ii = _iota((1, c, c), 1)
    jj = _iota((1, c, c), 2)
    tri = (_iota((c, c), 1) <= _iota((c, c), 0)).astype(F32)
    eye = ii == jj
    qs, ks, vs, bs, acs = [], [], [], [], []
    for bi in range(nb):
        xb_ref[bi, pad:pad + c, :] = qkv_ref[bi]
        conv = xb_ref[bi, pad - 3:pad - 3 + c, :] * cw_ref[0:1, :]
        for j in range(1, CONV_K):
            conv = conv + xb_ref[bi, pad - 3 + j:pad - 3 + j + c, :] * cw_ref[j:j + 1, :]
        tail = xb_ref[bi, pad + c - (CONV_K - 1):pad + c, :]
        xb_ref[bi, pad - (CONV_K - 1):pad, :] = tail
        conv = _silu(conv)
        ba = ba_ref[bi]
        beta = _sigmoid(ba)
        g = -jnp.exp(pa_ref[0:1, :]) * _softplus(ba + pa_ref[1:2, :])
        acum = _dot(tri, g, HIGHEST)
        for h in range(N_HEADS):
            qs.append(conv[:, h * HEAD_DIM:(h + 1) * HEAD_DIM])
            ks.append(conv[:, 256 + h * HEAD_DIM:256 + (h + 1) * HEAD_DIM])
            vs.append(conv[:, 512 + h * HEAD_DIM:512 + (h + 1) * HEAD_DIM])
            bs.append(beta[:, h:h + 1])
            acs.append(acum[:, N_HEADS + h:N_HEADS + h + 1])
    q = jnp.stack(qs, 0)
    k = jnp.stack(ks, 0)
    v = jnp.stack(vs, 0)
    bcol = jnp.stack(bs, 0)
    acol = jnp.stack(acs, 0)
    q = q * lax.rsqrt(jnp.sum(q * q, -1, keepdims=True) + NORM_EPS) * SCALE
    k = k * lax.rsqrt(jnp.sum(k * k, -1, keepdims=True) + NORM_EPS)
    arow = jnp.sum(jnp.where(eye, acol, 0.0), 1, keepdims=True)
    decay = jnp.exp(jnp.minimum(acol - arow, 0.0))
    kb = k * bcol
    m = _bdot_nt(kb, k) * jnp.where(jj < ii, decay, 0.0)
    base = min(8, c)
    md = jnp.where(_same_block(ii, jj, base), m, 0.0)
    e = -md
    pw = _bdot(md, md)
    n = 2
    while n < base:
        e = e + pw + _bdot(e, pw)
        n *= 2
        if n < base:
            pw = _bdot(pw, pw)
    size = base
    while size < c:
        off = jnp.where(_same_block(ii, jj, 2 * size) & ~_same_block(ii, jj, size), m, 0.0)
        t1 = off + _bdot(e, off)
        e = e - t1 - _bdot(t1, e)
        size *= 2
    ea = jnp.exp(acol)
    rhs = jnp.concatenate([v * bcol, kb * ea], 2)
    sol = rhs + _bdot(e, rhs)
    s = st_ref[...].reshape(nb * N_HEADS, HEAD_DIM, HEAD_DIM)
    v_new = sol[:, :, 0:HEAD_DIM] - _bdot(sol[:, :, HEAD_DIM:128], s)
    att = _bdot_nt(q, k) * jnp.where(jj <= ii, decay, 0.0)
    o = _bdot(q * ea, s) + _bdot(att, v_new)
    o = o * lax.rsqrt(jnp.mean(o * o, -1, keepdims=True) + NORM_EPS) * ng_ref[...]
    a_last = acol[:, c - 1:c, :]
    kd = k * jnp.exp(a_last - acol)
    for bi in range(nb):
        z = z_ref[bi]
        for h in range(N_HEADS):
            gi = bi * N_HEADS + h
            sl = slice(h * HEAD_DIM, (h + 1) * HEAD_DIM)
            st_ref[bi, h] = jnp.exp(a_last[gi]) * s[gi] + _dot_tn(_mxu(kd[gi]), _mxu(v_new[gi]))
            o_ref[bi, :, sl] = (o[gi] * _silu(z[:, sl])).astype(o_ref.dtype)


def _gdn_call(p, conv_buf, state0, conv_w, a_log, dt_bias, norm_g):
    b, t, _ = p.shape
    c = min(GDN_CHUNK, t)
    nb = 2 if b % 2 == 0 else 1
    assert t % c == 0 and c >= CONV_K - 1
    pa = jnp.zeros((2, 128), F32).at[0, N_HEADS:2 * N_HEADS].set(a_log).at[1, N_HEADS:2 * N_HEADS].set(dt_bias)
    st = pl.BlockSpec((nb, N_HEADS, HEAD_DIM, HEAD_DIM), lambda b_, i: (b_, 0, 0, 0))
    return pl.pallas_call(
        _gdn_kernel,
        out_shape=(jax.ShapeDtypeStruct((b, t, 256), BF16),
                   jax.ShapeDtypeStruct((b, N_HEADS, HEAD_DIM, HEAD_DIM), F32)),
        grid=(b // nb, t // c),
        in_specs=[pl.BlockSpec((nb, c, 768), lambda b_, i: (b_, i, P_DQKV // 768)),
                  pl.BlockSpec((nb, c, 256), lambda b_, i: (b_, i, P_DZ // 256)),
                  pl.BlockSpec((nb, c, 128), lambda b_, i: (b_, i, P_DBA // 128)),
                  pl.BlockSpec((CONV_K, 768), lambda b_, i: (0, 0)),
                  pl.BlockSpec((nb, CONV_K - 1, 768), lambda b_, i: (b_, 0, 0)),
                  st,
                  pl.BlockSpec((2, 128), lambda b_, i: (0, 0)),
                  pl.BlockSpec((1, HEAD_DIM), lambda b_, i: (0, 0))],
        out_specs=(pl.BlockSpec((nb, c, 256), lambda b_, i: (b_, i, 0)), st),
        scratch_shapes=[pltpu.VMEM((nb, 8 + c, 768), F32)],
        compiler_params=_params(("parallel", "arbitrary")),
        name="gdn",
    )(p, p, p, conv_w, conv_buf, state0, pa, norm_g.reshape(1, HEAD_DIM))


def _outproj_kernel(oa_ref, ob_ref, oc_ref, od_ref, x_ref, g1_ref, sc2_ref, sh2_ref, w_ref,
                    l1g_ref, l1b_ref, rw_ref, rb_ref, x1_ref, u2_ref, wc_ref):
    y = _dot(oa_ref[0], w_ref[0:256, :])
    y = y + _dot(ob_ref[0], w_ref[256:512, :])
    y = y + _dot(oc_ref[0], w_ref[512:768, :])
    y = y + _dot(od_ref[0], w_ref[768:1024, :])
    x1 = _ln(DN_ALPHA * x_ref[0] + g1_ref[0] * y) * l1g_ref[...] + l1b_ref[...]
    x1_ref[0] = x1
    u2 = _ln(x1) * (1.0 + sc2_ref[0]) + sh2_ref[0]
    u2_ref[0] = u2.astype(BF16)
    scores = _sigmoid(_dot_nt(rw_ref[...], u2, HIGHEST))
    biased = scores + rb_ref[...]
    tm = biased.shape[1]
    per = N_EXPERTS // N_GROUPS
    idx = _iota((per, tm), 0).astype(F32)
    grp_rows = []
    for g in range(N_GROUPS):
        vals = biased[g * per:(g + 1) * per, :]
        m1 = jnp.max(vals, 0, keepdims=True)
        first = jnp.min(jnp.where(vals == m1, idx, float(per)), 0, keepdims=True)
        m2 = jnp.max(jnp.where(idx == first, -jnp.inf, vals), 0, keepdims=True)
        grp_rows.append(m1 + m2)
    gs = jnp.concatenate(grp_rows, 0)
    gi = _iota((N_GROUPS, tm), 0)
    rank = jnp.zeros((N_GROUPS, tm), F32)
    for g2 in range(N_GROUPS):
        row = gs[g2:g2 + 1, :]
        rank = rank + jnp.where((row > gs) | ((row == gs) & (g2 < gi)), 1.0, 0.0)
    keep = jnp.where(rank < float(TOPK_GROUPS), 1.0, 0.0)
    emask = jnp.concatenate([jnp.broadcast_to(keep[g:g + 1, :], (per, tm)) for g in range(N_GROUPS)], 0) > 0.5
    sel = _topk_mask(jnp.where(emask, biased, -jnp.inf), TOP_K, 0)
    w = sel * scores
    w = w / jnp.sum(w, 0, keepdims=True) * ROUTED_SCALE
    wc_ref[0] = jnp.concatenate([w, jnp.zeros((LANES - N_EXPERTS, tm), F32)], 0).T


def _outproj_call(oa, ob, oc, od, x, g1, sc2, sh2, w_out, ln_g, ln_b, rw_t, rb):
    g, r, _ = x.shape
    tm = min(256, r)
    o_spec = pl.BlockSpec((1, tm, 256), lambda g_, i: (g_, i, 0))
    x_spec = pl.BlockSpec((1, tm, D_MODEL), lambda g_, i: (g_, i, 0))
    vec = pl.BlockSpec((1, D_MODEL), lambda g_, i: (0, 0))
    return pl.pallas_call(
        _outproj_kernel,
        out_shape=(jax.ShapeDtypeStruct((g, r, D_MODEL), F32),
                   jax.ShapeDtypeStruct((g, r, D_MODEL), BF16),
                   jax.ShapeDtypeStruct((g, r, LANES), F32)),
        grid=(g, r // tm),
        in_specs=[o_spec, o_spec, o_spec, o_spec, x_spec,
                  _mod_spec(g1, tm), _mod_spec(sc2, tm), _mod_spec(sh2, tm),
                  pl.BlockSpec((D_MODEL, D_MODEL), lambda g_, i: (0, 0)),
                  vec, vec,
                  pl.BlockSpec((N_EXPERTS, D_MODEL), lambda g_, i: (0, 0)),
                  pl.BlockSpec((N_EXPERTS, 1), lambda g_, i: (0, 0))],
        out_specs=(x_spec, x_spec, pl.BlockSpec((1, tm, LANES), lambda g_, i: (g_, i, 0))),
        compiler_params=_params(("parallel", "parallel")),
        name="outproj",
    )(oa, ob, oc, od, x, g1, sc2, sh2, w_out, ln_g.reshape(1, -1), ln_b.reshape(1, -1), rw_t, rb.reshape(-1, 1))


EXPERTS_PER_STEP = 2


def _swiglu_act(hid):
    return _silu(hid[:, 0:EXPERT_FF]) * hid[:, EXPERT_FF:2 * EXPERT_FF]


def _moe_kernel(u_ref, wc_ref, x_ref, g2_ref, wgu_ref, wdn_ref, sgu_ref, sdn_ref, l2g_ref, l2b_ref,
                o_ref, acc_ref):
    step = pl.program_id(2)
    eps = wgu_ref.shape[0]
    u = u_ref[0]

    @pl.when(step == 0)
    def _shared():
        acc_ref[...] = _dot(_swiglu_act(_dot(u, sgu_ref[...])).astype(BF16), sdn_ref[...])

    wc = wc_ref[0]
    lane = _iota(wc.shape, 1)
    acts = []
    for k in range(eps):
        col = jnp.sum(jnp.where(lane == step * eps + k, wc, 0.0), -1, keepdims=True)
        acts.append((_swiglu_act(_dot(u, wgu_ref[k])) * col).astype(BF16))
    act = jnp.concatenate(acts, 1)
    acc_ref[...] += _dot(act, wdn_ref[...].reshape(eps * EXPERT_FF, D_MODEL))

    @pl.when(step == pl.num_programs(2) - 1)
    def _finish():
        o_ref[0] = _ln(DN_ALPHA * x_ref[0] + g2_ref[0] * acc_ref[...]) * l2g_ref[...] + l2b_ref[...]


def _moe_call(u2, wc, x1, g2, wgu, wdn, layer, sgu, sdn, ln_g, ln_b):
    g, r, _ = x1.shape
    tm = min(1024, r)
    eps = EXPERTS_PER_STEP
    tok = lambda width: pl.BlockSpec((1, tm, width), lambda g_, i, e: (g_, i, 0))
    if g2.shape[1] == 1:
        g2_spec = pl.BlockSpec((1, 1, D_MODEL), lambda g_, i, e: (g_, 0, 0))
    else:
        g2_spec = tok(D_MODEL)
    vec = pl.BlockSpec((1, D_MODEL), lambda g_, i, e: (0, 0))
    return pl.pallas_call(
        _moe_kernel,
        out_shape=jax.ShapeDtypeStruct((g, r, D_MODEL), F32),
        grid=(g, r // tm, N_EXPERTS // eps),
        in_specs=[tok(D_MODEL), tok(LANES), tok(D_MODEL), g2_spec,
                  pl.BlockSpec((None, eps, D_MODEL, 2 * EXPERT_FF), lambda g_, i, e: (layer, e, 0, 0)),
                  pl.BlockSpec((None, eps, EXPERT_FF, D_MODEL), lambda g_, i, e: (layer, e, 0, 0)),
                  pl.BlockSpec((D_MODEL, 2 * EXPERT_FF), lambda g_, i, e: (0, 0)),
                  pl.BlockSpec((EXPERT_FF, D_MODEL), lambda g_, i, e: (0, 0)),
                  vec, vec],
        out_specs=tok(D_MODEL),
        scratch_shapes=[pltpu.VMEM((tm, D_MODEL), F32)],
        compiler_params=_params(("parallel", "parallel", "arbitrary")),
        name="moe",
    )(u2, wc, x1, g2, wgu, wdn, sgu, sdn, ln_g.reshape(1, -1), ln_b.reshape(1, -1))


def _run_trunk(x, mod, p0, weights, stacked, past):
    b, t, _ = x.shape
    per_token = t < 128
    if per_token:
        grp = lambda a: a.reshape(1, b * t, a.shape[-1])
        mod_rows = lambda m: jnp.repeat(m, t, axis=0)[None]
    else:
        grp = lambda a: a
        mod_rows = lambda m: m[:, None, :]
    ungrp = lambda a: a.reshape(b, t, a.shape[-1])

    outs = []
    for l in range(DEPTH):
        w = {k: v[l] for k, v in weights.items()}
        sh1, sc1, g1, sh2, sc2, g2 = [mod_rows(m) for m in jnp.split(mod[l], 6, axis=-1)]
        p = ungrp(_proj_call(grp(x), sc1, sh1, w["w_in"]))
        if past is None:
            o_a, nsa_rows, win_rows = _nsa_prompt_call(p, w["nsa_pool"])
            win_new = win_rows[:, t - min(NSA_WINDOW, t):]
            o_c, logf = _fox_prompt_call(p, w["fox_f_bias"])
            ret_s0 = jnp.zeros((b, N_HEADS, HEAD_DIM, HEAD_DIM), F32)
            gdn_s0 = ret_s0
            conv_buf = jnp.zeros((b, CONV_K - 1, 3 * GROUP_WIDTH), F32)
        else:
            o_a, nsa_rows, win_new = _nsa_sample_call(p, past["nsa_t"], past["page_table"], l,
                                                      past["win_t"][l], w["nsa_pool"], p0)
            o_c, logf = _fox_sample_call(p, past["fox_kv_t"], past["fox_lf_t"], past["page_table"], l,
                                         w["fox_f_bias"])
            ret_s0, gdn_s0, conv_buf = past["state_ret"][l], past["state_gdn"][l], past["state_gdn_conv"][l]
        o_b, ret_s = _ret_call(p, ret_s0, w["ret_gn_g"], w["ret_gn_b"], p0)
        o_d, gdn_s = _gdn_call(p, conv_buf, gdn_s0, w["gdn_conv_w"], w["gdn_A_log"], w["gdn_dt_bias"],
                               w["gdn_norm_g"])
        x1, u2, wc = _outproj_call(grp(o_a), grp(o_b), grp(o_c), grp(o_d), grp(x), g1, sc2, sh2,
                                   w["w_out"], w["ln1_g"], w["ln1_b"], w["router_w_t"], w["router_b"])
        x = ungrp(_moe_call(u2, wc, x1, g2, stacked["exp_w_gu"], stacked["exp_w_down"], l,
                            w["sh_w_gu"], w["sh_w_down"], w["ln2_g"], w["ln2_b"]))
        qkv = p[:, :, P_DQKV:P_DQKV + 768]
        conv_new = jnp.concatenate([conv_buf, qkv], axis=1)[:, t:]
        outs.append((nsa_rows.reshape(b, t, 4, HEAD_DIM),
                     p[:, :, P_CK:P_CK + 512].reshape(b, t, 2, N_HEADS, HEAD_DIM),
                     logf[:, :, :N_HEADS],
                     win_new.reshape(b, win_new.shape[1], 2, HEAD_DIM),
                     ret_s, gdn_s, conv_new))
    nsa, fkv, flf, win, ret, gdn, conv = zip(*outs)
    return x, (jnp.stack(nsa, 1), jnp.stack(fkv, 1), jnp.stack(flf, 1), jnp.stack(win, 0),
               jnp.stack(ret, 0), jnp.stack(gdn, 0), jnp.stack(conv, 0))


def kernel(x_prompt, x_sample, cache_nsa, cache_fox_kv, cache_fox_logf, state_nsa_win, state_ret, state_gdn, state_gdn_conv, page_table, c_prompt, c_sample, w_mod, b_mod, w_in, w_out, nsa_pool, ret_gn_g, ret_gn_b, fox_f_bias, gdn_conv_w, gdn_A_log, gdn_dt_bias, gdn_norm_g, ln1_g, ln1_b, ln2_g, ln2_b, router_w, router_b, exp_w_gu, exp_w_down, sh_w_gu, sh_w_down):
    b = x_prompt.shape[0]
    db = x_sample.shape[0]
    n_pool, _, page_sz = cache_nsa.shape[:3]
    past_len = page_table.shape[1] * page_sz
    perm = _proj_perm()
    w_in_p = jnp.where(jnp.asarray(perm >= 0)[None, None, :],
                       jnp.take(w_in, jnp.asarray(np.maximum(perm, 0)), axis=2), 0.0).astype(BF16)
    weights = dict(
        w_in=w_in_p, w_out=w_out.astype(BF16), nsa_pool=nsa_pool, ret_gn_g=ret_gn_g, ret_gn_b=ret_gn_b,
        fox_f_bias=fox_f_bias, gdn_conv_w=gdn_conv_w, gdn_A_log=gdn_A_log, gdn_dt_bias=gdn_dt_bias,
        gdn_norm_g=gdn_norm_g, ln1_g=ln1_g, ln1_b=ln1_b, ln2_g=ln2_g, ln2_b=ln2_b,
        router_w_t=jnp.swapaxes(router_w, 1, 2), router_b=router_b,
        sh_w_gu=sh_w_gu.astype(BF16), sh_w_down=sh_w_down.astype(BF16))
    stacked = dict(exp_w_gu=exp_w_gu.astype(BF16), exp_w_down=exp_w_down.astype(BF16))
    n_c = b + db
    n_pad = -n_c % 8
    c_all = jnp.concatenate([c_prompt, c_sample, jnp.zeros((n_pad, D_MODEL), F32)], axis=0)
    mod = _mod_call(c_all, w_mod, b_mod)
    past = dict(
        nsa_t=jnp.transpose(cache_nsa, (0, 1, 3, 4, 2)).reshape(n_pool, DEPTH, 4 * HEAD_DIM, page_sz),
        fox_kv_t=jnp.transpose(cache_fox_kv, (0, 1, 3, 4, 5, 2)).reshape(n_pool, DEPTH, 2 * GROUP_WIDTH, page_sz),
        fox_lf_t=jnp.transpose(cache_fox_logf, (0, 1, 3, 2)),
        win_t=jnp.transpose(state_nsa_win, (0, 1, 3, 4, 2)).reshape(DEPTH, db, 2 * HEAD_DIM, state_nsa_win.shape[2]),
        state_ret=state_ret, state_gdn=state_gdn, state_gdn_conv=state_gdn_conv, page_table=page_table)
    y_p, (nsa_p, fkv_p, flf_p, win_p, ret_p, gdn_p, conv_p) = _run_trunk(x_prompt, mod[:, :b], 0, weights, stacked, None)
    y_s, (nsa_s, fkv_s, flf_s, win_s, ret_s, gdn_s, conv_s) = _run_trunk(x_sample, mod[:, b:n_c], past_len, weights, stacked, past)
    return (y_p, y_s, nsa_p, nsa_s, fkv_p, fkv_s, flf_p, flf_s, win_p, win_s,
            ret_p, ret_s, gdn_p, gdn_s, conv_p, conv_s)
```

```python
import functools
import math

import numpy as np
import jax
import jax.numpy as jnp
from jax import lax
from jax.experimental import pallas as pl
from jax.experimental.pallas import tpu as pltpu

F32 = jnp.float32
BF16 = jnp.bfloat16
HIGHEST = lax.Precision.HIGHEST

D_MODEL = 1024
DEPTH = 4
HEAD_DIM = 64
N_HEADS = 4
GROUP_WIDTH = N_HEADS * HEAD_DIM
NSA_BLOCK = 64
NSA_TOPN = 8
NSA_WINDOW = 512
NSA_FORCE = 1.0e4
ROPE_THETA = 500000.0
ROPE_DIMS = HEAD_DIM // 4
RET_THETA = 10000.0
RET_CHUNK = 128
GDN_CHUNK = 64
CONV_K = 4
N_EXPERTS = 64
TOP_K = 8
N_GROUPS = 8
TOPK_GROUPS = 4
EXPERT_FF = 256
ROUTED_SCALE = 2.5
DN_ALPHA = (2 * DEPTH) ** 0.25
LN_EPS = 1e-5
NORM_EPS = 1e-6
NEG_BIG = -1e30
SCALE = HEAD_DIM ** -0.5
QUERY_BLOCK = 256
LANES = 128
VMEM_LIMIT = 56 * 1024 * 1024

IN_SPLITS = (GROUP_WIDTH, 6 * HEAD_DIM, 3 * N_HEADS,
             GROUP_WIDTH, GROUP_WIDTH, GROUP_WIDTH, GROUP_WIDTH,
             GROUP_WIDTH, GROUP_WIDTH, GROUP_WIDTH, N_HEADS,
             3 * GROUP_WIDTH, N_HEADS, N_HEADS, GROUP_WIDTH)
IN_WIDTH = sum(IN_SPLITS)

P_AQ, P_BQ, P_BK, P_BV, P_BZ = 0, 256, 512, 768, 1024
P_CQ, P_CK, P_CV, P_DZ, P_DQKV = 1280, 1536, 1792, 2048, 2304
P_AKC, P_AKS, P_AKW, P_AG, P_CF, P_DBA = 3072, 3200, 3328, 3456, 3584, 3712
P_WIDTH = 3840


def _proj_perm():
    src = np.cumsum((0,) + IN_SPLITS)
    (q_a, kv_a, g_a, q_b, k_b, v_b, z_b, q_c, k_c, v_c, f_c, qkv_d, beta_d, a_d, z_d) = [int(s) for s in src[:-1]]
    perm = -np.ones((P_WIDTH,), np.int64)

    def put(dst, start, width):
        perm[dst:dst + width] = np.arange(start, start + width)

    put(P_AQ, q_a, 256)
    put(P_AKC, kv_a, 128)
    put(P_AKS, kv_a + 128, 128)
    put(P_AKW, kv_a + 256, 128)
    for h in range(N_HEADS):
        for j in range(3):
            perm[P_AG + j * N_HEADS + h] = g_a + h * 3 + j
    put(P_BQ, q_b, 256)
    put(P_BK, k_b, 256)
    put(P_BV, v_b, 256)
    put(P_BZ, z_b, 256)
    put(P_CQ, q_c, 256)
    put(P_CK, k_c, 256)
    put(P_CV, v_c, 256)
    put(P_CF, f_c, 4)
    put(P_DQKV, qkv_d, 768)
    put(P_DZ, z_d, 256)
    put(P_DBA, beta_d, 4)
    put(P_DBA + 4, a_d, 4)
    return perm


def _permute_columns(w, perm):
    pieces = []
    i, n = 0, len(perm)
    while i < n:
        j = i + 1
        if perm[i] < 0:
            while j < n and perm[j] < 0:
                j += 1
            pieces.append(jnp.zeros(w.shape[:-1] + (j - i,), w.dtype))
        else:
            while j < n and perm[j] == perm[j - 1] + 1:
                j += 1
            pieces.append(w[..., int(perm[i]):int(perm[i]) + (j - i)])
        i = j
    return jnp.concatenate(pieces, -1)


def _rope_tables(pos, n_rot, theta, n_heads, pad_identity=0):
    half = n_rot // 2
    inv = theta ** (-np.arange(half, dtype=np.float64) / half)
    ang = np.asarray(pos, np.float64)[:, None] * inv[None, :]
    t = ang.shape[0]
    c = np.ones((t, HEAD_DIM)); sa = np.zeros((t, HEAD_DIM)); sb = np.zeros((t, HEAD_DIM))
    c[:, :half] = np.cos(ang); c[:, half:n_rot] = np.cos(ang)
    sa[:, :half] = -np.sin(ang)
    sb[:, half:n_rot] = np.sin(ang)
    c = np.tile(c, (1, n_heads)); sa = np.tile(sa, (1, n_heads)); sb = np.tile(sb, (1, n_heads))
    if pad_identity:
        c = np.concatenate([c, np.ones((t, pad_identity))], 1)
        sa = np.concatenate([sa, np.zeros((t, pad_identity))], 1)
        sb = np.concatenate([sb, np.zeros((t, pad_identity))], 1)
    return tuple(jnp.asarray(a, F32) for a in (c, sa, sb))


def _dot(a, b, prec=None):
    return jnp.dot(a, b, preferred_element_type=F32, precision=prec)


def _dot_nt(a, b, prec=None):
    return lax.dot_general(a, b, (((1,), (1,)), ((), ())), preferred_element_type=F32, precision=prec)


def _dot_tn(a, b, prec=None):
    return lax.dot_general(a, b, (((0,), (0,)), ((), ())), preferred_element_type=F32, precision=prec)


def _iota(shape, axis):
    return lax.broadcasted_iota(jnp.int32, shape, axis)


def _ln(x):
    mu = jnp.mean(x, -1, keepdims=True)
    xc = x - mu
    var = jnp.mean(xc * xc, -1, keepdims=True)
    return xc * lax.rsqrt(var + LN_EPS)


def _sigmoid(x):
    return 1.0 / (1.0 + jnp.exp(-x))


def _silu(x):
    return x * _sigmoid(x)


def _softplus(x):
    return jnp.maximum(x, 0.0) + jnp.log1p(jnp.exp(-jnp.abs(x)))


def _log_sigmoid(x):
    return -_softplus(-x)


def _rope(x, c, sa, sb, half):
    w = x.shape[-1]
    return x * c + pltpu.roll(x, w - half, 1) * sa + pltpu.roll(x, half, 1) * sb


def _masked_softmax(s, mask, axis):
    s = jnp.where(mask, s, NEG_BIG)
    e = jnp.where(mask, jnp.exp(s - jnp.max(s, axis, keepdims=True)), 0.0)
    return e / jnp.maximum(jnp.sum(e, axis, keepdims=True), 1e-30)


def _softmax_pv(s, mask, v):
    s = jnp.where(mask, s, NEG_BIG)
    m = jnp.max(s, -1, keepdims=True)
    e = jnp.exp(s - m)
    den = jnp.sum(e, -1, keepdims=True)
    inv = jnp.where(m > 0.5 * NEG_BIG, 1.0 / jnp.maximum(den, 1e-30), 0.0)
    return _dot(e.astype(BF16), v) * inv


def _topk_mask(vals, k, axis):
    n = vals.shape[axis]
    idx = _iota(vals.shape, axis).astype(F32)
    sel = jnp.zeros(vals.shape, F32)
    work = vals
    for _ in range(k):
        m = jnp.max(work, axis, keepdims=True)
        first = jnp.min(jnp.where(work == m, idx, float(n)), axis, keepdims=True)
        pick = idx == first
        sel = jnp.where(pick, 1.0, sel)
        work = jnp.where(pick, -jnp.inf, work)
    return sel


def _pool_weights(pool_ref, n_rep):
    pl_t = pool_ref[...]
    e = jnp.exp(pl_t - jnp.max(pl_t, -1, keepdims=True))
    return e / (jnp.sum(e, -1, keepdims=True) / float(n_rep))


CAUSAL_GROUPS = 4


def _causal_branches(qi, n_qblocks, t, body):
    groups = CAUSAL_GROUPS if n_qblocks % CAUSAL_GROUPS == 0 else 1
    per = n_qblocks // groups
    for r in range(groups):
        pl.when((qi >= r * per) & (qi < (r + 1) * per))(functools.partial(body, (r + 1) * (t // groups)))


def _params(sem):
    return pltpu.CompilerParams(dimension_semantics=sem, vmem_limit_bytes=VMEM_LIMIT)


def _mod_kernel(c_ref, w_ref, b_ref, o_ref):
    o_ref[0] = _dot(c_ref[...].astype(BF16), w_ref[0].astype(BF16)) + b_ref[0]


def _mod_call(c_all, w_mod, b_mod):
    n = c_all.shape[0]
    tn = 1536
    return pl.pallas_call(
        _mod_kernel,
        out_shape=jax.ShapeDtypeStruct((DEPTH, n, 6 * D_MODEL), F32),
        grid=(DEPTH, 6 * D_MODEL // tn),
        in_specs=[pl.BlockSpec((n, D_MODEL), lambda l, j: (0, 0)),
                  pl.BlockSpec((1, D_MODEL, tn), lambda l, j: (l, 0, j)),
                  pl.BlockSpec((1, 1, tn), lambda l, j: (l, 0, j))],
        out_specs=pl.BlockSpec((1, n, tn), lambda l, j: (l, 0, j)),
        compiler_params=_params(("parallel", "parallel")),
        name="mod",
    )(c_all, w_mod, b_mod.reshape(DEPTH, 1, 6 * D_MODEL))


def _proj_kernel(x_ref, sc_ref, sh_ref, w_ref, o_ref):
    u = _ln(x_ref[0]) * (1.0 + sc_ref[0]) + sh_ref[0]
    o_ref[0] = _dot(u.astype(BF16), w_ref[...])


def _mod_spec(m, tm):
    if m.shape[1] == 1:
        return pl.BlockSpec((1, 1, D_MODEL), lambda g, i: (g, 0, 0))
    return pl.BlockSpec((1, tm, D_MODEL), lambda g, i: (g, i, 0))


def _proj_call(x, sc, sh, w):
    g, r, _ = x.shape
    tm = min(512, r)
    return pl.pallas_call(
        _proj_kernel,
        out_shape=jax.ShapeDtypeStruct((g, r, P_WIDTH), F32),
        grid=(g, r // tm),
        in_specs=[pl.BlockSpec((1, tm, D_MODEL), lambda g_, i: (g_, i, 0)),
                  _mod_spec(sc, tm), _mod_spec(sh, tm),
                  pl.BlockSpec((D_MODEL, P_WIDTH), lambda g_, i: (0, 0))],
        out_specs=pl.BlockSpec((1, tm, P_WIDTH), lambda g_, i: (g_, i, 0)),
        compiler_params=_params(("parallel", "parallel")),
        name="proj",
    )(x, sc, sh, w)


def _nsa_prompt_kernel(q_ref, g_ref, kc_ref, ks_ref, kw_ref, pool_ref,
                       qc_ref, qa_ref, qb_ref, kc_t_ref, ka_t_ref, kb_t_ref,
                       oa_ref, rows_ref, win_ref,
                       comp_ref, ksb_ref, vsb_ref, kwp_ref, vwp_ref, osel_ref):
    qi = pl.program_id(1)
    t = kc_ref.shape[1]
    nb = t // NSA_BLOCK
    qb = q_ref.shape[1]
    wnd = NSA_WINDOW

    @pl.when(qi == 0)
    def _prep():
        kcvc = kc_ref[0]
        ks_rot = _rope(ks_ref[0], kc_t_ref[...], ka_t_ref[...], kb_t_ref[...], ROPE_DIMS // 2)
        kw_rot = _rope(kw_ref[0], kc_t_ref[...], ka_t_ref[...], kb_t_ref[...], ROPE_DIMS // 2)
        rows_ref[0, :, 0:128] = kcvc
        rows_ref[0, :, 128:256] = ks_rot
        win_ref[0] = kw_rot
        ksb_ref[...] = ks_rot[:, 0:64].astype(BF16)
        vsb_ref[...] = ks_rot[:, 64:128].astype(BF16)
        kwp_ref[0:wnd, :] = jnp.zeros((wnd, HEAD_DIM), BF16)
        vwp_ref[0:wnd, :] = jnp.zeros((wnd, HEAD_DIM), BF16)
        kwp_ref[wnd:wnd + t, :] = kw_rot[:, 0:64].astype(BF16)
        vwp_ref[wnd:wnd + t, :] = kw_rot[:, 64:128].astype(BF16)
        wts = _pool_weights(pool_ref, nb)
        same = (_iota((nb, t), 1) >> 6) == _iota((nb, t), 0)
        pk = jnp.where(same, wts[0:1, :], 0.0)
        pv = jnp.where(same, wts[1:2, :], 0.0)
        ck = _dot(pk, kcvc, HIGHEST)
        cv = _dot(pv, kcvc, HIGHEST)
        comp_ref[...] = jnp.where(_iota((nb, 128), 1) < HEAD_DIM, ck, cv)

    s0 = pl.multiple_of(qi * qb, qb)
    q = q_ref[0] * SCALE
    qr = _rope(q, qc_ref[...], qa_ref[...], qb_ref[...], ROPE_DIMS // 2)
    gates = _sigmoid(g_ref[0])
    comp = comp_ref[...]
    compk = comp[:, 0:HEAD_DIM]
    compv = comp[:, HEAD_DIM:128]
    qp = s0 + _iota((qb, 1), 0)
    qp_row = s0 + _iota((1, qb), 1)
    blk = _iota((nb, 1), 0)
    cmask = blk < ((qp_row + 1) >> 6)
    imp = jnp.zeros((nb, qb), F32)
    o_cmp = []
    for h in range(N_HEADS):
        qh = q[:, h * HEAD_DIM:(h + 1) * HEAD_DIM]
        pc = _masked_softmax(_dot_nt(compk, qh, HIGHEST), cmask, 0)
        imp = imp + pc
        o_cmp.append(_dot_tn(pc, compv))
    cur = qp_row >> 6
    imp = jnp.where((blk == cur) | (blk == 0), NSA_FORCE, imp)
    imp = jnp.where(blk <= cur, imp, -1.0)
    sel = _topk_mask(imp, min(NSA_TOPN, nb), 0)
    qr_heads = [qr[:, h * HEAD_DIM:(h + 1) * HEAD_DIM].astype(BF16) for h in range(N_HEADS)]

    def _selected(ext):
        expand = ((_iota((nb, ext), 1) >> 6) == _iota((nb, ext), 0)).astype(F32)
        selk = _dot_tn(sel, expand)
        smask = (selk > 0.5) & (_iota((1, ext), 1) <= qp)
        ksb = ksb_ref[0:ext, :]
        vsb = vsb_ref[0:ext, :]
        for h in range(N_HEADS):
            osel_ref[:, h * HEAD_DIM:(h + 1) * HEAD_DIM] = _softmax_pv(_dot_nt(qr_heads[h], ksb), smask, vsb)

    _causal_branches(qi, t // qb, t, _selected)
    kw = kwp_ref[pl.ds(s0, wnd + qb), :]
    vw = vwp_ref[pl.ds(s0, wnd + qb), :]
    wpos = s0 - wnd + _iota((1, wnd + qb), 1)
    wmask = (wpos >= 0) & (wpos <= qp) & (wpos > qp - wnd)
    for h in range(N_HEADS):
        o_win = _softmax_pv(_dot_nt(qr_heads[h], kw), wmask, vw)
        o_sel = osel_ref[:, h * HEAD_DIM:(h + 1) * HEAD_DIM]
        out = (gates[:, h:h + 1] * o_cmp[h] + gates[:, N_HEADS + h:N_HEADS + h + 1] * o_sel
               + gates[:, 2 * N_HEADS + h:2 * N_HEADS + h + 1] * o_win)
        oa_ref[0, :, h * HEAD_DIM:(h + 1) * HEAD_DIM] = out.astype(oa_ref.dtype)


def _nsa_prompt_call(p, pool_l):
    b, t, _ = p.shape
    qb = QUERY_BLOCK
    nb = t // NSA_BLOCK
    pos = np.arange(t)
    q_tabs = _rope_tables(pos, ROPE_DIMS, ROPE_THETA, N_HEADS)
    k_tabs = _rope_tables(pos, ROPE_DIMS, ROPE_THETA, 1, pad_identity=HEAD_DIM)
    pool_t = jnp.tile(pool_l, (1, nb))
    full = lambda col: pl.BlockSpec((1, t, 128), lambda b_, i: (b_, 0, col))
    qtab = pl.BlockSpec((qb, 256), lambda b_, i: (i, 0))
    ktab = pl.BlockSpec((t, 128), lambda b_, i: (0, 0))
    return pl.pallas_call(
        _nsa_prompt_kernel,
        out_shape=(jax.ShapeDtypeStruct((b, t, 256), BF16),
                   jax.ShapeDtypeStruct((b, t, 256), F32),
                   jax.ShapeDtypeStruct((b, t, 128), F32)),
        grid=(b, t // qb),
        in_specs=[pl.BlockSpec((1, qb, 256), lambda b_, i: (b_, i, P_AQ // 256)),
                  pl.BlockSpec((1, qb, 128), lambda b_, i: (b_, i, P_AG // 128)),
                  full(P_AKC // 128), full(P_AKS // 128), full(P_AKW // 128),
                  pl.BlockSpec((2, t), lambda b_, i: (0, 0)),
                  qtab, qtab, qtab, ktab, ktab, ktab],
        out_specs=(pl.BlockSpec((1, qb, 256), lambda b_, i: (b_, i, 0)),
                   pl.BlockSpec((1, t, 256), lambda b_, i: (b_, 0, 0)),
                   pl.BlockSpec((1, t, 128), lambda b_, i: (b_, 0, 0))),
        scratch_shapes=[pltpu.VMEM((nb, 128), F32),
                        pltpu.VMEM((t, HEAD_DIM), BF16), pltpu.VMEM((t, HEAD_DIM), BF16),
                        pltpu.VMEM((NSA_WINDOW + t, HEAD_DIM), BF16), pltpu.VMEM((NSA_WINDOW + t, HEAD_DIM), BF16),
                        pltpu.VMEM((qb, GROUP_WIDTH), F32)],
        compiler_params=_params(("parallel", "arbitrary")),
        name="nsa_prompt",
    )(p, p, p, p, p, pool_t, *q_tabs, *k_tabs)


def _softmax2(s1, mask1, s2, mask2):
    s1 = jnp.where(mask1, s1, NEG_BIG)
    s2 = jnp.where(mask2, s2, NEG_BIG)
    m = jnp.maximum(jnp.max(s1, -1, keepdims=True), jnp.max(s2, -1, keepdims=True))
    e1 = jnp.where(mask1, jnp.exp(s1 - m), 0.0)
    e2 = jnp.where(mask2, jnp.exp(s2 - m), 0.0)
    den = jnp.maximum(jnp.sum(e1, -1, keepdims=True) + jnp.sum(e2, -1, keepdims=True), 1e-30)
    return e1, e2, den


def _nsa_sample_kernel(pt_ref, *refs, past, pps):
    page_refs = refs[:pps]
    (q_ref, g_ref, kc_ref, ks_ref, kw_ref, hist_ref, pool_ref,
     qc_ref, qa_ref, qb_ref, kc_t_ref, ka_t_ref, kb_t_ref,
     oa_ref, rows_ref, win_ref, comp_ref, kst_ref, vst_ref) = refs[pps:]
    i = pl.program_id(1)
    t = q_ref.shape[1]
    page_sz = page_refs[0].shape[3]
    n_pages = past // page_sz
    cr = comp_ref.shape[0]
    wts = _pool_weights(pool_ref, page_sz // NSA_BLOCK)
    r16 = _iota((16, page_sz), 0)
    half16 = _iota((16, page_sz), 1) >> 6
    pkv = jnp.where((r16 == half16), wts[0:1, :], 0.0) + jnp.where((r16 - 8 == half16), wts[1:2, :], 0.0)
    lane_lo = _iota((8, 128), 1) < HEAD_DIM

    @pl.when(i == 0)
    def _init():
        comp_ref[cr - 16:cr, :] = jnp.zeros((16, 128), F32)

    pkv_hi = pkv.astype(BF16)
    pkv_split = jnp.concatenate([pkv_hi, (pkv - pkv_hi.astype(F32)).astype(BF16)], 0)
    for j in range(pps):
        page_t = page_refs[j][0, 0]
        x = page_t[0:128, :]
        x_hi = x.astype(BF16)
        x_lo = (x - x_hi.astype(F32)).astype(BF16)
        r_hi = _dot_nt(pkv_split, x_hi)
        res = r_hi[0:16] + r_hi[16:32] + _dot_nt(pkv_hi, x_lo)
        pg = i * pps + j
        comp_ref[pl.ds(pl.multiple_of(pg * 8, 8), 8), :] = jnp.where(lane_lo, res[0:8], res[8:16])
        col0 = pl.multiple_of(pg * page_sz, page_sz)
        kst_ref[:, pl.ds(col0, page_sz)] = page_t[128:192, :].astype(BF16)
        vst_ref[:, pl.ds(col0, page_sz)] = page_t[192:256, :].astype(BF16)

    @pl.when(i == pl.num_programs(1) - 1)
    def _finish():
        kcvc = kc_ref[0]
        ks_rot = _rope(ks_ref[0], kc_t_ref[...], ka_t_ref[...], kb_t_ref[...], ROPE_DIMS // 2)
        kw_rot = _rope(kw_ref[0], kc_t_ref[...], ka_t_ref[...], kb_t_ref[...], ROPE_DIMS // 2)
        rows_ref[0, :, 0:128] = kcvc
        rows_ref[0, :, 128:256] = ks_rot
        res_n = _dot(pkv[:, 0:t], kcvc, HIGHEST)
        comp_ref[cr - 16:cr - 8, :] = jnp.where(lane_lo, res_n[0:8], res_n[8:16])

        q = q_ref[0]
        qr = _rope(q, qc_ref[...], qa_ref[...], qb_ref[...], ROPE_DIMS // 2)
        zpad = jnp.zeros((32 - t, HEAD_DIM), F32)
        stack = lambda x: jnp.concatenate(
            [piece for h in range(N_HEADS) for piece in (x[:, h * HEAD_DIM:(h + 1) * HEAD_DIM], zpad)], 0)
        q_all = stack(q)
        qr_all = stack(qr)
        qidx = _iota((128, 1), 0) & 31
        qp = past + qidx

        comp = comp_ref[...]
        compk = comp[:, 0:HEAD_DIM]
        compv = comp[:, HEAD_DIM:128]
        ci = _iota((1, cr), 1)
        blk = 2 * (ci >> 3) + (ci & 7)
        valid = ((ci & 7) < 2) & (blk * NSA_BLOCK < past + t)
        cmask = valid & (blk < ((qp + 1) >> 6))
        pc = _masked_softmax(_dot_nt(q_all, compk, HIGHEST) * SCALE, cmask, -1)
        o_cmp = _dot(pc, compv)
        imp = pc[0:32] + pc[32:64] + pc[64:96] + pc[96:128]
        cur = qp[0:32] >> 6
        imp = jnp.where((blk == cur) | (blk == 0), NSA_FORCE, imp)
        imp = jnp.where(blk <= cur, imp, -1.0)
        imp = jnp.where(valid, imp, -2.0)
        sel32 = _topk_mask(imp, NSA_TOPN, 1)
        sel = jnp.concatenate([sel32] * N_HEADS, 0)

        lane_half = _iota((128, page_sz), 1) < NSA_BLOCK
        selk = jnp.concatenate(
            [jnp.where(lane_half, sel[:, 8 * pg:8 * pg + 1], sel[:, 8 * pg + 1:8 * pg + 2]) for pg in range(n_pages)], 1)
        smask = (selk > 0.5) & (_iota((1, past), 1) <= qp)
        new_idx = _iota((1, t), 1)
        nmask = (sel[:, cr - 16:cr - 15] > 0.5) & (new_idx <= qidx)
        s_past = _dot(qr_all.astype(BF16), kst_ref[...]) * SCALE
        s_new = _dot_nt(qr_all, ks_rot[:, 0:HEAD_DIM]) * SCALE
        e1, e2, den = _softmax2(s_past, smask, s_new, nmask)
        o_sel = (_dot_nt(e1.astype(BF16), vst_ref[...]) + _dot(e2, ks_rot[:, HEAD_DIM:128])) / den

        hist_t = hist_ref[0]
        wb = hist_t.shape[1]
        wpos = past - wb + _iota((1, wb), 1)
        hmask = (wpos >= 0) & (wpos <= qp) & (wpos > qp - NSA_WINDOW)
        wmask = (new_idx <= qidx) & (past + new_idx > qp - NSA_WINDOW)
        s_hist = _dot(qr_all, hist_t[0:HEAD_DIM, :]) * SCALE
        s_wnew = _dot_nt(qr_all, kw_rot[:, 0:HEAD_DIM]) * SCALE
        e1, e2, den = _softmax2(s_hist, hmask, s_wnew, wmask)
        o_win = (_dot_nt(e1, hist_t[HEAD_DIM:128, :]) + _dot(e2, kw_rot[:, HEAD_DIM:128])) / den
        hist_tok = hist_t.T
        win_ref[0, 0:wb - t, :] = hist_tok[t:wb, :]
        win_ref[0, wb - t:wb, :] = kw_rot

        gates = _sigmoid(g_ref[0])
        for h in range(N_HEADS):
            r = slice(h * 32, h * 32 + t)
            out = (gates[:, h:h + 1] * o_cmp[r] + gates[:, N_HEADS + h:N_HEADS + h + 1] * o_sel[r]
                   + gates[:, 2 * N_HEADS + h:2 * N_HEADS + h + 1] * o_win[r])
            oa_ref[0, :, h * HEAD_DIM:(h + 1) * HEAD_DIM] = out.astype(oa_ref.dtype)


def _pages_per_step(n_pages):
    return max(d for d in (8, 4, 2, 1) if n_pages % d == 0)


def _nsa_sample_call(p, cache_t, page_table, layer, hist_t, pool_l, past):
    b, t, _ = p.shape
    n_pages = page_table.shape[1]
    page_sz = cache_t.shape[3]
    wb = hist_t.shape[2]
    assert page_sz == 128 and t == 8 and past == n_pages * page_sz and wb == NSA_WINDOW and past >= wb
    pps = _pages_per_step(n_pages)
    pos = past + np.arange(t)
    q_tabs = _rope_tables(pos, ROPE_DIMS, ROPE_THETA, N_HEADS)
    k_tabs = _rope_tables(pos, ROPE_DIMS, ROPE_THETA, 1, pad_identity=HEAD_DIM)
    pool_t = jnp.tile(pool_l, (1, page_sz // NSA_BLOCK))
    new = lambda col: pl.BlockSpec((1, t, 128), lambda b_, i, pt: (b_, 0, col))
    const = lambda shape: pl.BlockSpec(shape, lambda b_, i, pt: (0,) * len(shape))
    page_spec = lambda j: pl.BlockSpec((1, 1, 256, page_sz), lambda b_, i, pt: (pt[b_, i * pps + j], layer, 0, 0))
    grid_spec = pltpu.PrefetchScalarGridSpec(
        num_scalar_prefetch=1,
        grid=(b, n_pages // pps),
        in_specs=[page_spec(j) for j in range(pps)] + [
            pl.BlockSpec((1, t, 256), lambda b_, i, pt: (b_, 0, P_AQ // 256)),
            new(P_AG // 128), new(P_AKC // 128), new(P_AKS // 128), new(P_AKW // 128),
            pl.BlockSpec((1, 128, wb), lambda b_, i, pt: (b_, 0, 0)),
            const((2, page_sz)),
            const((t, 256)), const((t, 256)), const((t, 256)),
            const((t, 128)), const((t, 128)), const((t, 128))],
        out_specs=(pl.BlockSpec((1, t, 256), lambda b_, i, pt: (b_, 0, 0)),
                   pl.BlockSpec((1, t, 256), lambda b_, i, pt: (b_, 0, 0)),
                   pl.BlockSpec((1, wb, 128), lambda b_, i, pt: (b_, 0, 0))),
        scratch_shapes=[pltpu.VMEM((8 * n_pages + 16, 128), F32),
                        pltpu.VMEM((HEAD_DIM, past), BF16),
                        pltpu.VMEM((HEAD_DIM, past), BF16)])
    return pl.pallas_call(
        functools.partial(_nsa_sample_kernel, past=past, pps=pps),
        out_shape=(jax.ShapeDtypeStruct((b, t, 256), BF16),
                   jax.ShapeDtypeStruct((b, t, 256), F32),
                   jax.ShapeDtypeStruct((b, wb, 128), F32)),
        grid_spec=grid_spec,
        compiler_params=_params(("parallel", "arbitrary")),
        name="nsa_sample",
    )(page_table, *([cache_t] * pps), p, p, p, p, p, hist_t, pool_t, *q_tabs, *k_tabs)


def _ret_kernel(q_ref, k_ref, v_ref, z_ref, s0_ref, c_ref, sa_ref, sb_ref, gng_ref, gnb_ref,
                o_ref, st_ref):
    ci = pl.program_id(1)
    c = q_ref.shape[1]

    @pl.when(ci == 0)
    def _init():
        st_ref[...] = s0_ref[...]

    tabs = (c_ref[...], sa_ref[...], sb_ref[...])
    q = _rope(q_ref[0], *tabs, HEAD_DIM // 2)
    k = _rope(k_ref[0], *tabs, HEAD_DIM // 2) * SCALE
    v = v_ref[0]
    z = z_ref[0]
    ii = _iota((c, c), 0)
    jj = _iota((c, c), 1)
    rowi = _iota((c, 1), 0).astype(F32)
    for h in range(N_HEADS):
        sl = slice(h * HEAD_DIM, (h + 1) * HEAD_DIM)
        lg = math.log1p(-2.0 ** (-5.0 - h))
        qh, kh, vh = q[:, sl], k[:, sl], v[:, sl]
        a = (rowi + 1.0) * lg
        dec = jnp.where(jj <= ii, jnp.exp(jnp.minimum((ii - jj).astype(F32) * lg, 0.0)), 0.0)
        s = st_ref[0, h]
        att = _dot_nt(qh, kh) * dec
        o = _dot(att, vh) + _dot(qh * jnp.exp(a), s)
        a_last = c * lg
        st_ref[0, h] = math.exp(a_last) * s + _dot_tn(kh * jnp.exp(a_last - a), vh)
        mu = jnp.mean(o, -1, keepdims=True)
        oc = o - mu
        var = jnp.mean(oc * oc, -1, keepdims=True)
        on = oc * lax.rsqrt(var + NORM_EPS) * gng_ref[:, sl] + gnb_ref[:, sl]
        o_ref[0, :, sl] = (on * _silu(z[:, sl])).astype(o_ref.dtype)


def _ret_call(p, state0, gn_g, gn_b, p0):
    b, t, _ = p.shape
    c = min(RET_CHUNK, t)
    assert t % c == 0
    tabs = _rope_tables(p0 + np.arange(t), HEAD_DIM, RET_THETA, N_HEADS)
    blk = lambda col: pl.BlockSpec((1, c, 256), lambda b_, i: (b_, i, col))
    tab = pl.BlockSpec((c, 256), lambda b_, i: (i, 0))
    st = pl.BlockSpec((1, N_HEADS, HEAD_DIM, HEAD_DIM), lambda b_, i: (b_, 0, 0, 0))
    vec = pl.BlockSpec((1, 256), lambda b_, i: (0, 0))
    return pl.pallas_call(
        _ret_kernel,
        out_shape=(jax.ShapeDtypeStruct((b, t, 256), BF16),
                   jax.ShapeDtypeStruct((b, N_HEADS, HEAD_DIM, HEAD_DIM), F32)),
        grid=(b, t // c),
        in_specs=[blk(P_BQ // 256), blk(P_BK // 256), blk(P_BV // 256), blk(P_BZ // 256), st,
                  tab, tab, tab, vec, vec],
        out_specs=(pl.BlockSpec((1, c, 256), lambda b_, i: (b_, i, 0)), st),
        compiler_params=_params(("parallel", "arbitrary")),
        name="ret",
    )(p, p, p, p, state0, *tabs, gn_g.reshape(1, 256), gn_b.reshape(1, 256))


def _fox_prompt_kernel(q_ref, k_ref, v_ref, f_ref, fb_ref, o_ref, lf_ref, cum_ref, cumt_ref, kb_ref, vb_ref):
    qi = pl.program_id(1)
    t = k_ref.shape[1]
    qb = q_ref.shape[1]

    @pl.when(qi == 0)
    def _prep():
        lf = _log_sigmoid(f_ref[0] + fb_ref[...])
        lf_ref[0] = lf
        tri = (_iota((qb, qb), 1) <= _iota((qb, qb), 0)).astype(F32)
        carry = jnp.zeros((1, 128), F32)
        for c in range(t // qb):
            blk = _dot(tri, lf[c * qb:(c + 1) * qb], HIGHEST) + carry
            cum_ref[c * qb:(c + 1) * qb, :] = blk
            carry = blk[qb - 1:qb, :]
        cumt_ref[...] = cum_ref[...].T
        for h in range(N_HEADS):
            kb_ref[h] = k_ref[0, :, h * HEAD_DIM:(h + 1) * HEAD_DIM].astype(BF16)
            vb_ref[h] = v_ref[0, :, h * HEAD_DIM:(h + 1) * HEAD_DIM].astype(BF16)

    s0 = pl.multiple_of(qi * qb, qb)
    q = (q_ref[0] * SCALE).astype(BF16)
    cq = cum_ref[pl.ds(s0, qb), :]
    qrow = s0 + _iota((qb, 1), 0)

    def _attend(ext):
        mask = _iota((1, ext), 1) <= qrow
        for h in range(N_HEADS):
            qh = q[:, h * HEAD_DIM:(h + 1) * HEAD_DIM]
            s = _dot_nt(qh, kb_ref[h, 0:ext, :]) + (cq[:, h:h + 1] - cumt_ref[h:h + 1, 0:ext])
            o = _softmax_pv(s, mask, vb_ref[h, 0:ext, :])
            o_ref[0, :, h * HEAD_DIM:(h + 1) * HEAD_DIM] = o.astype(o_ref.dtype)

    _causal_branches(qi, t // qb, t, _attend)


def _fox_prompt_call(p, f_bias):
    b, t, _ = p.shape
    qb = QUERY_BLOCK
    fb = jnp.zeros((1, 128), F32).at[0, :N_HEADS].set(f_bias)
    return pl.pallas_call(
        _fox_prompt_kernel,
        out_shape=(jax.ShapeDtypeStruct((b, t, 256), BF16),
                   jax.ShapeDtypeStruct((b, t, 128), F32)),
        grid=(b, t // qb),
        in_specs=[pl.BlockSpec((1, qb, 256), lambda b_, i: (b_, i, P_CQ // 256)),
                  pl.BlockSpec((1, t, 256), lambda b_, i: (b_, 0, P_CK // 256)),
                  pl.BlockSpec((1, t, 256), lambda b_, i: (b_, 0, P_CV // 256)),
                  pl.BlockSpec((1, t, 128), lambda b_, i: (b_, 0, P_CF // 128)),
                  pl.BlockSpec((1, 128), lambda b_, i: (0, 0))],
        out_specs=(pl.BlockSpec((1, qb, 256), lambda b_, i: (b_, i, 0)),
                   pl.BlockSpec((1, t, 128), lambda b_, i: (b_, 0, 0))),
        scratch_shapes=[pltpu.VMEM((t, 128), F32), pltpu.VMEM((128, t), F32),
                        pltpu.VMEM((N_HEADS, t, HEAD_DIM), BF16), pltpu.VMEM((N_HEADS, t, HEAD_DIM), BF16)],
        compiler_params=_params(("parallel", "arbitrary")),
        name="fox_prompt",
    )(p, p, p, p, fb)


def _rows_per_head(x, rows):
    return jnp.concatenate([jnp.broadcast_to(x[h:h + 1, :], (rows, x.shape[1])) for h in range(N_HEADS)], 0)


def _fox_sample_kernel(pt_ref, *refs, pps, nbr):
    kv_refs = refs[:nbr * pps]
    lf_refs = refs[nbr * pps:2 * nbr * pps]
    (q_ref, k_ref, v_ref, f_ref, fb_ref, o_ref, lf_ref,
     qbd_ref, m_ref, l_ref, acc_ref, carry_ref, cnew_ref) = refs[2 * nbr * pps:]
    i = pl.program_id(1)
    t = q_ref.shape[1]
    page_sz = kv_refs[0].shape[3]
    qidx = _iota((128, 1), 0) & 31

    @pl.when(i == 0)
    def _init():
        for r in range(nbr):
            q = q_ref[r]
            col_head = _iota((t, 256), 1) >> 6
            zpad = jnp.zeros((32 - t, 256), F32)
            qbd = jnp.concatenate(
                [piece for h in range(N_HEADS) for piece in (jnp.where(col_head == h, q, 0.0), zpad)], 0)
            qbd_ref[r] = qbd * SCALE
            lf = _log_sigmoid(f_ref[r] + fb_ref[...])
            lf_ref[r] = lf
            tri = (_iota((t, t), 1) <= _iota((t, t), 0)).astype(F32)
            cs = _dot(tri, lf, HIGHEST)
            zcol = jnp.zeros((32 - t, 1), F32)
            cnew = jnp.concatenate([piece for h in range(N_HEADS) for piece in (cs[:, h:h + 1], zcol)], 0)
            cnew_ref[r] = jnp.broadcast_to(cnew, (128, 128))
            eye = _iota((t, t), 0) == _iota((t, t), 1)
            cs_rows = jnp.concatenate(
                [jnp.broadcast_to(jnp.sum(jnp.where(eye, cs[:, h:h + 1], 0.0), 0, keepdims=True), (32, t))
                 for h in range(N_HEADS)], 0)
            s = _dot_nt(qbd * SCALE, k_ref[r]) + cnew - cs_rows
            mask = _iota((1, t), 1) <= qidx
            s = jnp.where(mask, s, NEG_BIG)
            m = jnp.max(s, -1, keepdims=True)
            e = jnp.where(mask, jnp.exp(s - m), 0.0)
            m_ref[r] = jnp.broadcast_to(m, (128, 128))
            l_ref[r] = jnp.broadcast_to(jnp.sum(e, -1, keepdims=True), (128, 128))
            acc_ref[r] = _dot(e, v_ref[r])
            carry_ref[r] = jnp.zeros((8, 128), F32)

    lane = _iota((N_HEADS, page_sz), 1)
    for r in range(nbr):
        carry = carry_ref[r, 0:N_HEADS, :]
        qbd_bf = qbd_ref[r].astype(BF16)
        cnew = cnew_ref[r, :, 0:1]
        tiles = []
        for j in range(pps):
            lf_t = lf_refs[r * pps + j][0, 0]
            incl = lf_t
            d = 1
            while d < page_sz:
                incl = incl + jnp.where(lane < page_sz - d, pltpu.roll(incl, page_sz - d, 1), 0.0)
                d *= 2
            bias = _rows_per_head(incl - lf_t + carry, 32) + cnew
            carry = carry + incl[:, 0:1]
            tiles.append(_dot(qbd_bf, kv_refs[r * pps + j][0, 0, 0:256, :].astype(BF16)) + bias)
        carry_ref[r, 0:N_HEADS, :] = carry
        s = jnp.concatenate(tiles, 1)
        m_old = m_ref[r, :, 0:1]
        m_new = jnp.maximum(m_old, jnp.max(s, -1, keepdims=True))
        alpha = jnp.exp(m_old - m_new)
        e = jnp.exp(s - m_new)
        m_ref[r] = jnp.broadcast_to(m_new, (128, 128))
        l_ref[r] = alpha * l_ref[r] + jnp.sum(e, -1, keepdims=True)
        acc = alpha * acc_ref[r]
        for j in range(pps):
            acc = acc + _dot_nt(e[:, j * page_sz:(j + 1) * page_sz].astype(BF16),
                                kv_refs[r * pps + j][0, 0, 256:512, :].astype(BF16))
        acc_ref[r] = acc

    @pl.when(i == pl.num_programs(1) - 1)
    def _finish():
        for r in range(nbr):
            o = acc_ref[r] / jnp.maximum(l_ref[r, :, 0:1], 1e-30)
            for h in range(N_HEADS):
                sl = slice(h * HEAD_DIM, (h + 1) * HEAD_DIM)
                o_ref[r, :, sl] = o[h * 32:h * 32 + t, sl].astype(o_ref.dtype)


def _fox_sample_call(p, kv_t, lf_t, page_table, layer, f_bias):
    b, t, _ = p.shape
    n_pages = page_table.shape[1]
    page_sz = kv_t.shape[3]
    assert t == 8 and page_sz == 128
    pps = _pages_per_step(n_pages)
    nbr = 2 if b % 2 == 0 else 1
    fb = jnp.zeros((1, 128), F32).at[0, :N_HEADS].set(f_bias)
    rev = lambda r, j: (lambda b_, i, pt: (pt[b_ * nbr + r, n_pages - 1 - (i * pps + j)], layer, 0, 0))
    slots = [(r, j) for r in range(nbr) for j in range(pps)]
    new = lambda width, col: pl.BlockSpec((nbr, t, width), lambda b_, i, pt: (b_, 0, col))
    grid_spec = pltpu.PrefetchScalarGridSpec(
        num_scalar_prefetch=1,
        grid=(b // nbr, n_pages // pps),
        in_specs=[pl.BlockSpec((1, 1, 512, page_sz), rev(r, j)) for r, j in slots]
        + [pl.BlockSpec((1, 1, N_HEADS, page_sz), rev(r, j)) for r, j in slots]
        + [new(256, P_CQ // 256), new(256, P_CK // 256), new(256, P_CV // 256), new(128, P_CF // 128),
           pl.BlockSpec((1, 128), lambda b_, i, pt: (0, 0))],
        out_specs=(pl.BlockSpec((nbr, t, 256), lambda b_, i, pt: (b_, 0, 0)),
                   pl.BlockSpec((nbr, t, 128), lambda b_, i, pt: (b_, 0, 0))),
        scratch_shapes=[pltpu.VMEM((nbr, 128, 256), F32), pltpu.VMEM((nbr, 128, 128), F32),
                        pltpu.VMEM((nbr, 128, 128), F32), pltpu.VMEM((nbr, 128, 256), F32),
                        pltpu.VMEM((nbr, 8, 128), F32), pltpu.VMEM((nbr, 128, 128), F32)])
    return pl.pallas_call(
        functools.partial(_fox_sample_kernel, pps=pps, nbr=nbr),
        out_shape=(jax.ShapeDtypeStruct((b, t, 256), BF16),
                   jax.ShapeDtypeStruct((b, t, 128), F32)),
        grid_spec=grid_spec,
        compiler_params=_params(("parallel", "arbitrary")),
        name="fox_sample",
    )(page_table, *([kv_t] * (nbr * pps)), *([lf_t] * (nbr * pps)), p, p, p, p, fb)


def _mxu(x):
    return x.astype(BF16) if x.shape[-2] % 16 == 0 else x


def _bdot(a, b):
    return lax.dot_general(_mxu(a), _mxu(b), (((2,), (1,)), ((0,), (0,))), preferred_element_type=F32)


def _bdot_nt(a, b):
    return lax.dot_general(_mxu(a), _mxu(b), (((2,), (2,)), ((0,), (0,))), preferred_element_type=F32)


def _same_block(ii, jj, size):
    shift = size.bit_length() - 1
    return (ii >> shift) == (jj >> shift)


def _gdn_kernel(qkv_ref, z_ref, ba_ref, cw_ref, cb_ref, s0_ref, pa_ref, ng_ref, o_ref, st_ref, xb_ref):
    ci = pl.program_id(1)
    nb, c = qkv_ref.shape[0], qkv_ref.shape[1]
    pad = 8

    @pl.when(ci == 0)
    def _init():
        st_ref[...] = s0_ref[...]
        xb_ref[:, pad - (CONV_K - 1):pad, :] = cb_ref[...]

    ii = _iota((1, c, c), 1)
    jj = _iota((1, c, c), 2)
    tri = (_iota((c, c), 1) <= _iota((c, c), 0)).astype(F32)
    eye = ii == jj
    qs, ks, vs, bs, acs = [], [], [], [], []
    for bi in range(nb):
        xb_ref[bi, pad:pad + c, :] = qkv_ref[bi]
        conv = xb_ref[bi, pad - 3:pad - 3 + c, :] * cw_ref[0:1, :]
        for j in range(1, CONV_K):
            conv = conv + xb_ref[bi, pad - 3 + j:pad - 3 + j + c, :] * cw_ref[j:j + 1, :]
        tail = xb_ref[bi, pad + c - (CONV_K - 1):pad + c, :]
        xb_ref[bi, pad - (CONV_K - 1):pad, :] = tail
        conv = _silu(conv)
        ba = ba_ref[bi]
        beta = _sigmoid(ba)
        g = -jnp.exp(pa_ref[0:1, :]) * _softplus(ba + pa_ref[1:2, :])
        acum = _dot(tri, g, HIGHEST)
        for h in range(N_HEADS):
            qs.append(conv[:, h * HEAD_DIM:(h + 1) * HEAD_DIM])
            ks.append(conv[:, 256 + h * HEAD_DIM:256 + (h + 1) * HEAD_DIM])
            vs.append(conv[:, 512 + h * HEAD_DIM:512 + (h + 1) * HEAD_DIM])
            bs.append(beta[:, h:h + 1])
            acs.append(acum[:, N_HEADS + h:N_HEADS + h + 1])
    q = jnp.stack(qs, 0)
    k = jnp.stack(ks, 0)
    v = jnp.stack(vs, 0)
    bcol = jnp.stack(bs, 0)
    acol = jnp.stack(acs, 0)
    q = q * lax.rsqrt(jnp.sum(q * q, -1, keepdims=True) + NORM_EPS) * SCALE
    k = k * lax.rsqrt(jnp.sum(k * k, -1, keepdims=True) + NORM_EPS)
    arow = jnp.sum(jnp.where(eye, acol, 0.0), 1, keepdims=True)
    decay = jnp.exp(jnp.minimum(acol - arow, 0.0))
    kb = k * bcol
    m = _bdot_nt(kb, k) * jnp.where(jj < ii, decay, 0.0)
    base = min(8, c)
    md = jnp.where(_same_block(ii, jj, base), m, 0.0)
    e = -md
    pw = _bdot(md, md)
    n = 2
    while n < base:
        e = e + pw + _bdot(e, pw)
        n *= 2
        if n < base:
            pw = _bdot(pw, pw)
    size = base
    while size < c:
        off = jnp.where(_same_block(ii, jj, 2 * size) & ~_same_block(ii, jj, size), m, 0.0)
        t1 = off + _bdot(e, off)
        e = e - t1 - _bdot(t1, e)
        size *= 2
    ea = jnp.exp(acol)
    rhs = jnp.concatenate([v * bcol, kb * ea], 2)
    sol = rhs + _bdot(e, rhs)
    s = st_ref[...].reshape(nb * N_HEADS, HEAD_DIM, HEAD_DIM)
    v_new = sol[:, :, 0:HEAD_DIM] - _bdot(sol[:, :, HEAD_DIM:128], s)
    att = _bdot_nt(q, k) * jnp.where(jj <= ii, decay, 0.0)
    o = _bdot(q * ea, s) + _bdot(att, v_new)
    o = o * lax.rsqrt(jnp.mean(o * o, -1, keepdims=True) + NORM_EPS) * ng_ref[...]
    a_last = acol[:, c - 1:c, :]
    kd = k * jnp.exp(a_last - acol)
    for bi in range(nb):
        z = z_ref[bi]
        for h in range(N_HEADS):
            gi = bi * N_HEADS + h
            sl = slice(h * HEAD_DIM, (h + 1) * HEAD_DIM)
            st_ref[bi, h] = jnp.exp(a_last[gi]) * s[gi] + _dot_tn(_mxu(kd[gi]), _mxu(v_new[gi]))
            o_ref[bi, :, sl] = (o[gi] * _silu(z[:, sl])).astype(o_ref.dtype)


def _gdn_call(p, conv_buf, state0, conv_w, a_log, dt_bias, norm_g):
    b, t, _ = p.shape
    c = min(GDN_CHUNK, t)
    nb = max(d for d in (4, 2, 1) if b % d == 0)
    assert t % c == 0 and c >= CONV_K - 1
    pa = jnp.zeros((2, 128), F32).at[0, N_HEADS:2 * N_HEADS].set(a_log).at[1, N_HEADS:2 * N_HEADS].set(dt_bias)
    st = pl.BlockSpec((nb, N_HEADS, HEAD_DIM, HEAD_DIM), lambda b_, i: (b_, 0, 0, 0))
    return pl.pallas_call(
        _gdn_kernel,
        out_shape=(jax.ShapeDtypeStruct((b, t, 256), BF16),
                   jax.ShapeDtypeStruct((b, N_HEADS, HEAD_DIM, HEAD_DIM), F32)),
        grid=(b // nb, t // c),
        in_specs=[pl.BlockSpec((nb, c, 768), lambda b_, i: (b_, i, P_DQKV // 768)),
                  pl.BlockSpec((nb, c, 256), lambda b_, i: (b_, i, P_DZ // 256)),
                  pl.BlockSpec((nb, c, 128), lambda b_, i: (b_, i, P_DBA // 128)),
                  pl.BlockSpec((CONV_K, 768), lambda b_, i: (0, 0)),
                  pl.BlockSpec((nb, CONV_K - 1, 768), lambda b_, i: (b_, 0, 0)),
                  st,
                  pl.BlockSpec((2, 128), lambda b_, i: (0, 0)),
                  pl.BlockSpec((1, HEAD_DIM), lambda b_, i: (0, 0))],
        out_specs=(pl.BlockSpec((nb, c, 256), lambda b_, i: (b_, i, 0)), st),
        scratch_shapes=[pltpu.VMEM((nb, 8 + c, 768), F32)],
        compiler_params=_params(("parallel", "arbitrary")),
        name="gdn",
    )(p, p, p, conv_w, conv_buf, state0, pa, norm_g.reshape(1, HEAD_DIM))


def _outproj_kernel(oa_ref, ob_ref, oc_ref, od_ref, x_ref, g1_ref, sc2_ref, sh2_ref, w_ref,
                    l1g_ref, l1b_ref, rw_ref, rb_ref, x1_ref, u2_ref, wc_ref):
    y = _dot(oa_ref[0], w_ref[0:256, :])
    y = y + _dot(ob_ref[0], w_ref[256:512, :])
    y = y + _dot(oc_ref[0], w_ref[512:768, :])
    y = y + _dot(od_ref[0], w_ref[768:1024, :])
    x1 = _ln(DN_ALPHA * x_ref[0] + g1_ref[0] * y) * l1g_ref[...] + l1b_ref[...]
    x1_ref[0] = x1
    u2 = _ln(x1) * (1.0 + sc2_ref[0]) + sh2_ref[0]
    u2_ref[0] = u2.astype(BF16)
    scores = _sigmoid(_dot_nt(rw_ref[...], u2, HIGHEST))
    biased = scores + rb_ref[...]
    tm = biased.shape[1]
    per = N_EXPERTS // N_GROUPS
    idx = _iota((per, tm), 0).astype(F32)
    grp_rows = []
    for g in range(N_GROUPS):
        vals = biased[g * per:(g + 1) * per, :]
        m1 = jnp.max(vals, 0, keepdims=True)
        first = jnp.min(jnp.where(vals == m1, idx, float(per)), 0, keepdims=True)
        m2 = jnp.max(jnp.where(idx == first, -jnp.inf, vals), 0, keepdims=True)
        grp_rows.append(m1 + m2)
    gs = jnp.concatenate(grp_rows, 0)
    gi = _iota((N_GROUPS, tm), 0)
    rank = jnp.zeros((N_GROUPS, tm), F32)
    for g2 in range(N_GROUPS):
        row = gs[g2:g2 + 1, :]
        rank = rank + jnp.where((row > gs) | ((row == gs) & (g2 < gi)), 1.0, 0.0)
    keep = jnp.where(rank < float(TOPK_GROUPS), 1.0, 0.0)
    emask = jnp.concatenate([jnp.broadcast_to(keep[g:g + 1, :], (per, tm)) for g in range(N_GROUPS)], 0) > 0.5
    sel = _topk_mask(jnp.where(emask, biased, -jnp.inf), TOP_K, 0)
    w = sel * scores
    w = w / jnp.sum(w, 0, keepdims=True) * ROUTED_SCALE
    wc_ref[0] = jnp.concatenate([w, jnp.zeros((LANES - N_EXPERTS, tm), F32)], 0).T


def _outproj_call(oa, ob, oc, od, x, g1, sc2, sh2, w_out, ln_g, ln_b, rw_t, rb):
    g, r, _ = x.shape
    tm = min(512, r)
    o_spec = pl.BlockSpec((1, tm, 256), lambda g_, i: (g_, i, 0))
    x_spec = pl.BlockSpec((1, tm, D_MODEL), lambda g_, i: (g_, i, 0))
    vec = pl.BlockSpec((1, D_MODEL), lambda g_, i: (0, 0))
    return pl.pallas_call(
        _outproj_kernel,
        out_shape=(jax.ShapeDtypeStruct((g, r, D_MODEL), F32),
                   jax.ShapeDtypeStruct((g, r, D_MODEL), BF16),
                   jax.ShapeDtypeStruct((g, r, LANES), F32)),
        grid=(g, r // tm),
        in_specs=[o_spec, o_spec, o_spec, o_spec, x_spec,
                  _mod_spec(g1, tm), _mod_spec(sc2, tm), _mod_spec(sh2, tm),
                  pl.BlockSpec((D_MODEL, D_MODEL), lambda g_, i: (0, 0)),
                  vec, vec,
                  pl.BlockSpec((N_EXPERTS, D_MODEL), lambda g_, i: (0, 0)),
                  pl.BlockSpec((N_EXPERTS, 1), lambda g_, i: (0, 0))],
        out_specs=(x_spec, x_spec, pl.BlockSpec((1, tm, LANES), lambda g_, i: (g_, i, 0))),
        compiler_params=_params(("parallel", "parallel")),
        name="outproj",
    )(oa, ob, oc, od, x, g1, sc2, sh2, w_out, ln_g.reshape(1, -1), ln_b.reshape(1, -1), rw_t, rb.reshape(-1, 1))


EXPERTS_PER_STEP = 4


def _swiglu_act(hid):
    return _silu(hid[:, 0:EXPERT_FF]) * hid[:, EXPERT_FF:2 * EXPERT_FF]


def _moe_kernel(u_ref, wc_ref, x_ref, g2_ref, wgu_ref, wdn_ref, sgu_ref, sdn_ref, l2g_ref, l2b_ref,
                o_ref, acc_ref):
    step = pl.program_id(2)
    eps = wgu_ref.shape[0]
    u = u_ref[0]

    @pl.when(step == 0)
    def _shared():
        acc_ref[...] = _dot(_swiglu_act(_dot(u, sgu_ref[...])).astype(BF16), sdn_ref[...])

    wc = wc_ref[0]
    lane = _iota(wc.shape, 1)
    acts = []
    for k in range(eps):
        col = jnp.sum(jnp.where(lane == step * eps + k, wc, 0.0), -1, keepdims=True)
        acts.append((_swiglu_act(_dot(u, wgu_ref[k])) * col).astype(BF16))
    act = jnp.concatenate(acts, 1)
    acc_ref[...] += _dot(act, wdn_ref[...].reshape(eps * EXPERT_FF, D_MODEL))

    @pl.when(step == pl.num_programs(2) - 1)
    def _finish():
        o_ref[0] = _ln(DN_ALPHA * x_ref[0] + g2_ref[0] * acc_ref[...]) * l2g_ref[...] + l2b_ref[...]


def _moe_call(u2, wc, x1, g2, wgu, wdn, layer, sgu, sdn, ln_g, ln_b):
    g, r, _ = x1.shape
    tm = min(1024, r)
    eps = EXPERTS_PER_STEP
    tok = lambda width: pl.BlockSpec((1, tm, width), lambda g_, i, e: (g_, i, 0))
    if g2.shape[1] == 1:
        g2_spec = pl.BlockSpec((1, 1, D_MODEL), lambda g_, i, e: (g_, 0, 0))
    else:
        g2_spec = tok(D_MODEL)
    vec = pl.BlockSpec((1, D_MODEL), lambda g_, i, e: (0, 0))
    return pl.pallas_call(
        _moe_kernel,
        out_shape=jax.ShapeDtypeStruct((g, r, D_MODEL), F32),
        grid=(g, r // tm, N_EXPERTS // eps),
        in_specs=[tok(D_MODEL), tok(LANES), tok(D_MODEL), g2_spec,
                  pl.BlockSpec((None, eps, D_MODEL, 2 * EXPERT_FF), lambda g_, i, e: (layer, e, 0, 0)),
                  pl.BlockSpec((None, eps, EXPERT_FF, D_MODEL), lambda g_, i, e: (layer, e, 0, 0)),
                  pl.BlockSpec((D_MODEL, 2 * EXPERT_FF), lambda g_, i, e: (0, 0)),
                  pl.BlockSpec((EXPERT_FF, D_MODEL), lambda g_, i, e: (0, 0)),
                  vec, vec],
        out_specs=tok(D_MODEL),
        scratch_shapes=[pltpu.VMEM((tm, D_MODEL), F32)],
        compiler_params=_params(("parallel", "parallel", "arbitrary")),
        name="moe",
    )(u2, wc, x1, g2, wgu, wdn, sgu, sdn, ln_g.reshape(1, -1), ln_b.reshape(1, -1))


def _run_trunk(x, mod, p0, weights, stacked, past):
    b, t, _ = x.shape
    per_token = t < 128
    if per_token:
        grp = lambda a: a.reshape(1, b * t, a.shape[-1])
        mod_rows = lambda m: jnp.repeat(m, t, axis=0)[None]
    else:
        grp = lambda a: a
        mod_rows = lambda m: m[:, None, :]
    ungrp = lambda a: a.reshape(b, t, a.shape[-1])

    outs = []
    for l in range(DEPTH):
        w = {k: v[l] for k, v in weights.items()}
        sh1, sc1, g1, sh2, sc2, g2 = [mod_rows(m) for m in jnp.split(mod[l], 6, axis=-1)]
        p = ungrp(_proj_call(grp(x), sc1, sh1, w["w_in"]))
        if past is None:
            o_a, nsa_rows, win_rows = _nsa_prompt_call(p, w["nsa_pool"])
            win_new = win_rows[:, t - min(NSA_WINDOW, t):]
            o_c, logf = _fox_prompt_call(p, w["fox_f_bias"])
            ret_s0 = jnp.zeros((b, N_HEADS, HEAD_DIM, HEAD_DIM), F32)
            gdn_s0 = ret_s0
            conv_buf = jnp.zeros((b, CONV_K - 1, 3 * GROUP_WIDTH), F32)
        else:
            o_a, nsa_rows, win_new = _nsa_sample_call(p, past["nsa_t"], past["page_table"], l,
                                                      past["win_t"][l], w["nsa_pool"], p0)
            o_c, logf = _fox_sample_call(p, past["fox_kv_t"], past["fox_lf_t"], past["page_table"], l,
                                         w["fox_f_bias"])
            ret_s0, gdn_s0, conv_buf = past["state_ret"][l], past["state_gdn"][l], past["state_gdn_conv"][l]
        o_b, ret_s = _ret_call(p, ret_s0, w["ret_gn_g"], w["ret_gn_b"], p0)
        o_d, gdn_s = _gdn_call(p, conv_buf, gdn_s0, w["gdn_conv_w"], w["gdn_A_log"], w["gdn_dt_bias"],
                               w["gdn_norm_g"])
        x1, u2, wc = _outproj_call(grp(o_a), grp(o_b), grp(o_c), grp(o_d), grp(x), g1, sc2, sh2,
                                   w["w_out"], w["ln1_g"], w["ln1_b"], w["router_w_t"], w["router_b"])
        x = ungrp(_moe_call(u2, wc, x1, g2, stacked["exp_w_gu"], stacked["exp_w_down"], l,
                            w["sh_w_gu"], w["sh_w_down"], w["ln2_g"], w["ln2_b"]))
        qkv = p[:, :, P_DQKV:P_DQKV + 768]
        conv_new = jnp.concatenate([conv_buf, qkv], axis=1)[:, t:]
        outs.append((nsa_rows.reshape(b, t, 4, HEAD_DIM),
                     p[:, :, P_CK:P_CK + 512].reshape(b, t, 2, N_HEADS, HEAD_DIM),
                     logf[:, :, :N_HEADS],
                     win_new.reshape(b, win_new.shape[1], 2, HEAD_DIM),
                     ret_s, gdn_s, conv_new))
    nsa, fkv, flf, win, ret, gdn, conv = zip(*outs)
    return x, (jnp.stack(nsa, 1), jnp.stack(fkv, 1), jnp.stack(flf, 1), jnp.stack(win, 0),
               jnp.stack(ret, 0), jnp.stack(gdn, 0), jnp.stack(conv, 0))


def kernel(x_prompt, x_sample, cache_nsa, cache_fox_kv, cache_fox_logf, state_nsa_win, state_ret, state_gdn, state_gdn_conv, page_table, c_prompt, c_sample, w_mod, b_mod, w_in, w_out, nsa_pool, ret_gn_g, ret_gn_b, fox_f_bias, gdn_conv_w, gdn_A_log, gdn_dt_bias, gdn_norm_g, ln1_g, ln1_b, ln2_g, ln2_b, router_w, router_b, exp_w_gu, exp_w_down, sh_w_gu, sh_w_down):
    b = x_prompt.shape[0]
    db = x_sample.shape[0]
    n_pool, _, page_sz = cache_nsa.shape[:3]
    past_len = page_table.shape[1] * page_sz
    w_in_p = _permute_columns(w_in.astype(BF16), _proj_perm())
    weights = dict(
        w_in=w_in_p, w_out=w_out.astype(BF16), nsa_pool=nsa_pool, ret_gn_g=ret_gn_g, ret_gn_b=ret_gn_b,
        fox_f_bias=fox_f_bias, gdn_conv_w=gdn_conv_w, gdn_A_log=gdn_A_log, gdn_dt_bias=gdn_dt_bias,
        gdn_norm_g=gdn_norm_g, ln1_g=ln1_g, ln1_b=ln1_b, ln2_g=ln2_g, ln2_b=ln2_b,
        router_w_t=jnp.swapaxes(router_w, 1, 2), router_b=router_b,
        sh_w_gu=sh_w_gu.astype(BF16), sh_w_down=sh_w_down.astype(BF16))
    stacked = dict(exp_w_gu=exp_w_gu.astype(BF16), exp_w_down=exp_w_down.astype(BF16))
    n_c = b + db
    n_pad = -n_c % 8
    c_all = jnp.concatenate([c_prompt, c_sample, jnp.zeros((n_pad, D_MODEL), F32)], axis=0)
    mod = _mod_call(c_all, w_mod, b_mod)
    past = dict(
        nsa_t=jnp.transpose(cache_nsa, (0, 1, 3, 4, 2)).reshape(n_pool, DEPTH, 4 * HEAD_DIM, page_sz),
        fox_kv_t=jnp.transpose(cache_fox_kv, (0, 1, 3, 4, 5, 2)).reshape(n_pool, DEPTH, 2 * GROUP_WIDTH, page_sz),
        fox_lf_t=jnp.transpose(cache_fox_logf, (0, 1, 3, 2)),
        win_t=jnp.transpose(state_nsa_win, (0, 1, 3, 4, 2)).reshape(DEPTH, db, 2 * HEAD_DIM, state_nsa_win.shape[2]),
        state_ret=state_ret, state_gdn=state_gdn, state_gdn_conv=state_gdn_conv, page_table=page_table)
    y_p, (nsa_p, fkv_p, flf_p, win_p, ret_p, gdn_p, conv_p) = _run_trunk(x_prompt, mod[:, :b], 0, weights, stacked, None)
    y_s, (nsa_s, fkv_s, flf_s, win_s, ret_s, gdn_s, conv_s) = _run_trunk(x_sample, mod[:, b:n_c], past_len, weights, stacked, past)
    return (y_p, y_s, nsa_p, nsa_s, fkv_p, fkv_s, flf_p, flf_s, win_p, win_s,
            ret_p, ret_s, gdn_p, gdn_s, conv_p, conv_s)
```

```python
import functools
import math

import numpy as np
import jax
import jax.numpy as jnp
from jax import lax
from jax.experimental import pallas as pl
from jax.experimental.pallas import tpu as pltpu

F32 = jnp.float32
BF16 = jnp.bfloat16
HIGHEST = lax.Precision.HIGHEST

D_MODEL = 1024
DEPTH = 4
HEAD_DIM = 64
N_HEADS = 4
GROUP_WIDTH = N_HEADS * HEAD_DIM
NSA_BLOCK = 64
NSA_TOPN = 8
NSA_WINDOW = 512
NSA_FORCE = 1.0e4
ROPE_THETA = 500000.0
ROPE_DIMS = HEAD_DIM // 4
RET_THETA = 10000.0
RET_CHUNK = 128
GDN_CHUNK = 64
CONV_K = 4
N_EXPERTS = 64
TOP_K = 8
N_GROUPS = 8
TOPK_GROUPS = 4
EXPERT_FF = 256
ROUTED_SCALE = 2.5
DN_ALPHA = (2 * DEPTH) ** 0.25
LN_EPS = 1e-5
NORM_EPS = 1e-6
NEG_BIG = -1e30
SCALE = HEAD_DIM ** -0.5
QUERY_BLOCK = 256
LANES = 128
VMEM_LIMIT = 56 * 1024 * 1024

IN_SPLITS = (GROUP_WIDTH, 6 * HEAD_DIM, 3 * N_HEADS,
             GROUP_WIDTH, GROUP_WIDTH, GROUP_WIDTH, GROUP_WIDTH,
             GROUP_WIDTH, GROUP_WIDTH, GROUP_WIDTH, N_HEADS,
             3 * GROUP_WIDTH, N_HEADS, N_HEADS, GROUP_WIDTH)
IN_WIDTH = sum(IN_SPLITS)

P_AQ, P_BQ, P_BK, P_BV, P_BZ = 0, 256, 512, 768, 1024
P_CQ, P_CK, P_CV, P_DZ, P_DQKV = 1280, 1536, 1792, 2048, 2304
P_AKC, P_AKS, P_AKW, P_AG, P_CF, P_DBA = 3072, 3200, 3328, 3456, 3584, 3712
P_WIDTH = 3840


def _proj_perm():
    src = np.cumsum((0,) + IN_SPLITS)
    (q_a, kv_a, g_a, q_b, k_b, v_b, z_b, q_c, k_c, v_c, f_c, qkv_d, beta_d, a_d, z_d) = [int(s) for s in src[:-1]]
    perm = -np.ones((P_WIDTH,), np.int64)

    def put(dst, start, width):
        perm[dst:dst + width] = np.arange(start, start + width)

    put(P_AQ, q_a, 256)
    put(P_AKC, kv_a, 128)
    put(P_AKS, kv_a + 128, 128)
    put(P_AKW, kv_a + 256, 128)
    for h in range(N_HEADS):
        for j in range(3):
            perm[P_AG + j * N_HEADS + h] = g_a + h * 3 + j
    put(P_BQ, q_b, 256)
    put(P_BK, k_b, 256)
    put(P_BV, v_b, 256)
    put(P_BZ, z_b, 256)
    put(P_CQ, q_c, 256)
    put(P_CK, k_c, 256)
    put(P_CV, v_c, 256)
    put(P_CF, f_c, 4)
    put(P_DQKV, qkv_d, 768)
    put(P_DZ, z_d, 256)
    put(P_DBA, beta_d, 4)
    put(P_DBA + 4, a_d, 4)
    return perm


def _permute_columns(w, perm):
    pieces = []
    i, n = 0, len(perm)
    while i < n:
        j = i + 1
        if perm[i] < 0:
            while j < n and perm[j] < 0:
                j += 1
            pieces.append(jnp.zeros(w.shape[:-1] + (j - i,), w.dtype))
        else:
            while j < n and perm[j] == perm[j - 1] + 1:
                j += 1
            pieces.append(w[..., int(perm[i]):int(perm[i]) + (j - i)])
        i = j
    return jnp.concatenate(pieces, -1)


def _rope_tables(pos, n_rot, theta, n_heads, pad_identity=0):
    half = n_rot // 2
    inv = theta ** (-np.arange(half, dtype=np.float64) / half)
    ang = np.asarray(pos, np.float64)[:, None] * inv[None, :]
    t = ang.shape[0]
    c = np.ones((t, HEAD_DIM)); sa = np.zeros((t, HEAD_DIM)); sb = np.zeros((t, HEAD_DIM))
    c[:, :half] = np.cos(ang); c[:, half:n_rot] = np.cos(ang)
    sa[:, :half] = -np.sin(ang)
    sb[:, half:n_rot] = np.sin(ang)
    c = np.tile(c, (1, n_heads)); sa = np.tile(sa, (1, n_heads)); sb = np.tile(sb, (1, n_heads))
    if pad_identity:
        c = np.concatenate([c, np.ones((t, pad_identity))], 1)
        sa = np.concatenate([sa, np.zeros((t, pad_identity))], 1)
        sb = np.concatenate([sb, np.zeros((t, pad_identity))], 1)
    return tuple(jnp.asarray(a, F32) for a in (c, sa, sb))


def _dot(a, b, prec=None):
    return jnp.dot(a, b, preferred_element_type=F32, precision=prec)


def _dot_nt(a, b, prec=None):
    return lax.dot_general(a, b, (((1,), (1,)), ((), ())), preferred_element_type=F32, precision=prec)


def _dot_tn(a, b, prec=None):
    return lax.dot_general(a, b, (((0,), (0,)), ((), ())), preferred_element_type=F32, precision=prec)


def _iota(shape, axis):
    return lax.broadcasted_iota(jnp.int32, shape, axis)


def _ln(x):
    mu = jnp.mean(x, -1, keepdims=True)
    xc = x - mu
    var = jnp.mean(xc * xc, -1, keepdims=True)
    return xc * lax.rsqrt(var + LN_EPS)


def _sigmoid(x):
    return 1.0 / (1.0 + jnp.exp(-x))


def _silu(x):
    return x * _sigmoid(x)


def _softplus(x):
    return jnp.maximum(x, 0.0) + jnp.log1p(jnp.exp(-jnp.abs(x)))


def _log_sigmoid(x):
    return -_softplus(-x)


def _rope(x, c, sa, sb, half):
    w = x.shape[-1]
    return x * c + pltpu.roll(x, w - half, 1) * sa + pltpu.roll(x, half, 1) * sb


def _masked_softmax(s, mask, axis):
    s = jnp.where(mask, s, NEG_BIG)
    e = jnp.where(mask, jnp.exp(s - jnp.max(s, axis, keepdims=True)), 0.0)
    return e / jnp.maximum(jnp.sum(e, axis, keepdims=True), 1e-30)


def _softmax_pv(s, mask, v):
    s = jnp.where(mask, s, NEG_BIG)
    m = jnp.max(s, -1, keepdims=True)
    e = jnp.exp(s - m)
    den = jnp.sum(e, -1, keepdims=True)
    inv = jnp.where(m > 0.5 * NEG_BIG, 1.0 / jnp.maximum(den, 1e-30), 0.0)
    return _dot(e.astype(BF16), v) * inv


def _topk_mask(vals, k, axis):
    n = vals.shape[axis]
    idx = _iota(vals.shape, axis).astype(F32)
    sel = jnp.zeros(vals.shape, F32)
    work = vals
    for _ in range(k):
        m = jnp.max(work, axis, keepdims=True)
        first = jnp.min(jnp.where(work == m, idx, float(n)), axis, keepdims=True)
        pick = idx == first
        sel = jnp.where(pick, 1.0, sel)
        work = jnp.where(pick, -jnp.inf, work)
    return sel


def _pool_weights(pool_ref, n_rep):
    pl_t = pool_ref[...]
    e = jnp.exp(pl_t - jnp.max(pl_t, -1, keepdims=True))
    return e / (jnp.sum(e, -1, keepdims=True) / float(n_rep))


CAUSAL_GROUPS = 8


def _causal_branches(qi, n_qblocks, t, body):
    groups = CAUSAL_GROUPS if n_qblocks % CAUSAL_GROUPS == 0 else 1
    per = n_qblocks // groups
    for r in range(groups):
        pl.when((qi >= r * per) & (qi < (r + 1) * per))(functools.partial(body, (r + 1) * (t // groups)))


def _params(sem):
    return pltpu.CompilerParams(dimension_semantics=sem, vmem_limit_bytes=VMEM_LIMIT)


def _mod_kernel(c_ref, w_ref, b_ref, o_ref):
    o_ref[0] = _dot(c_ref[...].astype(BF16), w_ref[0].astype(BF16)) + b_ref[0]


def _mod_call(c_all, w_mod, b_mod):
    n = c_all.shape[0]
    tn = 1536
    return pl.pallas_call(
        _mod_kernel,
        out_shape=jax.ShapeDtypeStruct((DEPTH, n, 6 * D_MODEL), F32),
        grid=(DEPTH, 6 * D_MODEL // tn),
        in_specs=[pl.BlockSpec((n, D_MODEL), lambda l, j: (0, 0)),
                  pl.BlockSpec((1, D_MODEL, tn), lambda l, j: (l, 0, j)),
                  pl.BlockSpec((1, 1, tn), lambda l, j: (l, 0, j))],
        out_specs=pl.BlockSpec((1, n, tn), lambda l, j: (l, 0, j)),
        compiler_params=_params(("parallel", "parallel")),
        name="mod",
    )(c_all, w_mod, b_mod.reshape(DEPTH, 1, 6 * D_MODEL))


def _proj_kernel(x_ref, sc_ref, sh_ref, w_ref, o_ref):
    u = _ln(x_ref[0]) * (1.0 + sc_ref[0]) + sh_ref[0]
    o_ref[0] = _dot(u.astype(BF16), w_ref[...])


def _mod_spec(m, tm):
    if m.shape[1] == 1:
        return pl.BlockSpec((1, 1, D_MODEL), lambda g, i: (g, 0, 0))
    return pl.BlockSpec((1, tm, D_MODEL), lambda g, i: (g, i, 0))


def _proj_call(x, sc, sh, w):
    g, r, _ = x.shape
    tm = min(512, r)
    return pl.pallas_call(
        _proj_kernel,
        out_shape=jax.ShapeDtypeStruct((g, r, P_WIDTH), F32),
        grid=(g, r // tm),
        in_specs=[pl.BlockSpec((1, tm, D_MODEL), lambda g_, i: (g_, i, 0)),
                  _mod_spec(sc, tm), _mod_spec(sh, tm),
                  pl.BlockSpec((D_MODEL, P_WIDTH), lambda g_, i: (0, 0))],
        out_specs=pl.BlockSpec((1, tm, P_WIDTH), lambda g_, i: (g_, i, 0)),
        compiler_params=_params(("parallel", "parallel")),
        name="proj",
    )(x, sc, sh, w)


def _nsa_prompt_kernel(q_ref, g_ref, kc_ref, ks_ref, kw_ref, pool_ref,
                       qc_ref, qa_ref, qb_ref, kc_t_ref, ka_t_ref, kb_t_ref,
                       oa_ref, rows_ref, win_ref,
                       comp_ref, ksb_ref, vsb_ref, kwp_ref, vwp_ref, osel_ref):
    qi = pl.program_id(1)
    t = kc_ref.shape[1]
    nb = t // NSA_BLOCK
    qb = q_ref.shape[1]
    wnd = NSA_WINDOW

    @pl.when(qi == 0)
    def _prep():
        kcvc = kc_ref[0]
        ks_rot = _rope(ks_ref[0], kc_t_ref[...], ka_t_ref[...], kb_t_ref[...], ROPE_DIMS // 2)
        kw_rot = _rope(kw_ref[0], kc_t_ref[...], ka_t_ref[...], kb_t_ref[...], ROPE_DIMS // 2)
        rows_ref[0, :, 0:128] = kcvc
        rows_ref[0, :, 128:256] = ks_rot
        win_ref[0] = kw_rot
        ksb_ref[...] = ks_rot[:, 0:64].astype(BF16)
        vsb_ref[...] = ks_rot[:, 64:128].astype(BF16)
        kwp_ref[0:wnd, :] = jnp.zeros((wnd, HEAD_DIM), BF16)
        vwp_ref[0:wnd, :] = jnp.zeros((wnd, HEAD_DIM), BF16)
        kwp_ref[wnd:wnd + t, :] = kw_rot[:, 0:64].astype(BF16)
        vwp_ref[wnd:wnd + t, :] = kw_rot[:, 64:128].astype(BF16)
        wts = _pool_weights(pool_ref, nb)
        same = (_iota((nb, t), 1) >> 6) == _iota((nb, t), 0)
        pk = jnp.where(same, wts[0:1, :], 0.0)
        pv = jnp.where(same, wts[1:2, :], 0.0)
        ck = _dot(pk, kcvc, HIGHEST)
        cv = _dot(pv, kcvc, HIGHEST)
        comp_ref[...] = jnp.where(_iota((nb, 128), 1) < HEAD_DIM, ck, cv)

    s0 = pl.multiple_of(qi * qb, qb)
    q = q_ref[0] * SCALE
    qr = _rope(q, qc_ref[...], qa_ref[...], qb_ref[...], ROPE_DIMS // 2)
    gates = _sigmoid(g_ref[0])
    comp = comp_ref[...]
    compk = comp[:, 0:HEAD_DIM]
    compv = comp[:, HEAD_DIM:128]
    qp = s0 + _iota((qb, 1), 0)
    qp_row = s0 + _iota((1, qb), 1)
    blk = _iota((nb, 1), 0)
    cmask = blk < ((qp_row + 1) >> 6)
    imp = jnp.zeros((nb, qb), F32)
    o_cmp = []
    for h in range(N_HEADS):
        qh = q[:, h * HEAD_DIM:(h + 1) * HEAD_DIM]
        pc = _masked_softmax(_dot_nt(compk, qh, HIGHEST), cmask, 0)
        imp = imp + pc
        o_cmp.append(_dot_tn(pc, compv))
    cur = qp_row >> 6
    imp = jnp.where((blk == cur) | (blk == 0), NSA_FORCE, imp)
    imp = jnp.where(blk <= cur, imp, -1.0)
    sel = _topk_mask(imp, min(NSA_TOPN, nb), 0)
    qr_heads = [qr[:, h * HEAD_DIM:(h + 1) * HEAD_DIM].astype(BF16) for h in range(N_HEADS)]

    def _selected(ext):
        expand = ((_iota((nb, ext), 1) >> 6) == _iota((nb, ext), 0)).astype(F32)
        selk = _dot_tn(sel, expand)
        smask = (selk > 0.5) & (_iota((1, ext), 1) <= qp)
        ksb = ksb_ref[0:ext, :]
        vsb = vsb_ref[0:ext, :]
        for h in range(N_HEADS):
            osel_ref[:, h * HEAD_DIM:(h + 1) * HEAD_DIM] = _softmax_pv(_dot_nt(qr_heads[h], ksb), smask, vsb)

    _causal_branches(qi, t // qb, t, _selected)
    kw = kwp_ref[pl.ds(s0, wnd + qb), :]
    vw = vwp_ref[pl.ds(s0, wnd + qb), :]
    wpos = s0 - wnd + _iota((1, wnd + qb), 1)
    wmask = (wpos >= 0) & (wpos <= qp) & (wpos > qp - wnd)
    for h in range(N_HEADS):
        o_win = _softmax_pv(_dot_nt(qr_heads[h], kw), wmask, vw)
        o_sel = osel_ref[:, h * HEAD_DIM:(h + 1) * HEAD_DIM]
        out = (gates[:, h:h + 1] * o_cmp[h] + gates[:, N_HEADS + h:N_HEADS + h + 1] * o_sel
               + gates[:, 2 * N_HEADS + h:2 * N_HEADS + h + 1] * o_win)
        oa_ref[0, :, h * HEAD_DIM:(h + 1) * HEAD_DIM] = out.astype(oa_ref.dtype)


def _nsa_prompt_call(p, pool_l):
    b, t, _ = p.shape
    qb = QUERY_BLOCK
    nb = t // NSA_BLOCK
    pos = np.arange(t)
    q_tabs = _rope_tables(pos, ROPE_DIMS, ROPE_THETA, N_HEADS)
    k_tabs = _rope_tables(pos, ROPE_DIMS, ROPE_THETA, 1, pad_identity=HEAD_DIM)
    pool_t = jnp.tile(pool_l, (1, nb))
    full = lambda col: pl.BlockSpec((1, t, 128), lambda b_, i: (b_, 0, col))
    qtab = pl.BlockSpec((qb, 256), lambda b_, i: (i, 0))
    ktab = pl.BlockSpec((t, 128), lambda b_, i: (0, 0))
    return pl.pallas_call(
        _nsa_prompt_kernel,
        out_shape=(jax.ShapeDtypeStruct((b, t, 256), BF16),
                   jax.ShapeDtypeStruct((b, t, 256), F32),
                   jax.ShapeDtypeStruct((b, t, 128), F32)),
        grid=(b, t // qb),
        in_specs=[pl.BlockSpec((1, qb, 256), lambda b_, i: (b_, i, P_AQ // 256)),
                  pl.BlockSpec((1, qb, 128), lambda b_, i: (b_, i, P_AG // 128)),
                  full(P_AKC // 128), full(P_AKS // 128), full(P_AKW // 128),
                  pl.BlockSpec((2, t), lambda b_, i: (0, 0)),
                  qtab, qtab, qtab, ktab, ktab, ktab],
        out_specs=(pl.BlockSpec((1, qb, 256), lambda b_, i: (b_, i, 0)),
                   pl.BlockSpec((1, t, 256), lambda b_, i: (b_, 0, 0)),
                   pl.BlockSpec((1, t, 128), lambda b_, i: (b_, 0, 0))),
        scratch_shapes=[pltpu.VMEM((nb, 128), F32),
                        pltpu.VMEM((t, HEAD_DIM), BF16), pltpu.VMEM((t, HEAD_DIM), BF16),
                        pltpu.VMEM((NSA_WINDOW + t, HEAD_DIM), BF16), pltpu.VMEM((NSA_WINDOW + t, HEAD_DIM), BF16),
                        pltpu.VMEM((qb, GROUP_WIDTH), F32)],
        compiler_params=_params(("parallel", "arbitrary")),
        name="nsa_prompt",
    )(p, p, p, p, p, pool_t, *q_tabs, *k_tabs)


def _softmax2(s1, mask1, s2, mask2):
    s1 = jnp.where(mask1, s1, NEG_BIG)
    s2 = jnp.where(mask2, s2, NEG_BIG)
    m = jnp.maximum(jnp.max(s1, -1, keepdims=True), jnp.max(s2, -1, keepdims=True))
    e1 = jnp.where(mask1, jnp.exp(s1 - m), 0.0)
    e2 = jnp.where(mask2, jnp.exp(s2 - m), 0.0)
    den = jnp.maximum(jnp.sum(e1, -1, keepdims=True) + jnp.sum(e2, -1, keepdims=True), 1e-30)
    return e1, e2, den


def _nsa_sample_kernel(pt_ref, *refs, past, pps):
    page_refs = refs[:pps]
    (q_ref, g_ref, kc_ref, ks_ref, kw_ref, hist_ref, pool_ref,
     qc_ref, qa_ref, qb_ref, kc_t_ref, ka_t_ref, kb_t_ref,
     oa_ref, rows_ref, win_ref, comp_ref, kst_ref, vst_ref) = refs[pps:]
    i = pl.program_id(1)
    t = q_ref.shape[1]
    page_sz = page_refs[0].shape[3]
    n_pages = past // page_sz
    cr = comp_ref.shape[0]
    wts = _pool_weights(pool_ref, page_sz // NSA_BLOCK)
    r16 = _iota((16, page_sz), 0)
    half16 = _iota((16, page_sz), 1) >> 6
    pkv = jnp.where((r16 == half16), wts[0:1, :], 0.0) + jnp.where((r16 - 8 == half16), wts[1:2, :], 0.0)
    lane_lo = _iota((8, 128), 1) < HEAD_DIM

    @pl.when(i == 0)
    def _init():
        comp_ref[cr - 16:cr, :] = jnp.zeros((16, 128), F32)

    pkv_hi = pkv.astype(BF16)
    pkv_split = jnp.concatenate([pkv_hi, (pkv - pkv_hi.astype(F32)).astype(BF16)], 0)
    for j in range(pps):
        page_t = page_refs[j][0, 0]
        x = page_t[0:128, :]
        x_hi = x.astype(BF16)
        x_lo = (x - x_hi.astype(F32)).astype(BF16)
        r_hi = _dot_nt(pkv_split, x_hi)
        res = r_hi[0:16] + r_hi[16:32] + _dot_nt(pkv_hi, x_lo)
        pg = i * pps + j
        comp_ref[pl.ds(pl.multiple_of(pg * 8, 8), 8), :] = jnp.where(lane_lo, res[0:8], res[8:16])
        col0 = pl.multiple_of(pg * page_sz, page_sz)
        kst_ref[:, pl.ds(col0, page_sz)] = page_t[128:192, :].astype(BF16)
        vst_ref[:, pl.ds(col0, page_sz)] = page_t[192:256, :].astype(BF16)

    @pl.when(i == pl.num_programs(1) - 1)
    def _finish():
        kcvc = kc_ref[0]
        ks_rot = _rope(ks_ref[0], kc_t_ref[...], ka_t_ref[...], kb_t_ref[...], ROPE_DIMS // 2)
        kw_rot = _rope(kw_ref[0], kc_t_ref[...], ka_t_ref[...], kb_t_ref[...], ROPE_DIMS // 2)
        rows_ref[0, :, 0:128] = kcvc
        rows_ref[0, :, 128:256] = ks_rot
        res_n = _dot(pkv[:, 0:t], kcvc, HIGHEST)
        comp_ref[cr - 16:cr - 8, :] = jnp.where(lane_lo, res_n[0:8], res_n[8:16])

        q = q_ref[0]
        qr = _rope(q, qc_ref[...], qa_ref[...], qb_ref[...], ROPE_DIMS // 2)
        zpad = jnp.zeros((32 - t, HEAD_DIM), F32)
        stack = lambda x: jnp.concatenate(
            [piece for h in range(N_HEADS) for piece in (x[:, h * HEAD_DIM:(h + 1) * HEAD_DIM], zpad)], 0)
        q_all = stack(q)
        qr_all = stack(qr)
        qidx = _iota((128, 1), 0) & 31
        qp = past + qidx

        comp = comp_ref[...]
        compk = comp[:, 0:HEAD_DIM]
        compv = comp[:, HEAD_DIM:128]
        ci = _iota((1, cr), 1)
        blk = 2 * (ci >> 3) + (ci & 7)
        valid = ((ci & 7) < 2) & (blk * NSA_BLOCK < past + t)
        cmask = valid & (blk < ((qp + 1) >> 6))
        pc = _masked_softmax(_dot_nt(q_all, compk, HIGHEST) * SCALE, cmask, -1)
        o_cmp = _dot(pc, compv)
        imp = pc[0:32] + pc[32:64] + pc[64:96] + pc[96:128]
        cur = qp[0:32] >> 6
        imp = jnp.where((blk == cur) | (blk == 0), NSA_FORCE, imp)
        imp = jnp.where(blk <= cur, imp, -1.0)
        imp = jnp.where(valid, imp, -2.0)
        sel32 = _topk_mask(imp, NSA_TOPN, 1)
        sel = jnp.concatenate([sel32] * N_HEADS, 0)

        lane_half = _iota((128, page_sz), 1) < NSA_BLOCK
        selk = jnp.concatenate(
            [jnp.where(lane_half, sel[:, 8 * pg:8 * pg + 1], sel[:, 8 * pg + 1:8 * pg + 2]) for pg in range(n_pages)], 1)
        smask = (selk > 0.5) & (_iota((1, past), 1) <= qp)
        new_idx = _iota((1, t), 1)
        nmask = (sel[:, cr - 16:cr - 15] > 0.5) & (new_idx <= qidx)
        s_past = _dot(qr_all.astype(BF16), kst_ref[...]) * SCALE
        s_new = _dot_nt(qr_all, ks_rot[:, 0:HEAD_DIM]) * SCALE
        e1, e2, den = _softmax2(s_past, smask, s_new, nmask)
        o_sel = (_dot_nt(e1.astype(BF16), vst_ref[...]) + _dot(e2, ks_rot[:, HEAD_DIM:128])) / den

        hist_t = hist_ref[0]
        wb = hist_t.shape[1]
        wpos = past - wb + _iota((1, wb), 1)
        hmask = (wpos >= 0) & (wpos <= qp) & (wpos > qp - NSA_WINDOW)
        wmask = (new_idx <= qidx) & (past + new_idx > qp - NSA_WINDOW)
        s_hist = _dot(qr_all, hist_t[0:HEAD_DIM, :]) * SCALE
        s_wnew = _dot_nt(qr_all, kw_rot[:, 0:HEAD_DIM]) * SCALE
        e1, e2, den = _softmax2(s_hist, hmask, s_wnew, wmask)
        o_win = (_dot_nt(e1, hist_t[HEAD_DIM:128, :]) + _dot(e2, kw_rot[:, HEAD_DIM:128])) / den
        hist_tok = hist_t.T
        win_ref[0, 0:wb - t, :] = hist_tok[t:wb, :]
        win_ref[0, wb - t:wb, :] = kw_rot

        gates = _sigmoid(g_ref[0])
        for h in range(N_HEADS):
            r = slice(h * 32, h * 32 + t)
            out = (gates[:, h:h + 1] * o_cmp[r] + gates[:, N_HEADS + h:N_HEADS + h + 1] * o_sel[r]
                   + gates[:, 2 * N_HEADS + h:2 * N_HEADS + h + 1] * o_win[r])
            oa_ref[0, :, h * HEAD_DIM:(h + 1) * HEAD_DIM] = out.astype(oa_ref.dtype)


def _pages_per_step(n_pages):
    return max(d for d in (8, 4, 2, 1) if n_pages % d == 0)


def _nsa_sample_call(p, cache_t, page_table, layer, hist_t, pool_l, past):
    b, t, _ = p.shape
    n_pages = page_table.shape[1]
    page_sz = cache_t.shape[3]
    wb = hist_t.shape[2]
    assert page_sz == 128 and t == 8 and past == n_pages * page_sz and wb == NSA_WINDOW and past >= wb
    pps = _pages_per_step(n_pages)
    pos = past + np.arange(t)
    q_tabs = _rope_tables(pos, ROPE_DIMS, ROPE_THETA, N_HEADS)
    k_tabs = _rope_tables(pos, ROPE_DIMS, ROPE_THETA, 1, pad_identity=HEAD_DIM)
    pool_t = jnp.tile(pool_l, (1, page_sz // NSA_BLOCK))
    new = lambda col: pl.BlockSpec((1, t, 128), lambda b_, i, pt: (b_, 0, col))
    const = lambda shape: pl.BlockSpec(shape, lambda b_, i, pt: (0,) * len(shape))
    page_spec = lambda j: pl.BlockSpec((1, 1, 256, page_sz), lambda b_, i, pt: (pt[b_, i * pps + j], layer, 0, 0))
    grid_spec = pltpu.PrefetchScalarGridSpec(
        num_scalar_prefetch=1,
        grid=(b, n_pages // pps),
        in_specs=[page_spec(j) for j in range(pps)] + [
            pl.BlockSpec((1, t, 256), lambda b_, i, pt: (b_, 0, P_AQ // 256)),
            new(P_AG // 128), new(P_AKC // 128), new(P_AKS // 128), new(P_AKW // 128),
            pl.BlockSpec((1, 128, wb), lambda b_, i, pt: (b_, 0, 0)),
            const((2, page_sz)),
            const((t, 256)), const((t, 256)), const((t, 256)),
            const((t, 128)), const((t, 128)), const((t, 128))],
        out_specs=(pl.BlockSpec((1, t, 256), lambda b_, i, pt: (b_, 0, 0)),
                   pl.BlockSpec((1, t, 256), lambda b_, i, pt: (b_, 0, 0)),
                   pl.BlockSpec((1, wb, 128), lambda b_, i, pt: (b_, 0, 0))),
        scratch_shapes=[pltpu.VMEM((8 * n_pages + 16, 128), F32),
                        pltpu.VMEM((HEAD_DIM, past), BF16),
                        pltpu.VMEM((HEAD_DIM, past), BF16)])
    return pl.pallas_call(
        functools.partial(_nsa_sample_kernel, past=past, pps=pps),
        out_shape=(jax.ShapeDtypeStruct((b, t, 256), BF16),
                   jax.ShapeDtypeStruct((b, t, 256), F32),
                   jax.ShapeDtypeStruct((b, wb, 128), F32)),
        grid_spec=grid_spec,
        compiler_params=_params(("parallel", "arbitrary")),
        name="nsa_sample",
    )(page_table, *([cache_t] * pps), p, p, p, p, p, hist_t, pool_t, *q_tabs, *k_tabs)


def _ret_kernel(q_ref, k_ref, v_ref, z_ref, s0_ref, c_ref, sa_ref, sb_ref, gng_ref, gnb_ref,
                o_ref, st_ref):
    ci = pl.program_id(1)
    nb, c = q_ref.shape[0], q_ref.shape[1]

    @pl.when(ci == 0)
    def _init():
        st_ref[...] = s0_ref[...]

    tabs = (c_ref[...], sa_ref[...], sb_ref[...])
    qs, ks, vs = [], [], []
    for bi in range(nb):
        q = _rope(q_ref[bi], *tabs, HEAD_DIM // 2)
        k = _rope(k_ref[bi], *tabs, HEAD_DIM // 2) * SCALE
        v = v_ref[bi]
        for h in range(N_HEADS):
            sl = slice(h * HEAD_DIM, (h + 1) * HEAD_DIM)
            qs.append(q[:, sl])
            ks.append(k[:, sl])
            vs.append(v[:, sl])
    q3 = jnp.stack(qs, 0)
    k3 = jnp.stack(ks, 0)
    v3 = jnp.stack(vs, 0)
    lgs = [math.log1p(-2.0 ** (-5.0 - h)) for h in range(N_HEADS)]
    ii = _iota((c, c), 0)
    jj = _iota((c, c), 1)
    diff = (ii - jj).astype(F32)
    rowi = _iota((c, 1), 0).astype(F32)
    dec = jnp.stack([jnp.where(jj <= ii, jnp.exp(jnp.minimum(diff * lg, 0.0)), 0.0) for lg in lgs] * nb, 0)
    ea = jnp.stack([jnp.exp((rowi + 1.0) * lg) for lg in lgs] * nb, 0)
    eb = jnp.stack([jnp.exp((c - 1.0 - rowi) * lg) for lg in lgs] * nb, 0)
    s = st_ref[...].reshape(nb * N_HEADS, HEAD_DIM, HEAD_DIM)
    att = _bdot_nt(q3, k3) * dec
    o = _bdot(att, v3) + _bdot(q3 * ea, s)
    kd = k3 * eb
    mu = jnp.mean(o, -1, keepdims=True)
    oc = o - mu
    var = jnp.mean(oc * oc, -1, keepdims=True)
    on = oc * lax.rsqrt(var + NORM_EPS)
    for bi in range(nb):
        z = z_ref[bi]
        for h in range(N_HEADS):
            gi = bi * N_HEADS + h
            sl = slice(h * HEAD_DIM, (h + 1) * HEAD_DIM)
            st_ref[bi, h] = math.exp(c * lgs[h]) * s[gi] + _dot_tn(_mxu(kd[gi]), _mxu(v3[gi]))
            o_ref[bi, :, sl] = ((on[gi] * gng_ref[:, sl] + gnb_ref[:, sl]) * _silu(z[:, sl])).astype(o_ref.dtype)


def _ret_call(p, state0, gn_g, gn_b, p0):
    b, t, _ = p.shape
    c = min(RET_CHUNK, t)
    assert t % c == 0
    tabs = _rope_tables(p0 + np.arange(t), HEAD_DIM, RET_THETA, N_HEADS)
    nb = 2 if b % 2 == 0 else 1
    blk = lambda col: pl.BlockSpec((nb, c, 256), lambda b_, i: (b_, i, col))
    tab = pl.BlockSpec((c, 256), lambda b_, i: (i, 0))
    st = pl.BlockSpec((nb, N_HEADS, HEAD_DIM, HEAD_DIM), lambda b_, i: (b_, 0, 0, 0))
    vec = pl.BlockSpec((1, 256), lambda b_, i: (0, 0))
    return pl.pallas_call(
        _ret_kernel,
        out_shape=(jax.ShapeDtypeStruct((b, t, 256), BF16),
                   jax.ShapeDtypeStruct((b, N_HEADS, HEAD_DIM, HEAD_DIM), F32)),
        grid=(b // nb, t // c),
        in_specs=[blk(P_BQ // 256), blk(P_BK // 256), blk(P_BV // 256), blk(P_BZ // 256), st,
                  tab, tab, tab, vec, vec],
        out_specs=(pl.BlockSpec((nb, c, 256), lambda b_, i: (b_, i, 0)), st),
        compiler_params=_params(("parallel", "arbitrary")),
        name="ret",
    )(p, p, p, p, state0, *tabs, gn_g.reshape(1, 256), gn_b.reshape(1, 256))


def _fox_prompt_kernel(q_ref, k_ref, v_ref, f_ref, fb_ref, o_ref, lf_ref, cum_ref, cumt_ref, kb_ref, vb_ref):
    qi = pl.program_id(1)
    t = k_ref.shape[1]
    qb = q_ref.shape[1]

    @pl.when(qi == 0)
    def _prep():
        lf = _log_sigmoid(f_ref[0] + fb_ref[...])
        lf_ref[0] = lf
        tri = (_iota((qb, qb), 1) <= _iota((qb, qb), 0)).astype(F32)
        carry = jnp.zeros((1, 128), F32)
        for c in range(t // qb):
            blk = _dot(tri, lf[c * qb:(c + 1) * qb], HIGHEST) + carry
            cum_ref[c * qb:(c + 1) * qb, :] = blk
            carry = blk[qb - 1:qb, :]
        cumt_ref[...] = cum_ref[...].T
        for h in range(N_HEADS):
            kb_ref[h] = k_ref[0, :, h * HEAD_DIM:(h + 1) * HEAD_DIM].astype(BF16)
            vb_ref[h] = v_ref[0, :, h * HEAD_DIM:(h + 1) * HEAD_DIM].astype(BF16)

    s0 = pl.multiple_of(qi * qb, qb)
    q = (q_ref[0] * SCALE).astype(BF16)
    cq = cum_ref[pl.ds(s0, qb), :]
    qrow = s0 + _iota((qb, 1), 0)

    def _attend(ext):
        mask = _iota((1, ext), 1) <= qrow
        for h in range(N_HEADS):
            qh = q[:, h * HEAD_DIM:(h + 1) * HEAD_DIM]
            s = _dot_nt(qh, kb_ref[h, 0:ext, :]) + (cq[:, h:h + 1] - cumt_ref[h:h + 1, 0:ext])
            o = _softmax_pv(s, mask, vb_ref[h, 0:ext, :])
            o_ref[0, :, h * HEAD_DIM:(h + 1) * HEAD_DIM] = o.astype(o_ref.dtype)

    _causal_branches(qi, t // qb, t, _attend)


def _fox_prompt_call(p, f_bias):
    b, t, _ = p.shape
    qb = QUERY_BLOCK
    fb = jnp.zeros((1, 128), F32).at[0, :N_HEADS].set(f_bias)
    return pl.pallas_call(
        _fox_prompt_kernel,
        out_shape=(jax.ShapeDtypeStruct((b, t, 256), BF16),
                   jax.ShapeDtypeStruct((b, t, 128), F32)),
        grid=(b, t // qb),
        in_specs=[pl.BlockSpec((1, qb, 256), lambda b_, i: (b_, i, P_CQ // 256)),
                  pl.BlockSpec((1, t, 256), lambda b_, i: (b_, 0, P_CK // 256)),
                  pl.BlockSpec((1, t, 256), lambda b_, i: (b_, 0, P_CV // 256)),
                  pl.BlockSpec((1, t, 128), lambda b_, i: (b_, 0, P_CF // 128)),
                  pl.BlockSpec((1, 128), lambda b_, i: (0, 0))],
        out_specs=(pl.BlockSpec((1, qb, 256), lambda b_, i: (b_, i, 0)),
                   pl.BlockSpec((1, t, 128), lambda b_, i: (b_, 0, 0))),
        scratch_shapes=[pltpu.VMEM((t, 128), F32), pltpu.VMEM((128, t), F32),
                        pltpu.VMEM((N_HEADS, t, HEAD_DIM), BF16), pltpu.VMEM((N_HEADS, t, HEAD_DIM), BF16)],
        compiler_params=_params(("parallel", "arbitrary")),
        name="fox_prompt",
    )(p, p, p, p, fb)


def _rows_per_head(x, rows):
    return jnp.concatenate([jnp.broadcast_to(x[h:h + 1, :], (rows, x.shape[1])) for h in range(N_HEADS)], 0)


def _fox_sample_kernel(pt_ref, *refs, pps, nbr):
    kv_refs = refs[:nbr * pps]
    lf_refs = refs[nbr * pps:2 * nbr * pps]
    (q_ref, k_ref, v_ref, f_ref, fb_ref, o_ref, lf_ref,
     qbd_ref, m_ref, l_ref, acc_ref, carry_ref, cnew_ref) = refs[2 * nbr * pps:]
    i = pl.program_id(1)
    t = q_ref.shape[1]
    page_sz = kv_refs[0].shape[3]
    qidx = _iota((128, 1), 0) & 31

    @pl.when(i == 0)
    def _init():
        for r in range(nbr):
            q = q_ref[r]
            col_head = _iota((t, 256), 1) >> 6
            zpad = jnp.zeros((32 - t, 256), F32)
            qbd = jnp.concatenate(
                [piece for h in range(N_HEADS) for piece in (jnp.where(col_head == h, q, 0.0), zpad)], 0)
            qbd_ref[r] = qbd * SCALE
            lf = _log_sigmoid(f_ref[r] + fb_ref[...])
            lf_ref[r] = lf
            tri = (_iota((t, t), 1) <= _iota((t, t), 0)).astype(F32)
            cs = _dot(tri, lf, HIGHEST)
            zcol = jnp.zeros((32 - t, 1), F32)
            cnew = jnp.concatenate([piece for h in range(N_HEADS) for piece in (cs[:, h:h + 1], zcol)], 0)
            cnew_ref[r] = jnp.broadcast_to(cnew, (128, 128))
            eye = _iota((t, t), 0) == _iota((t, t), 1)
            cs_rows = jnp.concatenate(
                [jnp.broadcast_to(jnp.sum(jnp.where(eye, cs[:, h:h + 1], 0.0), 0, keepdims=True), (32, t))
                 for h in range(N_HEADS)], 0)
            s = _dot_nt(qbd * SCALE, k_ref[r]) + cnew - cs_rows
            mask = _iota((1, t), 1) <= qidx
            s = jnp.where(mask, s, NEG_BIG)
            m = jnp.max(s, -1, keepdims=True)
            e = jnp.where(mask, jnp.exp(s - m), 0.0)
            m_ref[r] = jnp.broadcast_to(m, (128, 128))
            l_ref[r] = jnp.broadcast_to(jnp.sum(e, -1, keepdims=True), (128, 128))
            acc_ref[r] = _dot(e, v_ref[r])
            carry_ref[r] = jnp.zeros((8, 128), F32)

    lane = _iota((N_HEADS, page_sz), 1)
    for r in range(nbr):
        carry = carry_ref[r, 0:N_HEADS, :]
        qbd_bf = qbd_ref[r].astype(BF16)
        cnew = cnew_ref[r, :, 0:1]
        tiles = []
        for j in range(pps):
            lf_t = lf_refs[r * pps + j][0, 0]
            incl = lf_t
            d = 1
            while d < page_sz:
                incl = incl + jnp.where(lane < page_sz - d, pltpu.roll(incl, page_sz - d, 1), 0.0)
                d *= 2
            bias = _rows_per_head(incl - lf_t + carry, 32) + cnew
            carry = carry + incl[:, 0:1]
            tiles.append(_dot(qbd_bf, kv_refs[r * pps + j][0, 0, 0:256, :].astype(BF16)) + bias)
        carry_ref[r, 0:N_HEADS, :] = carry
        s = jnp.concatenate(tiles, 1)
        m_old = m_ref[r, :, 0:1]
        m_new = jnp.maximum(m_old, jnp.max(s, -1, keepdims=True))
        alpha = jnp.exp(m_old - m_new)
        e = jnp.exp(s - m_new)
        m_ref[r] = jnp.broadcast_to(m_new, (128, 128))
        l_ref[r] = alpha * l_ref[r] + jnp.sum(e, -1, keepdims=True)
        acc = alpha * acc_ref[r]
        for j in range(pps):
            acc = acc + _dot_nt(e[:, j * page_sz:(j + 1) * page_sz].astype(BF16),
                                kv_refs[r * pps + j][0, 0, 256:512, :].astype(BF16))
        acc_ref[r] = acc

    @pl.when(i == pl.num_programs(1) - 1)
    def _finish():
        for r in range(nbr):
            o = acc_ref[r] / jnp.maximum(l_ref[r, :, 0:1], 1e-30)
            for h in range(N_HEADS):
                sl = slice(h * HEAD_DIM, (h + 1) * HEAD_DIM)
                o_ref[r, :, sl] = o[h * 32:h * 32 + t, sl].astype(o_ref.dtype)


def _fox_sample_call(p, kv_t, lf_t, page_table, layer, f_bias):
    b, t, _ = p.shape
    n_pages = page_table.shape[1]
    page_sz = kv_t.shape[3]
    assert t == 8 and page_sz == 128
    pps = _pages_per_step(n_pages)
    nbr = 2 if b % 2 == 0 else 1
    fb = jnp.zeros((1, 128), F32).at[0, :N_HEADS].set(f_bias)
    rev = lambda r, j: (lambda b_, i, pt: (pt[b_ * nbr + r, n_pages - 1 - (i * pps + j)], layer, 0, 0))
    slots = [(r, j) for r in range(nbr) for j in range(pps)]
    new = lambda width, col: pl.BlockSpec((nbr, t, width), lambda b_, i, pt: (b_, 0, col))
    grid_spec = pltpu.PrefetchScalarGridSpec(
        num_scalar_prefetch=1,
        grid=(b // nbr, n_pages // pps),
        in_specs=[pl.BlockSpec((1, 1, 512, page_sz), rev(r, j)) for r, j in slots]
        + [pl.BlockSpec((1, 1, N_HEADS, page_sz), rev(r, j)) for r, j in slots]
        + [new(256, P_CQ // 256), new(256, P_CK // 256), new(256, P_CV // 256), new(128, P_CF // 128),
           pl.BlockSpec((1, 128), lambda b_, i, pt: (0, 0))],
        out_specs=(pl.BlockSpec((nbr, t, 256), lambda b_, i, pt: (b_, 0, 0)),
                   pl.BlockSpec((nbr, t, 128), lambda b_, i, pt: (b_, 0, 0))),
        scratch_shapes=[pltpu.VMEM((nbr, 128, 256), F32), pltpu.VMEM((nbr, 128, 128), F32),
                        pltpu.VMEM((nbr, 128, 128), F32), pltpu.VMEM((nbr, 128, 256), F32),
                        pltpu.VMEM((nbr, 8, 128), F32), pltpu.VMEM((nbr, 128, 128), F32)])
    return pl.pallas_call(
        functools.partial(_fox_sample_kernel, pps=pps, nbr=nbr),
        out_shape=(jax.ShapeDtypeStruct((b, t, 256), BF16),
                   jax.ShapeDtypeStruct((b, t, 128), F32)),
        grid_spec=grid_spec,
        compiler_params=_params(("parallel", "arbitrary")),
        name="fox_sample",
    )(page_table, *([kv_t] * (nbr * pps)), *([lf_t] * (nbr * pps)), p, p, p, p, fb)


def _mxu(x):
    return x.astype(BF16) if x.shape[-2] % 16 == 0 else x


def _bdot(a, b):
    return lax.dot_general(_mxu(a), _mxu(b), (((2,), (1,)), ((0,), (0,))), preferred_element_type=F32)


def _bdot_nt(a, b):
    return lax.dot_general(_mxu(a), _mxu(b), (((2,), (2,)), ((0,), (0,))), preferred_element_type=F32)


def _same_block(ii, jj, size):
    shift = size.bit_length() - 1
    return (ii >> shift) == (jj >> shift)


def _gdn_kernel(qkv_ref, z_ref, ba_ref, cw_ref, cb_ref, s0_ref, pa_ref, ng_ref, o_ref, st_ref, xb_ref):
    ci = pl.program_id(1)
    nb, c = qkv_ref.shape[0], qkv_ref.shape[1]
    pad = 8

    @pl.when(ci == 0)
    def _init():
        st_ref[...] = s0_ref[...]
        xb_ref[:, pad - (CONV_K - 1):pad, :] = cb_ref[...]

    ii = _iota((1, c, c), 1)
    jj = _iota((1, c, c), 2)
    tri = (_iota((c, c), 1) <= _iota((c, c), 0)).astype(F32)
    eye = ii == jj
    qs, ks, vs, bs, acs = [], [], [], [], []
    for bi in range(nb):
        xb_ref[bi, pad:pad + c, :] = qkv_ref[bi]
        conv = xb_ref[bi, pad - 3:pad - 3 + c, :] * cw_ref[0:1, :]
        for j in range(1, CONV_K):
            conv = conv + xb_ref[bi, pad - 3 + j:pad - 3 + j + c, :] * cw_ref[j:j + 1, :]
        tail = xb_ref[bi, pad + c - (CONV_K - 1):pad + c, :]
        xb_ref[bi, pad - (CONV_K - 1):pad, :] = tail
        conv = _silu(conv)
        ba = ba_ref[bi]
        beta = _sigmoid(ba)
        g = -jnp.exp(pa_ref[0:1, :]) * _softplus(ba + pa_ref[1:2, :])
        acum = _dot(tri, g, HIGHEST)
        for h in range(N_HEADS):
            qs.append(conv[:, h * HEAD_DIM:(h + 1) * HEAD_DIM])
            ks.append(conv[:, 256 + h * HEAD_DIM:256 + (h + 1) * HEAD_DIM])
            vs.append(conv[:, 512 + h * HEAD_DIM:512 + (h + 1) * HEAD_DIM])
            bs.append(beta[:, h:h + 1])
            acs.append(acum[:, N_HEADS + h:N_HEADS + h + 1])
    q = jnp.stack(qs, 0)
    k = jnp.stack(ks, 0)
    v = jnp.stack(vs, 0)
    bcol = jnp.stack(bs, 0)
    acol = jnp.stack(acs, 0)
    q = q * lax.rsqrt(jnp.sum(q * q, -1, keepdims=True) + NORM_EPS) * SCALE
    k = k * lax.rsqrt(jnp.sum(k * k, -1, keepdims=True) + NORM_EPS)
    arow = jnp.sum(jnp.where(eye, acol, 0.0), 1, keepdims=True)
    decay = jnp.exp(jnp.minimum(acol - arow, 0.0))
    kb = k * bcol
    m = _bdot_nt(kb, k) * jnp.where(jj < ii, decay, 0.0)
    base = min(8, c)
    md = jnp.where(_same_block(ii, jj, base), m, 0.0)
    e = -md
    pw = _bdot(md, md)
    n = 2
    while n < base:
        e = e + pw + _bdot(e, pw)
        n *= 2
        if n < base:
            pw = _bdot(pw, pw)
    size = base
    while size < c:
        off = jnp.where(_same_block(ii, jj, 2 * size) & ~_same_block(ii, jj, size), m, 0.0)
        t1 = off + _bdot(e, off)
        e = e - t1 - _bdot(t1, e)
        size *= 2
    ea = jnp.exp(acol)
    rhs = jnp.concatenate([v * bcol, kb * ea], 2)
    sol = rhs + _bdot(e, rhs)
    s = st_ref[...].reshape(nb * N_HEADS, HEAD_DIM, HEAD_DIM)
    v_new = sol[:, :, 0:HEAD_DIM] - _bdot(sol[:, :, HEAD_DIM:128], s)
    att = _bdot_nt(q, k) * jnp.where(jj <= ii, decay, 0.0)
    o = _bdot(q * ea, s) + _bdot(att, v_new)
    o = o * lax.rsqrt(jnp.mean(o * o, -1, keepdims=True) + NORM_EPS) * ng_ref[...]
    a_last = acol[:, c - 1:c, :]
    kd = k * jnp.exp(a_last - acol)
    for bi in range(nb):
        z = z_ref[bi]
        for h in range(N_HEADS):
            gi = bi * N_HEADS + h
            sl = slice(h * HEAD_DIM, (h + 1) * HEAD_DIM)
            st_ref[bi, h] = jnp.exp(a_last[gi]) * s[gi] + _dot_tn(_mxu(kd[gi]), _mxu(v_new[gi]))
            o_ref[bi, :, sl] = (o[gi] * _silu(z[:, sl])).astype(o_ref.dtype)


def _gdn_call(p, conv_buf, state0, conv_w, a_log, dt_bias, norm_g):
    b, t, _ = p.shape
    c = min(GDN_CHUNK, t)
    nb = max(d for d in (4, 2, 1) if b % d == 0)
    assert t % c == 0 and c >= CONV_K - 1
    pa = jnp.zeros((2, 128), F32).at[0, N_HEADS:2 * N_HEADS].set(a_log).at[1, N_HEADS:2 * N_HEADS].set(dt_bias)
    st = pl.BlockSpec((nb, N_HEADS, HEAD_DIM, HEAD_DIM), lambda b_, i: (b_, 0, 0, 0))
    return pl.pallas_call(
        _gdn_kernel,
        out_shape=(jax.ShapeDtypeStruct((b, t, 256), BF16),
                   jax.ShapeDtypeStruct((b, N_HEADS, HEAD_DIM, HEAD_DIM), F32)),
        grid=(b // nb, t // c),
        in_specs=[pl.BlockSpec((nb, c, 768), lambda b_, i: (b_, i, P_DQKV // 768)),
                  pl.BlockSpec((nb, c, 256), lambda b_, i: (b_, i, P_DZ // 256)),
                  pl.BlockSpec((nb, c, 128), lambda b_, i: (b_, i, P_DBA // 128)),
                  pl.BlockSpec((CONV_K, 768), lambda b_, i: (0, 0)),
                  pl.BlockSpec((nb, CONV_K - 1, 768), lambda b_, i: (b_, 0, 0)),
                  st,
                  pl.BlockSpec((2, 128), lambda b_, i: (0, 0)),
                  pl.BlockSpec((1, HEAD_DIM), lambda b_, i: (0, 0))],
        out_specs=(pl.BlockSpec((nb, c, 256), lambda b_, i: (b_, i, 0)), st),
        scratch_shapes=[pltpu.VMEM((nb, 8 + c, 768), F32)],
        compiler_params=_params(("parallel", "arbitrary")),
        name="gdn",
    )(p, p, p, conv_w, conv_buf, state0, pa, norm_g.reshape(1, HEAD_DIM))


def _outproj_kernel(oa_ref, ob_ref, oc_ref, od_ref, x_ref, g1_ref, sc2_ref, sh2_ref, w_ref,
                    l1g_ref, l1b_ref, rw_ref, rb_ref, x1_ref, u2_ref, wc_ref):
    y = _dot(oa_ref[0], w_ref[0:256, :])
    y = y + _dot(ob_ref[0], w_ref[256:512, :])
    y = y + _dot(oc_ref[0], w_ref[512:768, :])
    y = y + _dot(od_ref[0], w_ref[768:1024, :])
    x1 = _ln(DN_ALPHA * x_ref[0] + g1_ref[0] * y) * l1g_ref[...] + l1b_ref[...]
    x1_ref[0] = x1
    u2 = _ln(x1) * (1.0 + sc2_ref[0]) + sh2_ref[0]
    u2_ref[0] = u2.astype(BF16)
    scores = _sigmoid(_dot_nt(rw_ref[...], u2, HIGHEST))
    biased = scores + rb_ref[...]
    tm = biased.shape[1]
    per = N_EXPERTS // N_GROUPS
    idx = _iota((per, tm), 0).astype(F32)
    grp_rows = []
    for g in range(N_GROUPS):
        vals = biased[g * per:(g + 1) * per, :]
        m1 = jnp.max(vals, 0, keepdims=True)
        first = jnp.min(jnp.where(vals == m1, idx, float(per)), 0, keepdims=True)
        m2 = jnp.max(jnp.where(idx == first, -jnp.inf, vals), 0, keepdims=True)
        grp_rows.append(m1 + m2)
    gs = jnp.concatenate(grp_rows, 0)
    gi = _iota((N_GROUPS, tm), 0)
    rank = jnp.zeros((N_GROUPS, tm), F32)
    for g2 in range(N_GROUPS):
        row = gs[g2:g2 + 1, :]
        rank = rank + jnp.where((row > gs) | ((row == gs) & (g2 < gi)), 1.0, 0.0)
    keep = jnp.where(rank < float(TOPK_GROUPS), 1.0, 0.0)
    emask = jnp.concatenate([jnp.broadcast_to(keep[g:g + 1, :], (per, tm)) for g in range(N_GROUPS)], 0) > 0.5
    sel = _topk_mask(jnp.where(emask, biased, -jnp.inf), TOP_K, 0)
    w = sel * scores
    w = w / jnp.sum(w, 0, keepdims=True) * ROUTED_SCALE
    wc_ref[0] = jnp.concatenate([w, jnp.zeros((LANES - N_EXPERTS, tm), F32)], 0).T


def _outproj_call(oa, ob, oc, od, x, g1, sc2, sh2, w_out, ln_g, ln_b, rw_t, rb):
    g, r, _ = x.shape
    tm = min(512, r)
    o_spec = pl.BlockSpec((1, tm, 256), lambda g_, i: (g_, i, 0))
    x_spec = pl.BlockSpec((1, tm, D_MODEL), lambda g_, i: (g_, i, 0))
    vec = pl.BlockSpec((1, D_MODEL), lambda g_, i: (0, 0))
    return pl.pallas_call(
        _outproj_kernel,
        out_shape=(jax.ShapeDtypeStruct((g, r, D_MODEL), F32),
                   jax.ShapeDtypeStruct((g, r, D_MODEL), BF16),
                   jax.ShapeDtypeStruct((g, r, LANES), F32)),
        grid=(g, r // tm),
        in_specs=[o_spec, o_spec, o_spec, o_spec, x_spec,
                  _mod_spec(g1, tm), _mod_spec(sc2, tm), _mod_spec(sh2, tm),
                  pl.BlockSpec((D_MODEL, D_MODEL), lambda g_, i: (0, 0)),
                  vec, vec,
                  pl.BlockSpec((N_EXPERTS, D_MODEL), lambda g_, i: (0, 0)),
                  pl.BlockSpec((N_EXPERTS, 1), lambda g_, i: (0, 0))],
        out_specs=(x_spec, x_spec, pl.BlockSpec((1, tm, LANES), lambda g_, i: (g_, i, 0))),
        compiler_params=_params(("parallel", "parallel")),
        name="outproj",
    )(oa, ob, oc, od, x, g1, sc2, sh2, w_out, ln_g.reshape(1, -1), ln_b.reshape(1, -1), rw_t, rb.reshape(-1, 1))


EXPERTS_PER_STEP = 4


def _swiglu_act(hid):
    return _silu(hid[:, 0:EXPERT_FF]) * hid[:, EXPERT_FF:2 * EXPERT_FF]


def _moe_kernel(u_ref, wc_ref, x_ref, g2_ref, wgu_ref, wdn_ref, sgu_ref, sdn_ref, l2g_ref, l2b_ref,
                o_ref, acc_ref):
    step = pl.program_id(2)
    eps = wgu_ref.shape[0]
    u = u_ref[0]

    @pl.when(step == 0)
    def _shared():
        acc_ref[...] = _dot(_swiglu_act(_dot(u, sgu_ref[...])).astype(BF16), sdn_ref[...])

    wc = wc_ref[0]
    lane = _iota(wc.shape, 1)
    acts = []
    for k in range(eps):
        col = jnp.sum(jnp.where(lane == step * eps + k, wc, 0.0), -1, keepdims=True)
        acts.append((_swiglu_act(_dot(u, wgu_ref[k])) * col).astype(BF16))
    act = jnp.concatenate(acts, 1)
    acc_ref[...] += _dot(act, wdn_ref[...].reshape(eps * EXPERT_FF, D_MODEL))

    @pl.when(step == pl.num_programs(2) - 1)
    def _finish():
        o_ref[0] = _ln(DN_ALPHA * x_ref[0] + g2_ref[0] * acc_ref[...]) * l2g_ref[...] + l2b_ref[...]


def _moe_call(u2, wc, x1, g2, wgu, wdn, layer, sgu, sdn, ln_g, ln_b):
    g, r, _ = x1.shape
    tm = min(1024, r)
    eps = EXPERTS_PER_STEP
    tok = lambda width: pl.BlockSpec((1, tm, width), lambda g_, i, e: (g_, i, 0))
    if g2.shape[1] == 1:
        g2_spec = pl.BlockSpec((1, 1, D_MODEL), lambda g_, i, e: (g_, 0, 0))
    else:
        g2_spec = tok(D_MODEL)
    vec = pl.BlockSpec((1, D_MODEL), lambda g_, i, e: (0, 0))
    return pl.pallas_call(
        _moe_kernel,
        out_shape=jax.ShapeDtypeStruct((g, r, D_MODEL), F32),
        grid=(g, r // tm, N_EXPERTS // eps),
        in_specs=[tok(D_MODEL), tok(LANES), tok(D_MODEL), g2_spec,
                  pl.BlockSpec((None, eps, D_MODEL, 2 * EXPERT_FF), lambda g_, i, e: (layer, e, 0, 0)),
                  pl.BlockSpec((None, eps, EXPERT_FF, D_MODEL), lambda g_, i, e: (layer, e, 0, 0)),
                  pl.BlockSpec((D_MODEL, 2 * EXPERT_FF), lambda g_, i, e: (0, 0)),
                  pl.BlockSpec((EXPERT_FF, D_MODEL), lambda g_, i, e: (0, 0)),
                  vec, vec],
        out_specs=tok(D_MODEL),
        scratch_shapes=[pltpu.VMEM((tm, D_MODEL), F32)],
        compiler_params=_params(("parallel", "parallel", "arbitrary")),
        name="moe",
    )(u2, wc, x1, g2, wgu, wdn, sgu, sdn, ln_g.reshape(1, -1), ln_b.reshape(1, -1))


def _run_trunk(x, mod, p0, weights, stacked, past):
    b, t, _ = x.shape
    per_token = t < 128
    if per_token:
        grp = lambda a: a.reshape(1, b * t, a.shape[-1])
        mod_rows = lambda m: jnp.repeat(m, t, axis=0)[None]
    else:
        grp = lambda a: a
        mod_rows = lambda m: m[:, None, :]
    ungrp = lambda a: a.reshape(b, t, a.shape[-1])

    outs = []
    for l in range(DEPTH):
        w = {k: v[l] for k, v in weights.items()}
        sh1, sc1, g1, sh2, sc2, g2 = [mod_rows(m) for m in jnp.split(mod[l], 6, axis=-1)]
        p = ungrp(_proj_call(grp(x), sc1, sh1, w["w_in"]))
        if past is None:
            o_a, nsa_rows, win_rows = _nsa_prompt_call(p, w["nsa_pool"])
            win_new = win_rows[:, t - min(NSA_WINDOW, t):]
            o_c, logf = _fox_prompt_call(p, w["fox_f_bias"])
            ret_s0 = jnp.zeros((b, N_HEADS, HEAD_DIM, HEAD_DIM), F32)
            gdn_s0 = ret_s0
            conv_buf = jnp.zeros((b, CONV_K - 1, 3 * GROUP_WIDTH), F32)
        else:
            o_a, nsa_rows, win_new = _nsa_sample_call(p, past["nsa_t"], past["page_table"], l,
                                                      past["win_t"][l], w["nsa_pool"], p0)
            o_c, logf = _fox_sample_call(p, past["fox_kv_t"], past["fox_lf_t"], past["page_table"], l,
                                         w["fox_f_bias"])
            ret_s0, gdn_s0, conv_buf = past["state_ret"][l], past["state_gdn"][l], past["state_gdn_conv"][l]
        o_b, ret_s = _ret_call(p, ret_s0, w["ret_gn_g"], w["ret_gn_b"], p0)
        o_d, gdn_s = _gdn_call(p, conv_buf, gdn_s0, w["gdn_conv_w"], w["gdn_A_log"], w["gdn_dt_bias"],
                               w["gdn_norm_g"])
        x1, u2, wc = _outproj_call(grp(o_a), grp(o_b), grp(o_c), grp(o_d), grp(x), g1, sc2, sh2,
                                   w["w_out"], w["ln1_g"], w["ln1_b"], w["router_w_t"], w["router_b"])
        x = ungrp(_moe_call(u2, wc, x1, g2, stacked["exp_w_gu"], stacked["exp_w_down"], l,
                            w["sh_w_gu"], w["sh_w_down"], w["ln2_g"], w["ln2_b"]))
        qkv = p[:, :, P_DQKV:P_DQKV + 768]
        conv_new = jnp.concatenate([conv_buf, qkv], axis=1)[:, t:]
        outs.append((nsa_rows.reshape(b, t, 4, HEAD_DIM),
                     p[:, :, P_CK:P_CK + 512].reshape(b, t, 2, N_HEADS, HEAD_DIM),
                     logf[:, :, :N_HEADS],
                     win_new.reshape(b, win_new.shape[1], 2, HEAD_DIM),
                     ret_s, gdn_s, conv_new))
    nsa, fkv, flf, win, ret, gdn, conv = zip(*outs)
    return x, (jnp.stack(nsa, 1), jnp.stack(fkv, 1), jnp.stack(flf, 1), jnp.stack(win, 0),
               jnp.stack(ret, 0), jnp.stack(gdn, 0), jnp.stack(conv, 0))


def kernel(x_prompt, x_sample, cache_nsa, cache_fox_kv, cache_fox_logf, state_nsa_win, state_ret, state_gdn, state_gdn_conv, page_table, c_prompt, c_sample, w_mod, b_mod, w_in, w_out, nsa_pool, ret_gn_g, ret_gn_b, fox_f_bias, gdn_conv_w, gdn_A_log, gdn_dt_bias, gdn_norm_g, ln1_g, ln1_b, ln2_g, ln2_b, router_w, router_b, exp_w_gu, exp_w_down, sh_w_gu, sh_w_down):
    b = x_prompt.shape[0]
    db = x_sample.shape[0]
    n_pool, _, page_sz = cache_nsa.shape[:3]
    past_len = page_table.shape[1] * page_sz
    w_in_p = _permute_columns(w_in.astype(BF16), _proj_perm())
    weights = dict(
        w_in=w_in_p, w_out=w_out.astype(BF16), nsa_pool=nsa_pool, ret_gn_g=ret_gn_g, ret_gn_b=ret_gn_b,
        fox_f_bias=fox_f_bias, gdn_conv_w=gdn_conv_w, gdn_A_log=gdn_A_log, gdn_dt_bias=gdn_dt_bias,
        gdn_norm_g=gdn_norm_g, ln1_g=ln1_g, ln1_b=ln1_b, ln2_g=ln2_g, ln2_b=ln2_b,
        router_w_t=jnp.swapaxes(router_w, 1, 2), router_b=router_b,
        sh_w_gu=sh_w_gu.astype(BF16), sh_w_down=sh_w_down.astype(BF16))
    stacked = dict(exp_w_gu=exp_w_gu.astype(BF16), exp_w_down=exp_w_down.astype(BF16))
    n_c = b + db
    n_pad = -n_c % 8
    c_all = jnp.concatenate([c_prompt, c_sample, jnp.zeros((n_pad, D_MODEL), F32)], axis=0)
    mod = _mod_call(c_all, w_mod, b_mod)
    past = dict(
        nsa_t=jnp.transpose(cache_nsa, (0, 1, 3, 4, 2)).reshape(n_pool, DEPTH, 4 * HEAD_DIM, page_sz),
        fox_kv_t=jnp.transpose(cache_fox_kv, (0, 1, 3, 4, 5, 2)).reshape(n_pool, DEPTH, 2 * GROUP_WIDTH, page_sz),
        fox_lf_t=jnp.transpose(cache_fox_logf, (0, 1, 3, 2)),
        win_t=jnp.transpose(state_nsa_win, (0, 1, 3, 4, 2)).reshape(DEPTH, db, 2 * HEAD_DIM, state_nsa_win.shape[2]),
        state_ret=state_ret, state_gdn=state_gdn, state_gdn_conv=state_gdn_conv, page_table=page_table)
    y_p, (nsa_p, fkv_p, flf_p, win_p, ret_p, gdn_p, conv_p) = _run_trunk(x_prompt, mod[:, :b], 0, weights, stacked, None)
    y_s, (nsa_s, fkv_s, flf_s, win_s, ret_s, gdn_s, conv_s) = _run_trunk(x_sample, mod[:, b:n_c], past_len, weights, stacked, past)
    return (y_p, y_s, nsa_p, nsa_s, fkv_p, fkv_s, flf_p, flf_s, win_p, win_s,
            ret_p, ret_s, gdn_p, gdn_s, conv_p, conv_s)
```

```python
import functools
import math

import numpy as np
import jax
import jax.numpy as jnp
from jax import lax
from jax.experimental import pallas as pl
from jax.experimental.pallas import tpu as pltpu

F32 = jnp.float32
BF16 = jnp.bfloat16
HIGHEST = lax.Precision.HIGHEST

D_MODEL = 1024
DEPTH = 4
HEAD_DIM = 64
N_HEADS = 4
GROUP_WIDTH = N_HEADS * HEAD_DIM
NSA_BLOCK = 64
NSA_TOPN = 8
NSA_WINDOW = 512
NSA_FORCE = 1.0e4
ROPE_THETA = 500000.0
ROPE_DIMS = HEAD_DIM // 4
RET_THETA = 10000.0
RET_CHUNK = 128
GDN_CHUNK = 64
CONV_K = 4
N_EXPERTS = 64
TOP_K = 8
N_GROUPS = 8
TOPK_GROUPS = 4
EXPERT_FF = 256
ROUTED_SCALE = 2.5
DN_ALPHA = (2 * DEPTH) ** 0.25
LN_EPS = 1e-5
NORM_EPS = 1e-6
NEG_BIG = -1e30
SCALE = HEAD_DIM ** -0.5
QUERY_BLOCK = 256
LANES = 128
VMEM_LIMIT = 56 * 1024 * 1024

IN_SPLITS = (GROUP_WIDTH, 6 * HEAD_DIM, 3 * N_HEADS,
             GROUP_WIDTH, GROUP_WIDTH, GROUP_WIDTH, GROUP_WIDTH,
             GROUP_WIDTH, GROUP_WIDTH, GROUP_WIDTH, N_HEADS,
             3 * GROUP_WIDTH, N_HEADS, N_HEADS, GROUP_WIDTH)
IN_WIDTH = sum(IN_SPLITS)

P_AQ, P_BQ, P_BK, P_BV, P_BZ = 0, 256, 512, 768, 1024
P_CQ, P_CK, P_CV, P_DZ, P_DQKV = 1280, 1536, 1792, 2048, 2304
P_AKC, P_AKS, P_AKW, P_AG, P_CF, P_DBA = 3072, 3200, 3328, 3456, 3584, 3712
P_WIDTH = 3840


def _proj_perm():
    src = np.cumsum((0,) + IN_SPLITS)
    (q_a, kv_a, g_a, q_b, k_b, v_b, z_b, q_c, k_c, v_c, f_c, qkv_d, beta_d, a_d, z_d) = [int(s) for s in src[:-1]]
    perm = -np.ones((P_WIDTH,), np.int64)

    def put(dst, start, width):
        perm[dst:dst + width] = np.arange(start, start + width)

    put(P_AQ, q_a, 256)
    put(P_AKC, kv_a, 128)
    put(P_AKS, kv_a + 128, 128)
    put(P_AKW, kv_a + 256, 128)
    for h in range(N_HEADS):
        for j in range(3):
            perm[P_AG + j * N_HEADS + h] = g_a + h * 3 + j
    put(P_BQ, q_b, 256)
    put(P_BK, k_b, 256)
    put(P_BV, v_b, 256)
    put(P_BZ, z_b, 256)
    put(P_CQ, q_c, 256)
    put(P_CK, k_c, 256)
    put(P_CV, v_c, 256)
    put(P_CF, f_c, 4)
    put(P_DQKV, qkv_d, 768)
    put(P_DZ, z_d, 256)
    put(P_DBA, beta_d, 4)
    put(P_DBA + 4, a_d, 4)
    return perm


def _permute_columns(w, perm):
    pieces = []
    i, n = 0, len(perm)
    while i < n:
        j = i + 1
        if perm[i] < 0:
            while j < n and perm[j] < 0:
                j += 1
            pieces.append(jnp.zeros(w.shape[:-1] + (j - i,), w.dtype))
        else:
            while j < n and perm[j] == perm[j - 1] + 1:
                j += 1
            pieces.append(w[..., int(perm[i]):int(perm[i]) + (j - i)])
        i = j
    return jnp.concatenate(pieces, -1)


def _rope_tables(pos, n_rot, theta, n_heads, pad_identity=0):
    half = n_rot // 2
    inv = theta ** (-np.arange(half, dtype=np.float64) / half)
    ang = np.asarray(pos, np.float64)[:, None] * inv[None, :]
    t = ang.shape[0]
    c = np.ones((t, HEAD_DIM)); sa = np.zeros((t, HEAD_DIM)); sb = np.zeros((t, HEAD_DIM))
    c[:, :half] = np.cos(ang); c[:, half:n_rot] = np.cos(ang)
    sa[:, :half] = -np.sin(ang)
    sb[:, half:n_rot] = np.sin(ang)
    c = np.tile(c, (1, n_heads)); sa = np.tile(sa, (1, n_heads)); sb = np.tile(sb, (1, n_heads))
    if pad_identity:
        c = np.concatenate([c, np.ones((t, pad_identity))], 1)
        sa = np.concatenate([sa, np.zeros((t, pad_identity))], 1)
        sb = np.concatenate([sb, np.zeros((t, pad_identity))], 1)
    return tuple(jnp.asarray(a, F32) for a in (c, sa, sb))


def _dot(a, b, prec=None):
    return jnp.dot(a, b, preferred_element_type=F32, precision=prec)


def _dot_nt(a, b, prec=None):
    return lax.dot_general(a, b, (((1,), (1,)), ((), ())), preferred_element_type=F32, precision=prec)


def _dot_tn(a, b, prec=None):
    return lax.dot_general(a, b, (((0,), (0,)), ((), ())), preferred_element_type=F32, precision=prec)


def _iota(shape, axis):
    return lax.broadcasted_iota(jnp.int32, shape, axis)


def _ln(x):
    mu = jnp.mean(x, -1, keepdims=True)
    xc = x - mu
    var = jnp.mean(xc * xc, -1, keepdims=True)
    return xc * lax.rsqrt(var + LN_EPS)


def _sigmoid(x):
    return 1.0 / (1.0 + jnp.exp(-x))


def _silu(x):
    return x * _sigmoid(x)


def _softplus(x):
    return jnp.maximum(x, 0.0) + jnp.log1p(jnp.exp(-jnp.abs(x)))


def _log_sigmoid(x):
    return -_softplus(-x)


def _rope(x, c, sa, sb, half):
    w = x.shape[-1]
    return x * c + pltpu.roll(x, w - half, 1) * sa + pltpu.roll(x, half, 1) * sb


def _masked_softmax(s, mask, axis):
    s = jnp.where(mask, s, NEG_BIG)
    e = jnp.where(mask, jnp.exp(s - jnp.max(s, axis, keepdims=True)), 0.0)
    return e / jnp.maximum(jnp.sum(e, axis, keepdims=True), 1e-30)


def _softmax_pv(s, mask, v):
    s = jnp.where(mask, s, NEG_BIG)
    m = jnp.max(s, -1, keepdims=True)
    e = jnp.exp(s - m)
    den = jnp.sum(e, -1, keepdims=True)
    inv = jnp.where(m > 0.5 * NEG_BIG, 1.0 / jnp.maximum(den, 1e-30), 0.0)
    return _dot(e.astype(BF16), v) * inv


def _topk_mask(vals, k, axis):
    n = vals.shape[axis]
    idx = _iota(vals.shape, axis).astype(F32)
    sel = jnp.zeros(vals.shape, F32)
    work = vals
    for _ in range(k):
        m = jnp.max(work, axis, keepdims=True)
        first = jnp.min(jnp.where(work == m, idx, float(n)), axis, keepdims=True)
        pick = idx == first
        sel = jnp.where(pick, 1.0, sel)
        work = jnp.where(pick, -jnp.inf, work)
    return sel


def _pool_weights(pool_ref, n_rep):
    pl_t = pool_ref[...]
    e = jnp.exp(pl_t - jnp.max(pl_t, -1, keepdims=True))
    return e / (jnp.sum(e, -1, keepdims=True) / float(n_rep))


CAUSAL_GROUPS = 4


def _causal_branches(qi, n_qblocks, t, body):
    groups = CAUSAL_GROUPS if n_qblocks % CAUSAL_GROUPS == 0 else 1
    per = n_qblocks // groups
    for r in range(groups):
        pl.when((qi >= r * per) & (qi < (r + 1) * per))(functools.partial(body, (r + 1) * (t // groups)))


def _params(sem):
    return pltpu.CompilerParams(dimension_semantics=sem, vmem_limit_bytes=VMEM_LIMIT)


def _mod_kernel(c_ref, w_ref, b_ref, o_ref):
    o_ref[0] = _dot(c_ref[...].astype(BF16), w_ref[0].astype(BF16)) + b_ref[0]


def _mod_call(c_all, w_mod, b_mod):
    n = c_all.shape[0]
    tn = 1536
    return pl.pallas_call(
        _mod_kernel,
        out_shape=jax.ShapeDtypeStruct((DEPTH, n, 6 * D_MODEL), F32),
        grid=(DEPTH, 6 * D_MODEL // tn),
        in_specs=[pl.BlockSpec((n, D_MODEL), lambda l, j: (0, 0)),
                  pl.BlockSpec((1, D_MODEL, tn), lambda l, j: (l, 0, j)),
                  pl.BlockSpec((1, 1, tn), lambda l, j: (l, 0, j))],
        out_specs=pl.BlockSpec((1, n, tn), lambda l, j: (l, 0, j)),
        compiler_params=_params(("parallel", "parallel")),
        name="mod",
    )(c_all, w_mod, b_mod.reshape(DEPTH, 1, 6 * D_MODEL))


def _proj_kernel(x_ref, sc_ref, sh_ref, w_ref, o_ref):
    u = _ln(x_ref[0]) * (1.0 + sc_ref[0]) + sh_ref[0]
    o_ref[0] = _dot(u.astype(BF16), w_ref[...])


def _mod_spec(m, tm):
    if m.shape[1] == 1:
        return pl.BlockSpec((1, 1, D_MODEL), lambda g, i: (g, 0, 0))
    return pl.BlockSpec((1, tm, D_MODEL), lambda g, i: (g, i, 0))


def _proj_call(x, sc, sh, w):
    g, r, _ = x.shape
    tm = min(512, r)
    return pl.pallas_call(
        _proj_kernel,
        out_shape=jax.ShapeDtypeStruct((g, r, P_WIDTH), F32),
        grid=(g, r // tm),
        in_specs=[pl.BlockSpec((1, tm, D_MODEL), lambda g_, i: (g_, i, 0)),
                  _mod_spec(sc, tm), _mod_spec(sh, tm),
                  pl.BlockSpec((D_MODEL, P_WIDTH), lambda g_, i: (0, 0))],
        out_specs=pl.BlockSpec((1, tm, P_WIDTH), lambda g_, i: (g_, i, 0)),
        compiler_params=_params(("parallel", "parallel")),
        name="proj",
    )(x, sc, sh, w)


def _nsa_prompt_kernel(q_ref, g_ref, kc_ref, ks_ref, kw_ref, pool_ref,
                       qc_ref, qa_ref, qb_ref, kc_t_ref, ka_t_ref, kb_t_ref,
                       oa_ref, rows_ref, win_ref,
                       comp_ref, ksb_ref, vsb_ref, kwp_ref, vwp_ref, osel_ref):
    qi = pl.program_id(1)
    t = kc_ref.shape[1]
    nb = t // NSA_BLOCK
    qb = q_ref.shape[1]
    wnd = NSA_WINDOW

    @pl.when(qi == 0)
    def _prep():
        kcvc = kc_ref[0]
        ks_rot = _rope(ks_ref[0], kc_t_ref[...], ka_t_ref[...], kb_t_ref[...], ROPE_DIMS // 2)
        kw_rot = _rope(kw_ref[0], kc_t_ref[...], ka_t_ref[...], kb_t_ref[...], ROPE_DIMS // 2)
        rows_ref[0, :, 0:128] = kcvc
        rows_ref[0, :, 128:256] = ks_rot
        win_ref[0] = kw_rot
        ksb_ref[...] = ks_rot[:, 0:64].astype(BF16)
        vsb_ref[...] = ks_rot[:, 64:128].astype(BF16)
        kwp_ref[0:wnd, :] = jnp.zeros((wnd, HEAD_DIM), BF16)
        vwp_ref[0:wnd, :] = jnp.zeros((wnd, HEAD_DIM), BF16)
        kwp_ref[wnd:wnd + t, :] = kw_rot[:, 0:64].astype(BF16)
        vwp_ref[wnd:wnd + t, :] = kw_rot[:, 64:128].astype(BF16)
        wts = _pool_weights(pool_ref, nb)
        same = (_iota((nb, t), 1) >> 6) == _iota((nb, t), 0)
        pk = jnp.where(same, wts[0:1, :], 0.0)
        pv = jnp.where(same, wts[1:2, :], 0.0)
        ck = _dot(pk, kcvc, HIGHEST)
        cv = _dot(pv, kcvc, HIGHEST)
        comp_ref[...] = jnp.where(_iota((nb, 128), 1) < HEAD_DIM, ck, cv)

    s0 = pl.multiple_of(qi * qb, qb)
    q = q_ref[0] * SCALE
    qr = _rope(q, qc_ref[...], qa_ref[...], qb_ref[...], ROPE_DIMS // 2)
    gates = _sigmoid(g_ref[0])
    comp = comp_ref[...]
    compk = comp[:, 0:HEAD_DIM]
    compv = comp[:, HEAD_DIM:128]
    qp = s0 + _iota((qb, 1), 0)
    qp_row = s0 + _iota((1, qb), 1)
    blk = _iota((nb, 1), 0)
    cmask = blk < ((qp_row + 1) >> 6)
    imp = jnp.zeros((nb, qb), F32)
    o_cmp = []
    for h in range(N_HEADS):
        qh = q[:, h * HEAD_DIM:(h + 1) * HEAD_DIM]
        pc = _masked_softmax(_dot_nt(compk, qh, HIGHEST), cmask, 0)
        imp = imp + pc
        o_cmp.append(_dot_tn(pc, compv))
    cur = qp_row >> 6
    imp = jnp.where((blk == cur) | (blk == 0), NSA_FORCE, imp)
    imp = jnp.where(blk <= cur, imp, -1.0)
    sel = _topk_mask(imp, min(NSA_TOPN, nb), 0)
    qr_heads = [qr[:, h * HEAD_DIM:(h + 1) * HEAD_DIM].astype(BF16) for h in range(N_HEADS)]

    def _selected(ext):
        expand = ((_iota((nb, ext), 1) >> 6) == _iota((nb, ext), 0)).astype(F32)
        selk = _dot_tn(sel, expand)
        smask = (selk > 0.5) & (_iota((1, ext), 1) <= qp)
        ksb = ksb_ref[0:ext, :]
        vsb = vsb_ref[0:ext, :]
        for h in range(N_HEADS):
            osel_ref[:, h * HEAD_DIM:(h + 1) * HEAD_DIM] = _softmax_pv(_dot_nt(qr_heads[h], ksb), smask, vsb)

    _causal_branches(qi, t // qb, t, _selected)
    kw = kwp_ref[pl.ds(s0, wnd + qb), :]
    vw = vwp_ref[pl.ds(s0, wnd + qb), :]
    wpos = s0 - wnd + _iota((1, wnd + qb), 1)
    wmask = (wpos >= 0) & (wpos <= qp) & (wpos > qp - wnd)
    for h in range(N_HEADS):
        o_win = _softmax_pv(_dot_nt(qr_heads[h], kw), wmask, vw)
        o_sel = osel_ref[:, h * HEAD_DIM:(h + 1) * HEAD_DIM]
        out = (gates[:, h:h + 1] * o_cmp[h] + gates[:, N_HEADS + h:N_HEADS + h + 1] * o_sel
               + gates[:, 2 * N_HEADS + h:2 * N_HEADS + h + 1] * o_win)
        oa_ref[0, :, h * HEAD_DIM:(h + 1) * HEAD_DIM] = out.astype(oa_ref.dtype)


def _nsa_prompt_call(p, pool_l):
    b, t, _ = p.shape
    qb = QUERY_BLOCK
    nb = t // NSA_BLOCK
    pos = np.arange(t)
    q_tabs = _rope_tables(pos, ROPE_DIMS, ROPE_THETA, N_HEADS)
    k_tabs = _rope_tables(pos, ROPE_DIMS, ROPE_THETA, 1, pad_identity=HEAD_DIM)
    pool_t = jnp.tile(pool_l, (1, nb))
    full = lambda col: pl.BlockSpec((1, t, 128), lambda b_, i: (b_, 0, col))
    qtab = pl.BlockSpec((qb, 256), lambda b_, i: (i, 0))
    ktab = pl.BlockSpec((t, 128), lambda b_, i: (0, 0))
    return pl.pallas_call(
        _nsa_prompt_kernel,
        out_shape=(jax.ShapeDtypeStruct((b, t, 256), BF16),
                   jax.ShapeDtypeStruct((b, t, 256), F32),
                   jax.ShapeDtypeStruct((b, t, 128), F32)),
        grid=(b, t // qb),
        in_specs=[pl.BlockSpec((1, qb, 256), lambda b_, i: (b_, i, P_AQ // 256)),
                  pl.BlockSpec((1, qb, 128), lambda b_, i: (b_, i, P_AG // 128)),
                  full(P_AKC // 128), full(P_AKS // 128), full(P_AKW // 128),
                  pl.BlockSpec((2, t), lambda b_, i: (0, 0)),
                  qtab, qtab, qtab, ktab, ktab, ktab],
        out_specs=(pl.BlockSpec((1, qb, 256), lambda b_, i: (b_, i, 0)),
                   pl.BlockSpec((1, t, 256), lambda b_, i: (b_, 0, 0)),
                   pl.BlockSpec((1, t, 128), lambda b_, i: (b_, 0, 0))),
        scratch_shapes=[pltpu.VMEM((nb, 128), F32),
                        pltpu.VMEM((t, HEAD_DIM), BF16), pltpu.VMEM((t, HEAD_DIM), BF16),
                        pltpu.VMEM((NSA_WINDOW + t, HEAD_DIM), BF16), pltpu.VMEM((NSA_WINDOW + t, HEAD_DIM), BF16),
                        pltpu.VMEM((qb, GROUP_WIDTH), F32)],
        compiler_params=_params(("parallel", "arbitrary")),
        name="nsa_prompt",
    )(p, p, p, p, p, pool_t, *q_tabs, *k_tabs)


def _softmax2(s1, mask1, s2, mask2):
    s1 = jnp.where(mask1, s1, NEG_BIG)
    s2 = jnp.where(mask2, s2, NEG_BIG)
    m = jnp.maximum(jnp.max(s1, -1, keepdims=True), jnp.max(s2, -1, keepdims=True))
    e1 = jnp.where(mask1, jnp.exp(s1 - m), 0.0)
    e2 = jnp.where(mask2, jnp.exp(s2 - m), 0.0)
    den = jnp.maximum(jnp.sum(e1, -1, keepdims=True) + jnp.sum(e2, -1, keepdims=True), 1e-30)
    return e1, e2, den


def _nsa_sample_kernel(pt_ref, *refs, past, pps):
    page_refs = refs[:pps]
    (q_ref, g_ref, kc_ref, ks_ref, kw_ref, hist_ref, pool_ref,
     qc_ref, qa_ref, qb_ref, kc_t_ref, ka_t_ref, kb_t_ref,
     oa_ref, rows_ref, win_ref, comp_ref, kst_ref, vst_ref) = refs[pps:]
    i = pl.program_id(1)
    t = q_ref.shape[1]
    page_sz = page_refs[0].shape[3]
    n_pages = past // page_sz
    cr = comp_ref.shape[0]
    wts = _pool_weights(pool_ref, page_sz // NSA_BLOCK)
    r16 = _iota((16, page_sz), 0)
    half16 = _iota((16, page_sz), 1) >> 6
    pkv = jnp.where((r16 == half16), wts[0:1, :], 0.0) + jnp.where((r16 - 8 == half16), wts[1:2, :], 0.0)
    lane_lo = _iota((8, 128), 1) < HEAD_DIM

    @pl.when(i == 0)
    def _init():
        comp_ref[cr - 16:cr, :] = jnp.zeros((16, 128), F32)

    pkv_hi = pkv.astype(BF16)
    pkv_split = jnp.concatenate([pkv_hi, (pkv - pkv_hi.astype(F32)).astype(BF16)], 0)
    for j in range(pps):
        page_t = page_refs[j][0, 0]
        x = page_t[0:128, :]
        x_hi = x.astype(BF16)
        x_lo = (x - x_hi.astype(F32)).astype(BF16)
        r_hi = _dot_nt(pkv_split, x_hi)
        res = r_hi[0:16] + r_hi[16:32] + _dot_nt(pkv_hi, x_lo)
        pg = i * pps + j
        comp_ref[pl.ds(pl.multiple_of(pg * 8, 8), 8), :] = jnp.where(lane_lo, res[0:8], res[8:16])
        col0 = pl.multiple_of(pg * page_sz, page_sz)
        kst_ref[:, pl.ds(col0, page_sz)] = page_t[128:192, :].astype(BF16)
        vst_ref[:, pl.ds(col0, page_sz)] = page_t[192:256, :].astype(BF16)

    @pl.when(i == pl.num_programs(1) - 1)
    def _finish():
        kcvc = kc_ref[0]
        ks_rot = _rope(ks_ref[0], kc_t_ref[...], ka_t_ref[...], kb_t_ref[...], ROPE_DIMS // 2)
        kw_rot = _rope(kw_ref[0], kc_t_ref[...], ka_t_ref[...], kb_t_ref[...], ROPE_DIMS // 2)
        rows_ref[0, :, 0:128] = kcvc
        rows_ref[0, :, 128:256] = ks_rot
        res_n = _dot(pkv[:, 0:t], kcvc, HIGHEST)
        comp_ref[cr - 16:cr - 8, :] = jnp.where(lane_lo, res_n[0:8], res_n[8:16])

        q = q_ref[0]
        qr = _rope(q, qc_ref[...], qa_ref[...], qb_ref[...], ROPE_DIMS // 2)
        zpad = jnp.zeros((32 - t, HEAD_DIM), F32)
        stack = lambda x: jnp.concatenate(
            [piece for h in range(N_HEADS) for piece in (x[:, h * HEAD_DIM:(h + 1) * HEAD_DIM], zpad)], 0)
        q_all = stack(q)
        qr_all = stack(qr)
        qidx = _iota((128, 1), 0) & 31
        qp = past + qidx

        comp = comp_ref[...]
        compk = comp[:, 0:HEAD_DIM]
        compv = comp[:, HEAD_DIM:128]
        ci = _iota((1, cr), 1)
        blk = 2 * (ci >> 3) + (ci & 7)
        valid = ((ci & 7) < 2) & (blk * NSA_BLOCK < past + t)
        cmask = valid & (blk < ((qp + 1) >> 6))
        pc = _masked_softmax(_dot_nt(q_all, compk, HIGHEST) * SCALE, cmask, -1)
        o_cmp = _dot(pc, compv)
        imp = pc[0:32] + pc[32:64] + pc[64:96] + pc[96:128]
        cur = qp[0:32] >> 6
        imp = jnp.where((blk == cur) | (blk == 0), NSA_FORCE, imp)
        imp = jnp.where(blk <= cur, imp, -1.0)
        imp = jnp.where(valid, imp, -2.0)
        sel32 = _topk_mask(imp, NSA_TOPN, 1)
        sel = jnp.concatenate([sel32] * N_HEADS, 0)

        lane_half = _iota((128, page_sz), 1) < NSA_BLOCK
        selk = jnp.concatenate(
            [jnp.where(lane_half, sel[:, 8 * pg:8 * pg + 1], sel[:, 8 * pg + 1:8 * pg + 2]) for pg in range(n_pages)], 1)
        smask = (selk > 0.5) & (_iota((1, past), 1) <= qp)
        new_idx = _iota((1, t), 1)
        nmask = (sel[:, cr - 16:cr - 15] > 0.5) & (new_idx <= qidx)
        s_past = _dot(qr_all.astype(BF16), kst_ref[...]) * SCALE
        s_new = _dot_nt(qr_all, ks_rot[:, 0:HEAD_DIM]) * SCALE
        e1, e2, den = _softmax2(s_past, smask, s_new, nmask)
        o_sel = (_dot_nt(e1.astype(BF16), vst_ref[...]) + _dot(e2, ks_rot[:, HEAD_DIM:128])) / den

        hist_t = hist_ref[0]
        wb = hist_t.shape[1]
        wpos = past - wb + _iota((1, wb), 1)
        hmask = (wpos >= 0) & (wpos <= qp) & (wpos > qp - NSA_WINDOW)
        wmask = (new_idx <= qidx) & (past + new_idx > qp - NSA_WINDOW)
        s_hist = _dot(qr_all, hist_t[0:HEAD_DIM, :]) * SCALE
        s_wnew = _dot_nt(qr_all, kw_rot[:, 0:HEAD_DIM]) * SCALE
        e1, e2, den = _softmax2(s_hist, hmask, s_wnew, wmask)
        o_win = (_dot_nt(e1, hist_t[HEAD_DIM:128, :]) + _dot(e2, kw_rot[:, HEAD_DIM:128])) / den
        hist_tok = hist_t.T
        win_ref[0, 0:wb - t, :] = hist_tok[t:wb, :]
        win_ref[0, wb - t:wb, :] = kw_rot

        gates = _sigmoid(g_ref[0])
        for h in range(N_HEADS):
            r = slice(h * 32, h * 32 + t)
            out = (gates[:, h:h + 1] * o_cmp[r] + gates[:, N_HEADS + h:N_HEADS + h + 1] * o_sel[r]
                   + gates[:, 2 * N_HEADS + h:2 * N_HEADS + h + 1] * o_win[r])
            oa_ref[0, :, h * HEAD_DIM:(h + 1) * HEAD_DIM] = out.astype(oa_ref.dtype)


def _pages_per_step(n_pages):
    return max(d for d in (8, 4, 2, 1) if n_pages % d == 0)


def _nsa_sample_call(p, cache_t, page_table, layer, hist_t, pool_l, past):
    b, t, _ = p.shape
    n_pages = page_table.shape[1]
    page_sz = cache_t.shape[3]
    wb = hist_t.shape[2]
    assert page_sz == 128 and t == 8 and past == n_pages * page_sz and wb == NSA_WINDOW and past >= wb
    pps = _pages_per_step(n_pages)
    pos = past + np.arange(t)
    q_tabs = _rope_tables(pos, ROPE_DIMS, ROPE_THETA, N_HEADS)
    k_tabs = _rope_tables(pos, ROPE_DIMS, ROPE_THETA, 1, pad_identity=HEAD_DIM)
    pool_t = jnp.tile(pool_l, (1, page_sz // NSA_BLOCK))
    new = lambda col: pl.BlockSpec((1, t, 128), lambda b_, i, pt: (b_, 0, col))
    const = lambda shape: pl.BlockSpec(shape, lambda b_, i, pt: (0,) * len(shape))
    page_spec = lambda j: pl.BlockSpec((1, 1, 256, page_sz), lambda b_, i, pt: (pt[b_, i * pps + j], layer, 0, 0))
    grid_spec = pltpu.PrefetchScalarGridSpec(
        num_scalar_prefetch=1,
        grid=(b, n_pages // pps),
        in_specs=[page_spec(j) for j in range(pps)] + [
            pl.BlockSpec((1, t, 256), lambda b_, i, pt: (b_, 0, P_AQ // 256)),
            new(P_AG // 128), new(P_AKC // 128), new(P_AKS // 128), new(P_AKW // 128),
            pl.BlockSpec((1, 128, wb), lambda b_, i, pt: (b_, 0, 0)),
            const((2, page_sz)),
            const((t, 256)), const((t, 256)), const((t, 256)),
            const((t, 128)), const((t, 128)), const((t, 128))],
        out_specs=(pl.BlockSpec((1, t, 256), lambda b_, i, pt: (b_, 0, 0)),
                   pl.BlockSpec((1, t, 256), lambda b_, i, pt: (b_, 0, 0)),
                   pl.BlockSpec((1, wb, 128), lambda b_, i, pt: (b_, 0, 0))),
        scratch_shapes=[pltpu.VMEM((8 * n_pages + 16, 128), F32),
                        pltpu.VMEM((HEAD_DIM, past), BF16),
                        pltpu.VMEM((HEAD_DIM, past), BF16)])
    return pl.pallas_call(
        functools.partial(_nsa_sample_kernel, past=past, pps=pps),
        out_shape=(jax.ShapeDtypeStruct((b, t, 256), BF16),
                   jax.ShapeDtypeStruct((b, t, 256), F32),
                   jax.ShapeDtypeStruct((b, wb, 128), F32)),
        grid_spec=grid_spec,
        compiler_params=_params(("parallel", "arbitrary")),
        name="nsa_sample",
    )(page_table, *([cache_t] * pps), p, p, p, p, p, hist_t, pool_t, *q_tabs, *k_tabs)


def _ret_kernel(q_ref, k_ref, v_ref, z_ref, s0_ref, c_ref, sa_ref, sb_ref, gng_ref, gnb_ref,
                o_ref, st_ref):
    ci = pl.program_id(1)
    nb, c = q_ref.shape[0], q_ref.shape[1]

    @pl.when(ci == 0)
    def _init():
        st_ref[...] = s0_ref[...]

    tabs = (c_ref[...], sa_ref[...], sb_ref[...])
    qs, ks, vs = [], [], []
    for bi in range(nb):
        q = _rope(q_ref[bi], *tabs, HEAD_DIM // 2)
        k = _rope(k_ref[bi], *tabs, HEAD_DIM // 2) * SCALE
        v = v_ref[bi]
        for h in range(N_HEADS):
            sl = slice(h * HEAD_DIM, (h + 1) * HEAD_DIM)
            qs.append(q[:, sl])
            ks.append(k[:, sl])
            vs.append(v[:, sl])
    q3 = jnp.stack(qs, 0)
    k3 = jnp.stack(ks, 0)
    v3 = jnp.stack(vs, 0)
    lgs = [math.log1p(-2.0 ** (-5.0 - h)) for h in range(N_HEADS)]
    ii = _iota((c, c), 0)
    jj = _iota((c, c), 1)
    diff = (ii - jj).astype(F32)
    rowi = _iota((c, 1), 0).astype(F32)
    dec = jnp.stack([jnp.where(jj <= ii, jnp.exp(jnp.minimum(diff * lg, 0.0)), 0.0) for lg in lgs] * nb, 0)
    ea = jnp.stack([jnp.exp((rowi + 1.0) * lg) for lg in lgs] * nb, 0)
    eb = jnp.stack([jnp.exp((c - 1.0 - rowi) * lg) for lg in lgs] * nb, 0)
    s = st_ref[...].reshape(nb * N_HEADS, HEAD_DIM, HEAD_DIM)
    att = _bdot_nt(q3, k3) * dec
    o = _bdot(att, v3) + _bdot(q3 * ea, s)
    kd = k3 * eb
    mu = jnp.mean(o, -1, keepdims=True)
    oc = o - mu
    var = jnp.mean(oc * oc, -1, keepdims=True)
    on = oc * lax.rsqrt(var + NORM_EPS)
    for bi in range(nb):
        z = z_ref[bi]
        for h in range(N_HEADS):
            gi = bi * N_HEADS + h
            sl = slice(h * HEAD_DIM, (h + 1) * HEAD_DIM)
            st_ref[bi, h] = math.exp(c * lgs[h]) * s[gi] + _dot_tn(_mxu(kd[gi]), _mxu(v3[gi]))
            o_ref[bi, :, sl] = ((on[gi] * gng_ref[:, sl] + gnb_ref[:, sl]) * _silu(z[:, sl])).astype(o_ref.dtype)


def _ret_call(p, state0, gn_g, gn_b, p0):
    b, t, _ = p.shape
    c = min(RET_CHUNK, t)
    assert t % c == 0
    tabs = _rope_tables(p0 + np.arange(t), HEAD_DIM, RET_THETA, N_HEADS)
    nb = 2 if b % 2 == 0 else 1
    blk = lambda col: pl.BlockSpec((nb, c, 256), lambda b_, i: (b_, i, col))
    tab = pl.BlockSpec((c, 256), lambda b_, i: (i, 0))
    st = pl.BlockSpec((nb, N_HEADS, HEAD_DIM, HEAD_DIM), lambda b_, i: (b_, 0, 0, 0))
    vec = pl.BlockSpec((1, 256), lambda b_, i: (0, 0))
    return pl.pallas_call(
        _ret_kernel,
        out_shape=(jax.ShapeDtypeStruct((b, t, 256), BF16),
                   jax.ShapeDtypeStruct((b, N_HEADS, HEAD_DIM, HEAD_DIM), F32)),
        grid=(b // nb, t // c),
        in_specs=[blk(P_BQ // 256), blk(P_BK // 256), blk(P_BV // 256), blk(P_BZ // 256), st,
                  tab, tab, tab, vec, vec],
        out_specs=(pl.BlockSpec((nb, c, 256), lambda b_, i: (b_, i, 0)), st),
        compiler_params=_params(("parallel", "arbitrary")),
        name="ret",
    )(p, p, p, p, state0, *tabs, gn_g.reshape(1, 256), gn_b.reshape(1, 256))


def _fox_prompt_kernel(q_ref, k_ref, v_ref, f_ref, fb_ref, o_ref, lf_ref, cum_ref, cumt_ref, kb_ref, vb_ref):
    qi = pl.program_id(1)
    t = k_ref.shape[1]
    qb = q_ref.shape[1]

    @pl.when(qi == 0)
    def _prep():
        lf = _log_sigmoid(f_ref[0] + fb_ref[...])
        lf_ref[0] = lf
        tri = (_iota((qb, qb), 1) <= _iota((qb, qb), 0)).astype(F32)
        carry = jnp.zeros((1, 128), F32)
        for c in range(t // qb):
            blk = _dot(tri, lf[c * qb:(c + 1) * qb], HIGHEST) + carry
            cum_ref[c * qb:(c + 1) * qb, :] = blk
            carry = blk[qb - 1:qb, :]
        cumt_ref[...] = cum_ref[...].T
        for h in range(N_HEADS):
            kb_ref[h] = k_ref[0, :, h * HEAD_DIM:(h + 1) * HEAD_DIM].astype(BF16)
            vb_ref[h] = v_ref[0, :, h * HEAD_DIM:(h + 1) * HEAD_DIM].astype(BF16)

    s0 = pl.multiple_of(qi * qb, qb)
    q = (q_ref[0] * SCALE).astype(BF16)
    cq = cum_ref[pl.ds(s0, qb), :]
    qrow = s0 + _iota((qb, 1), 0)

    def _attend(ext):
        mask = _iota((1, ext), 1) <= qrow
        for h in range(N_HEADS):
            qh = q[:, h * HEAD_DIM:(h + 1) * HEAD_DIM]
            s = _dot_nt(qh, kb_ref[h, 0:ext, :]) + (cq[:, h:h + 1] - cumt_ref[h:h + 1, 0:ext])
            o = _softmax_pv(s, mask, vb_ref[h, 0:ext, :])
            o_ref[0, :, h * HEAD_DIM:(h + 1) * HEAD_DIM] = o.astype(o_ref.dtype)

    _causal_branches(qi, t // qb, t, _attend)


def _fox_prompt_call(p, f_bias):
    b, t, _ = p.shape
    qb = QUERY_BLOCK
    fb = jnp.zeros((1, 128), F32).at[0, :N_HEADS].set(f_bias)
    return pl.pallas_call(
        _fox_prompt_kernel,
        out_shape=(jax.ShapeDtypeStruct((b, t, 256), BF16),
                   jax.ShapeDtypeStruct((b, t, 128), F32)),
        grid=(b, t // qb),
        in_specs=[pl.BlockSpec((1, qb, 256), lambda b_, i: (b_, i, P_CQ // 256)),
                  pl.BlockSpec((1, t, 256), lambda b_, i: (b_, 0, P_CK // 256)),
                  pl.BlockSpec((1, t, 256), lambda b_, i: (b_, 0, P_CV // 256)),
                  pl.BlockSpec((1, t, 128), lambda b_, i: (b_, 0, P_CF // 128)),
                  pl.BlockSpec((1, 128), lambda b_, i: (0, 0))],
        out_specs=(pl.BlockSpec((1, qb, 256), lambda b_, i: (b_, i, 0)),
                   pl.BlockSpec((1, t, 128), lambda b_, i: (b_, 0, 0))),
        scratch_shapes=[pltpu.VMEM((t, 128), F32), pltpu.VMEM((128, t), F32),
                        pltpu.VMEM((N_HEADS, t, HEAD_DIM), BF16), pltpu.VMEM((N_HEADS, t, HEAD_DIM), BF16)],
        compiler_params=_params(("parallel", "arbitrary")),
        name="fox_prompt",
    )(p, p, p, p, fb)


def _rows_per_head(x, rows):
    return jnp.concatenate([jnp.broadcast_to(x[h:h + 1, :], (rows, x.shape[1])) for h in range(N_HEADS)], 0)


def _fox_sample_kernel(pt_ref, *refs, pps, nbr):
    kv_refs = refs[:nbr * pps]
    lf_refs = refs[nbr * pps:2 * nbr * pps]
    (q_ref, k_ref, v_ref, f_ref, fb_ref, o_ref, lf_ref,
     qbd_ref, m_ref, l_ref, acc_ref, carry_ref, cnew_ref) = refs[2 * nbr * pps:]
    i = pl.program_id(1)
    t = q_ref.shape[1]
    page_sz = kv_refs[0].shape[3]
    qidx = _iota((128, 1), 0) & 31

    @pl.when(i == 0)
    def _init():
        for r in range(nbr):
            q = q_ref[r]
            col_head = _iota((t, 256), 1) >> 6
            zpad = jnp.zeros((32 - t, 256), F32)
            qbd = jnp.concatenate(
                [piece for h in range(N_HEADS) for piece in (jnp.where(col_head == h, q, 0.0), zpad)], 0)
            qbd_ref[r] = qbd * SCALE
            lf = _log_sigmoid(f_ref[r] + fb_ref[...])
            lf_ref[r] = lf
            tri = (_iota((t, t), 1) <= _iota((t, t), 0)).astype(F32)
            cs = _dot(tri, lf, HIGHEST)
            zcol = jnp.zeros((32 - t, 1), F32)
            cnew = jnp.concatenate([piece for h in range(N_HEADS) for piece in (cs[:, h:h + 1], zcol)], 0)
            cnew_ref[r] = jnp.broadcast_to(cnew, (128, 128))
            eye = _iota((t, t), 0) == _iota((t, t), 1)
            cs_rows = jnp.concatenate(
                [jnp.broadcast_to(jnp.sum(jnp.where(eye, cs[:, h:h + 1], 0.0), 0, keepdims=True), (32, t))
                 for h in range(N_HEADS)], 0)
            s = _dot_nt(qbd * SCALE, k_ref[r]) + cnew - cs_rows
            mask = _iota((1, t), 1) <= qidx
            s = jnp.where(mask, s, NEG_BIG)
            m = jnp.max(s, -1, keepdims=True)
            e = jnp.where(mask, jnp.exp(s - m), 0.0)
            m_ref[r] = jnp.broadcast_to(m, (128, 128))
            l_ref[r] = jnp.broadcast_to(jnp.sum(e, -1, keepdims=True), (128, 128))
            acc_ref[r] = _dot(e, v_ref[r])
            carry_ref[r] = jnp.zeros((8, 128), F32)

    lane = _iota((N_HEADS, page_sz), 1)
    for r in range(nbr):
        carry = carry_ref[r, 0:N_HEADS, :]
        qbd_bf = qbd_ref[r].astype(BF16)
        cnew = cnew_ref[r, :, 0:1]
        tiles = []
        for j in range(pps):
            lf_t = lf_refs[r * pps + j][0, 0]
            incl = lf_t
            d = 1
            while d < page_sz:
                incl = incl + jnp.where(lane < page_sz - d, pltpu.roll(incl, page_sz - d, 1), 0.0)
                d *= 2
            bias = _rows_per_head(incl - lf_t + carry, 32) + cnew
            carry = carry + incl[:, 0:1]
            tiles.append(_dot(qbd_bf, kv_refs[r * pps + j][0, 0, 0:256, :].astype(BF16)) + bias)
        carry_ref[r, 0:N_HEADS, :] = carry
        s = jnp.concatenate(tiles, 1)
        m_old = m_ref[r, :, 0:1]
        m_new = jnp.maximum(m_old, jnp.max(s, -1, keepdims=True))
        alpha = jnp.exp(m_old - m_new)
        e = jnp.exp(s - m_new)
        m_ref[r] = jnp.broadcast_to(m_new, (128, 128))
        l_ref[r] = alpha * l_ref[r] + jnp.sum(e, -1, keepdims=True)
        acc = alpha * acc_ref[r]
        for j in range(pps):
            acc = acc + _dot_nt(e[:, j * page_sz:(j + 1) * page_sz].astype(BF16),
                                kv_refs[r * pps + j][0, 0, 256:512, :].astype(BF16))
        acc_ref[r] = acc

    @pl.when(i == pl.num_programs(1) - 1)
    def _finish():
        for r in range(nbr):
            o = acc_ref[r] / jnp.maximum(l_ref[r, :, 0:1], 1e-30)
            for h in range(N_HEADS):
                sl = slice(h * HEAD_DIM, (h + 1) * HEAD_DIM)
                o_ref[r, :, sl] = o[h * 32:h * 32 + t, sl].astype(o_ref.dtype)


def _fox_sample_call(p, kv_t, lf_t, page_table, layer, f_bias):
    b, t, _ = p.shape
    n_pages = page_table.shape[1]
    page_sz = kv_t.shape[3]
    assert t == 8 and page_sz == 128
    pps = _pages_per_step(n_pages)
    nbr = 2 if b % 2 == 0 else 1
    fb = jnp.zeros((1, 128), F32).at[0, :N_HEADS].set(f_bias)
    rev = lambda r, j: (lambda b_, i, pt: (pt[b_ * nbr + r, n_pages - 1 - (i * pps + j)], layer, 0, 0))
    slots = [(r, j) for r in range(nbr) for j in range(pps)]
    new = lambda width, col: pl.BlockSpec((nbr, t, width), lambda b_, i, pt: (b_, 0, col))
    grid_spec = pltpu.PrefetchScalarGridSpec(
        num_scalar_prefetch=1,
        grid=(b // nbr, n_pages // pps),
        in_specs=[pl.BlockSpec((1, 1, 512, page_sz), rev(r, j)) for r, j in slots]
        + [pl.BlockSpec((1, 1, N_HEADS, page_sz), rev(r, j)) for r, j in slots]
        + [new(256, P_CQ // 256), new(256, P_CK // 256), new(256, P_CV // 256), new(128, P_CF // 128),
           pl.BlockSpec((1, 128), lambda b_, i, pt: (0, 0))],
        out_specs=(pl.BlockSpec((nbr, t, 256), lambda b_, i, pt: (b_, 0, 0)),
                   pl.BlockSpec((nbr, t, 128), lambda b_, i, pt: (b_, 0, 0))),
        scratch_shapes=[pltpu.VMEM((nbr, 128, 256), F32), pltpu.VMEM((nbr, 128, 128), F32),
                        pltpu.VMEM((nbr, 128, 128), F32), pltpu.VMEM((nbr, 128, 256), F32),
                        pltpu.VMEM((nbr, 8, 128), F32), pltpu.VMEM((nbr, 128, 128), F32)])
    return pl.pallas_call(
        functools.partial(_fox_sample_kernel, pps=pps, nbr=nbr),
        out_shape=(jax.ShapeDtypeStruct((b, t, 256), BF16),
                   jax.ShapeDtypeStruct((b, t, 128), F32)),
        grid_spec=grid_spec,
        compiler_params=_params(("parallel", "arbitrary")),
        name="fox_sample",
    )(page_table, *([kv_t] * (nbr * pps)), *([lf_t] * (nbr * pps)), p, p, p, p, fb)


def _mxu(x):
    return x.astype(BF16) if x.shape[-2] % 16 == 0 else x


def _bdot(a, b):
    return lax.dot_general(_mxu(a), _mxu(b), (((2,), (1,)), ((0,), (0,))), preferred_element_type=F32)


def _bdot_nt(a, b):
    return lax.dot_general(_mxu(a), _mxu(b), (((2,), (2,)), ((0,), (0,))), preferred_element_type=F32)


def _same_block(ii, jj, size):
    shift = size.bit_length() - 1
    return (ii >> shift) == (jj >> shift)


def _gdn_kernel(qkv_ref, z_ref, ba_ref, cw_ref, cb_ref, s0_ref, pa_ref, ng_ref, o_ref, st_ref, xb_ref):
    ci = pl.program_id(1)
    nb, c = qkv_ref.shape[0], qkv_ref.shape[1]
    pad = 8

    @pl.when(ci == 0)
    def _init():
        st_ref[...] = s0_ref[...]
        xb_ref[:, pad - (CONV_K - 1):pad, :] = cb_ref[...]

    ii = _iota((1, c, c), 1)
    jj = _iota((1, c, c), 2)
    tri = (_iota((c, c), 1) <= _iota((c, c), 0)).astype(F32)
    eye = ii == jj
    qs, ks, vs, bs, acs = [], [], [], [], []
    for bi in range(nb):
        xb_ref[bi, pad:pad + c, :] = qkv_ref[bi]
        conv = xb_ref[bi, pad - 3:pad - 3 + c, :] * cw_ref[0:1, :]
        for j in range(1, CONV_K):
            conv = conv + xb_ref[bi, pad - 3 + j:pad - 3 + j + c, :] * cw_ref[j:j + 1, :]
        tail = xb_ref[bi, pad + c - (CONV_K - 1):pad + c, :]
        xb_ref[bi, pad - (CONV_K - 1):pad, :] = tail
        conv = _silu(conv)
        ba = ba_ref[bi]
        beta = _sigmoid(ba)
        g = -jnp.exp(pa_ref[0:1, :]) * _softplus(ba + pa_ref[1:2, :])
        acum = _dot(tri, g, HIGHEST)
        for h in range(N_HEADS):
            qs.append(conv[:, h * HEAD_DIM:(h + 1) * HEAD_DIM])
            ks.append(conv[:, 256 + h * HEAD_DIM:256 + (h + 1) * HEAD_DIM])
            vs.append(conv[:, 512 + h * HEAD_DIM:512 + (h + 1) * HEAD_DIM])
            bs.append(beta[:, h:h + 1])
            acs.append(acum[:, N_HEADS + h:N_HEADS + h + 1])
    q = jnp.stack(qs, 0)
    k = jnp.stack(ks, 0)
    v = jnp.stack(vs, 0)
    bcol = jnp.stack(bs, 0)
    acol = jnp.stack(acs, 0)
    q = q * lax.rsqrt(jnp.sum(q * q, -1, keepdims=True) + NORM_EPS) * SCALE
    k = k * lax.rsqrt(jnp.sum(k * k, -1, keepdims=True) + NORM_EPS)
    arow = jnp.sum(jnp.where(eye, acol, 0.0), 1, keepdims=True)
    decay = jnp.exp(jnp.minimum(acol - arow, 0.0))
    kb = k * bcol
    m = _bdot_nt(kb, k) * jnp.where(jj < ii, decay, 0.0)
    base = min(8, c)
    md = jnp.where(_same_block(ii, jj, base), m, 0.0)
    e = -md
    pw = _bdot(md, md)
    n = 2
    while n < base:
        e = e + pw + _bdot(e, pw)
        n *= 2
        if n < base:
            pw = _bdot(pw, pw)
    size = base
    while size < c:
        off = jnp.where(_same_block(ii, jj, 2 * size) & ~_same_block(ii, jj, size), m, 0.0)
        t1 = off + _bdot(e, off)
        e = e - t1 - _bdot(t1, e)
        size *= 2
    ea = jnp.exp(acol)
    rhs = jnp.concatenate([v * bcol, kb * ea], 2)
    sol = rhs + _bdot(e, rhs)
    s = st_ref[...].reshape(nb * N_HEADS, HEAD_DIM, HEAD_DIM)
    v_new = sol[:, :, 0:HEAD_DIM] - _bdot(sol[:, :, HEAD_DIM:128], s)
    att = _bdot_nt(q, k) * jnp.where(jj <= ii, decay, 0.0)
    o = _bdot(q * ea, s) + _bdot(att, v_new)
    o = o * lax.rsqrt(jnp.mean(o * o, -1, keepdims=True) + NORM_EPS) * ng_ref[...]
    a_last = acol[:, c - 1:c, :]
    kd = k * jnp.exp(a_last - acol)
    for bi in range(nb):
        z = z_ref[bi]
        for h in range(N_HEADS):
            gi = bi * N_HEADS + h
            sl = slice(h * HEAD_DIM, (h + 1) * HEAD_DIM)
            st_ref[bi, h] = jnp.exp(a_last[gi]) * s[gi] + _dot_tn(_mxu(kd[gi]), _mxu(v_new[gi]))
            o_ref[bi, :, sl] = (o[gi] * _silu(z[:, sl])).astype(o_ref.dtype)


def _gdn_call(p, conv_buf, state0, conv_w, a_log, dt_bias, norm_g):
    b, t, _ = p.shape
    c = min(GDN_CHUNK, t)
    nb = max(d for d in (4, 2, 1) if b % d == 0)
    assert t % c == 0 and c >= CONV_K - 1
    pa = jnp.zeros((2, 128), F32).at[0, N_HEADS:2 * N_HEADS].set(a_log).at[1, N_HEADS:2 * N_HEADS].set(dt_bias)
    st = pl.BlockSpec((nb, N_HEADS, HEAD_DIM, HEAD_DIM), lambda b_, i: (b_, 0, 0, 0))
    return pl.pallas_call(
        _gdn_kernel,
        out_shape=(jax.ShapeDtypeStruct((b, t, 256), BF16),
                   jax.ShapeDtypeStruct((b, N_HEADS, HEAD_DIM, HEAD_DIM), F32)),
        grid=(b // nb, t // c),
        in_specs=[pl.BlockSpec((nb, c, 768), lambda b_, i: (b_, i, P_DQKV // 768)),
                  pl.BlockSpec((nb, c, 256), lambda b_, i: (b_, i, P_DZ // 256)),
                  pl.BlockSpec((nb, c, 128), lambda b_, i: (b_, i, P_DBA // 128)),
                  pl.BlockSpec((CONV_K, 768), lambda b_, i: (0, 0)),
                  pl.BlockSpec((nb, CONV_K - 1, 768), lambda b_, i: (b_, 0, 0)),
                  st,
                  pl.BlockSpec((2, 128), lambda b_, i: (0, 0)),
                  pl.BlockSpec((1, HEAD_DIM), lambda b_, i: (0, 0))],
        out_specs=(pl.BlockSpec((nb, c, 256), lambda b_, i: (b_, i, 0)), st),
        scratch_shapes=[pltpu.VMEM((nb, 8 + c, 768), F32)],
        compiler_params=_params(("parallel", "arbitrary")),
        name="gdn",
    )(p, p, p, conv_w, conv_buf, state0, pa, norm_g.reshape(1, HEAD_DIM))


def _outproj_kernel(oa_ref, ob_ref, oc_ref, od_ref, x_ref, g1_ref, sc2_ref, sh2_ref, w_ref,
                    l1g_ref, l1b_ref, rw_ref, rb_ref, x1_ref, u2_ref, wc_ref):
    y = _dot(oa_ref[0], w_ref[0:256, :])
    y = y + _dot(ob_ref[0], w_ref[256:512, :])
    y = y + _dot(oc_ref[0], w_ref[512:768, :])
    y = y + _dot(od_ref[0], w_ref[768:1024, :])
    x1 = _ln(DN_ALPHA * x_ref[0] + g1_ref[0] * y) * l1g_ref[...] + l1b_ref[...]
    x1_ref[0] = x1
    u2 = _ln(x1) * (1.0 + sc2_ref[0]) + sh2_ref[0]
    u2_ref[0] = u2.astype(BF16)
    scores = _sigmoid(_dot_nt(rw_ref[...], u2, HIGHEST))
    biased = scores + rb_ref[...]
    tm = biased.shape[1]
    per = N_EXPERTS // N_GROUPS
    idx = _iota((per, tm), 0).astype(F32)
    grp_rows = []
    for g in range(N_GROUPS):
        vals = biased[g * per:(g + 1) * per, :]
        m1 = jnp.max(vals, 0, keepdims=True)
        first = jnp.min(jnp.where(vals == m1, idx, float(per)), 0, keepdims=True)
        m2 = jnp.max(jnp.where(idx == first, -jnp.inf, vals), 0, keepdims=True)
        grp_rows.append(m1 + m2)
    gs = jnp.concatenate(grp_rows, 0)
    gi = _iota((N_GROUPS, tm), 0)
    rank = jnp.zeros((N_GROUPS, tm), F32)
    for g2 in range(N_GROUPS):
        row = gs[g2:g2 + 1, :]
        rank = rank + jnp.where((row > gs) | ((row == gs) & (g2 < gi)), 1.0, 0.0)
    keep = jnp.where(rank < float(TOPK_GROUPS), 1.0, 0.0)
    emask = jnp.concatenate([jnp.broadcast_to(keep[g:g + 1, :], (per, tm)) for g in range(N_GROUPS)], 0) > 0.5
    sel = _topk_mask(jnp.where(emask, biased, -jnp.inf), TOP_K, 0)
    w = sel * scores
    w = w / jnp.sum(w, 0, keepdims=True) * ROUTED_SCALE
    wc_ref[0] = jnp.concatenate([w, jnp.zeros((LANES - N_EXPERTS, tm), F32)], 0).T


def _outproj_call(oa, ob, oc, od, x, g1, sc2, sh2, w_out, ln_g, ln_b, rw_t, rb):
    g, r, _ = x.shape
    tm = min(512, r)
    o_spec = pl.BlockSpec((1, tm, 256), lambda g_, i: (g_, i, 0))
    x_spec = pl.BlockSpec((1, tm, D_MODEL), lambda g_, i: (g_, i, 0))
    vec = pl.BlockSpec((1, D_MODEL), lambda g_, i: (0, 0))
    return pl.pallas_call(
        _outproj_kernel,
        out_shape=(jax.ShapeDtypeStruct((g, r, D_MODEL), F32),
                   jax.ShapeDtypeStruct((g, r, D_MODEL), BF16),
                   jax.ShapeDtypeStruct((g, r, LANES), F32)),
        grid=(g, r // tm),
        in_specs=[o_spec, o_spec, o_spec, o_spec, x_spec,
                  _mod_spec(g1, tm), _mod_spec(sc2, tm), _mod_spec(sh2, tm),
                  pl.BlockSpec((D_MODEL, D_MODEL), lambda g_, i: (0, 0)),
                  vec, vec,
                  pl.BlockSpec((N_EXPERTS, D_MODEL), lambda g_, i: (0, 0)),
                  pl.BlockSpec((N_EXPERTS, 1), lambda g_, i: (0, 0))],
        out_specs=(x_spec, x_spec, pl.BlockSpec((1, tm, LANES), lambda g_, i: (g_, i, 0))),
        compiler_params=_params(("parallel", "parallel")),
        name="outproj",
    )(oa, ob, oc, od, x, g1, sc2, sh2, w_out, ln_g.reshape(1, -1), ln_b.reshape(1, -1), rw_t, rb.reshape(-1, 1))


EXPERTS_PER_STEP = 4


def _swiglu_act(hid):
    return _silu(hid[:, 0:EXPERT_FF]) * hid[:, EXPERT_FF:2 * EXPERT_FF]


def _moe_kernel(u_ref, wc_ref, x_ref, g2_ref, wgu_ref, wdn_ref, sgu_ref, sdn_ref, l2g_ref, l2b_ref,
                o_ref, acc_ref):
    step = pl.program_id(2)
    eps = wgu_ref.shape[0]
    u = u_ref[0]

    @pl.when(step == 0)
    def _shared():
        acc_ref[...] = _dot(_swiglu_act(_dot(u, sgu_ref[...])).astype(BF16), sdn_ref[...])

    wc = wc_ref[0]
    lane = _iota(wc.shape, 1)
    acts = []
    for k in range(eps):
        col = jnp.sum(jnp.where(lane == step * eps + k, wc, 0.0), -1, keepdims=True)
        acts.append((_swiglu_act(_dot(u, wgu_ref[k])) * col).astype(BF16))
    act = jnp.concatenate(acts, 1)
    acc_ref[...] += _dot(act, wdn_ref[...].reshape(eps * EXPERT_FF, D_MODEL))

    @pl.when(step == pl.num_programs(2) - 1)
    def _finish():
        o_ref[0] = _ln(DN_ALPHA * x_ref[0] + g2_ref[0] * acc_ref[...]) * l2g_ref[...] + l2b_ref[...]


def _moe_call(u2, wc, x1, g2, wgu, wdn, layer, sgu, sdn, ln_g, ln_b):
    g, r, _ = x1.shape
    tm = min(1024, r)
    eps = EXPERTS_PER_STEP
    tok = lambda width: pl.BlockSpec((1, tm, width), lambda g_, i, e: (g_, i, 0))
    if g2.shape[1] == 1:
        g2_spec = pl.BlockSpec((1, 1, D_MODEL), lambda g_, i, e: (g_, 0, 0))
    else:
        g2_spec = tok(D_MODEL)
    vec = pl.BlockSpec((1, D_MODEL), lambda g_, i, e: (0, 0))
    return pl.pallas_call(
        _moe_kernel,
        out_shape=jax.ShapeDtypeStruct((g, r, D_MODEL), F32),
        grid=(g, r // tm, N_EXPERTS // eps),
        in_specs=[tok(D_MODEL), tok(LANES), tok(D_MODEL), g2_spec,
                  pl.BlockSpec((None, eps, D_MODEL, 2 * EXPERT_FF), lambda g_, i, e: (layer, e, 0, 0)),
                  pl.BlockSpec((None, eps, EXPERT_FF, D_MODEL), lambda g_, i, e: (layer, e, 0, 0)),
                  pl.BlockSpec((D_MODEL, 2 * EXPERT_FF), lambda g_, i, e: (0, 0)),
                  pl.BlockSpec((EXPERT_FF, D_MODEL), lambda g_, i, e: (0, 0)),
                  vec, vec],
        out_specs=tok(D_MODEL),
        scratch_shapes=[pltpu.VMEM((tm, D_MODEL), F32)],
        compiler_params=_params(("parallel", "parallel", "arbitrary")),
        name="moe",
    )(u2, wc, x1, g2, wgu, wdn, sgu, sdn, ln_g.reshape(1, -1), ln_b.reshape(1, -1))


def _run_trunk(x, mod, p0, weights, stacked, past):
    b, t, _ = x.shape
    per_token = t < 128
    if per_token:
        grp = lambda a: a.reshape(1, b * t, a.shape[-1])
        mod_rows = lambda m: jnp.repeat(m, t, axis=0)[None]
    else:
        grp = lambda a: a
        mod_rows = lambda m: m[:, None, :]
    ungrp = lambda a: a.reshape(b, t, a.shape[-1])

    outs = []
    for l in range(DEPTH):
        w = {k: v[l] for k, v in weights.items()}
        sh1, sc1, g1, sh2, sc2, g2 = [mod_rows(m) for m in jnp.split(mod[l], 6, axis=-1)]
        p = ungrp(_proj_call(grp(x), sc1, sh1, w["w_in"]))
        if past is None:
            o_a, nsa_rows, win_rows = _nsa_prompt_call(p, w["nsa_pool"])
            win_new = win_rows[:, t - min(NSA_WINDOW, t):]
            o_c, logf = _fox_prompt_call(p, w["fox_f_bias"])
            ret_s0 = jnp.zeros((b, N_HEADS, HEAD_DIM, HEAD_DIM), F32)
            gdn_s0 = ret_s0
            conv_buf = jnp.zeros((b, CONV_K - 1, 3 * GROUP_WIDTH), F32)
        else:
            o_a, nsa_rows, win_new = _nsa_sample_call(p, past["nsa_t"], past["page_table"], l,
                                                      past["win_t"][l], w["nsa_pool"], p0)
            o_c, logf = _fox_sample_call(p, past["fox_kv_t"], past["fox_lf_t"], past["page_table"], l,
                                         w["fox_f_bias"])
            ret_s0, gdn_s0, conv_buf = past["state_ret"][l], past["state_gdn"][l], past["state_gdn_conv"][l]
        o_b, ret_s = _ret_call(p, ret_s0, w["ret_gn_g"], w["ret_gn_b"], p0)
        o_d, gdn_s = _gdn_call(p, conv_buf, gdn_s0, w["gdn_conv_w"], w["gdn_A_log"], w["gdn_dt_bias"],
                               w["gdn_norm_g"])
        x1, u2, wc = _outproj_call(grp(o_a), grp(o_b), grp(o_c), grp(o_d), grp(x), g1, sc2, sh2,
                                   w["w_out"], w["ln1_g"], w["ln1_b"], w["router_w_t"], w["router_b"])
        x = ungrp(_moe_call(u2, wc, x1, g2, stacked["exp_w_gu"], stacked["exp_w_down"], l,
                            w["sh_w_gu"], w["sh_w_down"], w["ln2_g"], w["ln2_b"]))
        qkv = p[:, :, P_DQKV:P_DQKV + 768]
        conv_new = jnp.concatenate([conv_buf, qkv], axis=1)[:, t:]
        outs.append((nsa_rows.reshape(b, t, 4, HEAD_DIM),
                     p[:, :, P_CK:P_CK + 512].reshape(b, t, 2, N_HEADS, HEAD_DIM),
                     logf[:, :, :N_HEADS],
                     win_new.reshape(b, win_new.shape[1], 2, HEAD_DIM),
                     ret_s, gdn_s, conv_new))
    nsa, fkv, flf, win, ret, gdn, conv = zip(*outs)
    return x, (jnp.stack(nsa, 1), jnp.stack(fkv, 1), jnp.stack(flf, 1), jnp.stack(win, 0),
               jnp.stack(ret, 0), jnp.stack(gdn, 0), jnp.stack(conv, 0))


def kernel(x_prompt, x_sample, cache_nsa, cache_fox_kv, cache_fox_logf, state_nsa_win, state_ret, state_gdn, state_gdn_conv, page_table, c_prompt, c_sample, w_mod, b_mod, w_in, w_out, nsa_pool, ret_gn_g, ret_gn_b, fox_f_bias, gdn_conv_w, gdn_A_log, gdn_dt_bias, gdn_norm_g, ln1_g, ln1_b, ln2_g, ln2_b, router_w, router_b, exp_w_gu, exp_w_down, sh_w_gu, sh_w_down):
    b = x_prompt.shape[0]
    db = x_sample.shape[0]
    n_pool, _, page_sz = cache_nsa.shape[:3]
    past_len = page_table.shape[1] * page_sz
    w_in_p = _permute_columns(w_in.astype(BF16), _proj_perm())
    weights = dict(
        w_in=w_in_p, w_out=w_out.astype(BF16), nsa_pool=nsa_pool, ret_gn_g=ret_gn_g, ret_gn_b=ret_gn_b,
        fox_f_bias=fox_f_bias, gdn_conv_w=gdn_conv_w, gdn_A_log=gdn_A_log, gdn_dt_bias=gdn_dt_bias,
        gdn_norm_g=gdn_norm_g, ln1_g=ln1_g, ln1_b=ln1_b, ln2_g=ln2_g, ln2_b=ln2_b,
        router_w_t=jnp.swapaxes(router_w, 1, 2), router_b=router_b,
        sh_w_gu=sh_w_gu.astype(BF16), sh_w_down=sh_w_down.astype(BF16))
    stacked = dict(exp_w_gu=exp_w_gu.astype(BF16), exp_w_down=exp_w_down.astype(BF16))
    n_c = b + db
    n_pad = -n_c % 8
    c_all = jnp.concatenate([c_prompt, c_sample, jnp.zeros((n_pad, D_MODEL), F32)], axis=0)
    mod = _mod_call(c_all, w_mod, b_mod)
    past = dict(
        nsa_t=jnp.transpose(cache_nsa, (0, 1, 3, 4, 2)).reshape(n_pool, DEPTH, 4 * HEAD_DIM, page_sz),
        fox_kv_t=jnp.transpose(cache_fox_kv, (0, 1, 3, 4, 5, 2)).reshape(n_pool, DEPTH, 2 * GROUP_WIDTH, page_sz),
        fox_lf_t=jnp.transpose(cache_fox_logf, (0, 1, 3, 2)),
        win_t=jnp.transpose(state_nsa_win, (0, 1, 3, 4, 2)).reshape(DEPTH, db, 2 * HEAD_DIM, state_nsa_win.shape[2]),
        state_ret=state_ret, state_gdn=state_gdn, state_gdn_conv=state_gdn_conv, page_table=page_table)
    y_p, (nsa_p, fkv_p, flf_p, win_p, ret_p, gdn_p, conv_p) = _run_trunk(x_prompt, mod[:, :b], 0, weights, stacked, None)
    y_s, (nsa_s, fkv_s, flf_s, win_s, ret_s, gdn_s, conv_s) = _run_trunk(x_sample, mod[:, b:n_c], past_len, weights, stacked, past)
    return (y_p, y_s, nsa_p, nsa_s, fkv_p, fkv_s, flf_p, flf_s, win_p, win_s,
            ret_p, ret_s, gdn_p, gdn_s, conv_p, conv_s)
```

```python
import functools
import math

import numpy as np
import jax
import jax.numpy as jnp
from jax import lax
from jax.experimental import pallas as pl
from jax.experimental.pallas import tpu as pltpu

F32 = jnp.float32
BF16 = jnp.bfloat16
HIGHEST = lax.Precision.HIGHEST

D_MODEL = 1024
DEPTH = 4
HEAD_DIM = 64
N_HEADS = 4
GROUP_WIDTH = N_HEADS * HEAD_DIM
NSA_BLOCK = 64
NSA_TOPN = 8
NSA_WINDOW = 512
NSA_FORCE = 1.0e4
ROPE_THETA = 500000.0
ROPE_DIMS = HEAD_DIM // 4
RET_THETA = 10000.0
RET_CHUNK = 128
GDN_CHUNK = 64
CONV_K = 4
N_EXPERTS = 64
TOP_K = 8
N_GROUPS = 8
TOPK_GROUPS = 4
EXPERT_FF = 256
ROUTED_SCALE = 2.5
DN_ALPHA = (2 * DEPTH) ** 0.25
LN_EPS = 1e-5
NORM_EPS = 1e-6
NEG_BIG = -1e30
SCALE = HEAD_DIM ** -0.5
QUERY_BLOCK = 256
LANES = 128
VMEM_LIMIT = 56 * 1024 * 1024

IN_SPLITS = (GROUP_WIDTH, 6 * HEAD_DIM, 3 * N_HEADS,
             GROUP_WIDTH, GROUP_WIDTH, GROUP_WIDTH, GROUP_WIDTH,
             GROUP_WIDTH, GROUP_WIDTH, GROUP_WIDTH, N_HEADS,
             3 * GROUP_WIDTH, N_HEADS, N_HEADS, GROUP_WIDTH)
IN_WIDTH = sum(IN_SPLITS)

P_AQ, P_BQ, P_BK, P_BV, P_BZ = 0, 256, 512, 768, 1024
P_CQ, P_CK, P_CV, P_DZ, P_DQKV = 1280, 1536, 1792, 2048, 2304
P_AKC, P_AKS, P_AKW, P_AG, P_CF, P_DBA = 3072, 3200, 3328, 3456, 3584, 3712
P_WIDTH = 3840


def _proj_perm():
    src = np.cumsum((0,) + IN_SPLITS)
    (q_a, kv_a, g_a, q_b, k_b, v_b, z_b, q_c, k_c, v_c, f_c, qkv_d, beta_d, a_d, z_d) = [int(s) for s in src[:-1]]
    perm = -np.ones((P_WIDTH,), np.int64)

    def put(dst, start, width):
        perm[dst:dst + width] = np.arange(start, start + width)

    put(P_AQ, q_a, 256)
    put(P_AKC, kv_a, 128)
    put(P_AKS, kv_a + 128, 128)
    put(P_AKW, kv_a + 256, 128)
    for h in range(N_HEADS):
        for j in range(3):
            perm[P_AG + j * N_HEADS + h] = g_a + h * 3 + j
    put(P_BQ, q_b, 256)
    put(P_BK, k_b, 256)
    put(P_BV, v_b, 256)
    put(P_BZ, z_b, 256)
    put(P_CQ, q_c, 256)
    put(P_CK, k_c, 256)
    put(P_CV, v_c, 256)
    put(P_CF, f_c, 4)
    put(P_DQKV, qkv_d, 768)
    put(P_DZ, z_d, 256)
    put(P_DBA, beta_d, 4)
    put(P_DBA + 4, a_d, 4)
    return perm


def _permute_columns(w, perm):
    pieces = []
    i, n = 0, len(perm)
    while i < n:
        j = i + 1
        if perm[i] < 0:
            while j < n and perm[j] < 0:
                j += 1
            pieces.append(jnp.zeros(w.shape[:-1] + (j - i,), w.dtype))
        else:
            while j < n and perm[j] == perm[j - 1] + 1:
                j += 1
            pieces.append(w[..., int(perm[i]):int(perm[i]) + (j - i)])
        i = j
    return jnp.concatenate(pieces, -1)


def _rope_tables(pos, n_rot, theta, n_heads, pad_identity=0):
    half = n_rot // 2
    inv = theta ** (-np.arange(half, dtype=np.float64) / half)
    ang = np.asarray(pos, np.float64)[:, None] * inv[None, :]
    t = ang.shape[0]
    c = np.ones((t, HEAD_DIM)); sa = np.zeros((t, HEAD_DIM)); sb = np.zeros((t, HEAD_DIM))
    c[:, :half] = np.cos(ang); c[:, half:n_rot] = np.cos(ang)
    sa[:, :half] = -np.sin(ang)
    sb[:, half:n_rot] = np.sin(ang)
    c = np.tile(c, (1, n_heads)); sa = np.tile(sa, (1, n_heads)); sb = np.tile(sb, (1, n_heads))
    if pad_identity:
        c = np.concatenate([c, np.ones((t, pad_identity))], 1)
        sa = np.concatenate([sa, np.zeros((t, pad_identity))], 1)
        sb = np.concatenate([sb, np.zeros((t, pad_identity))], 1)
    return tuple(jnp.asarray(a, F32) for a in (c, sa, sb))


def _dot(a, b, prec=None):
    return jnp.dot(a, b, preferred_element_type=F32, precision=prec)


def _dot_nt(a, b, prec=None):
    return lax.dot_general(a, b, (((1,), (1,)), ((), ())), preferred_element_type=F32, precision=prec)


def _dot_tn(a, b, prec=None):
    return lax.dot_general(a, b, (((0,), (0,)), ((), ())), preferred_element_type=F32, precision=prec)


def _iota(shape, axis):
    return lax.broadcasted_iota(jnp.int32, shape, axis)


def _ln(x):
    mu = jnp.mean(x, -1, keepdims=True)
    xc = x - mu
    var = jnp.mean(xc * xc, -1, keepdims=True)
    return xc * lax.rsqrt(var + LN_EPS)


def _sigmoid(x):
    return 1.0 / (1.0 + jnp.exp(-x))


def _silu(x):
    return x * _sigmoid(x)


def _softplus(x):
    return jnp.maximum(x, 0.0) + jnp.log1p(jnp.exp(-jnp.abs(x)))


def _log_sigmoid(x):
    return -_softplus(-x)


def _rope(x, c, sa, sb, half):
    w = x.shape[-1]
    return x * c + pltpu.roll(x, w - half, 1) * sa + pltpu.roll(x, half, 1) * sb


def _masked_softmax(s, mask, axis):
    s = jnp.where(mask, s, NEG_BIG)
    e = jnp.where(mask, jnp.exp(s - jnp.max(s, axis, keepdims=True)), 0.0)
    return e / jnp.maximum(jnp.sum(e, axis, keepdims=True), 1e-30)


def _softmax_pv(s, mask, v):
    s = jnp.where(mask, s, NEG_BIG)
    m = jnp.max(s, -1, keepdims=True)
    e = jnp.exp(s - m)
    den = jnp.sum(e, -1, keepdims=True)
    inv = jnp.where(m > 0.5 * NEG_BIG, 1.0 / jnp.maximum(den, 1e-30), 0.0)
    return _dot(e.astype(BF16), v) * inv


def _topk_mask(vals, k, axis):
    n = vals.shape[axis]
    idx = _iota(vals.shape, axis).astype(F32)
    sel = jnp.zeros(vals.shape, F32)
    work = vals
    for _ in range(k):
        m = jnp.max(work, axis, keepdims=True)
        first = jnp.min(jnp.where(work == m, idx, float(n)), axis, keepdims=True)
        pick = idx == first
        sel = jnp.where(pick, 1.0, sel)
        work = jnp.where(pick, -jnp.inf, work)
    return sel


def _pool_weights(pool_ref, n_rep):
    pl_t = pool_ref[...]
    e = jnp.exp(pl_t - jnp.max(pl_t, -1, keepdims=True))
    return e / (jnp.sum(e, -1, keepdims=True) / float(n_rep))


CAUSAL_GROUPS = 4


def _causal_branches(qi, n_qblocks, t, body):
    groups = CAUSAL_GROUPS if n_qblocks % CAUSAL_GROUPS == 0 else 1
    per = n_qblocks // groups
    for r in range(groups):
        pl.when((qi >= r * per) & (qi < (r + 1) * per))(functools.partial(body, (r + 1) * (t // groups)))


def _params(sem):
    return pltpu.CompilerParams(dimension_semantics=sem, vmem_limit_bytes=VMEM_LIMIT)


def _mod_kernel(c_ref, w_ref, b_ref, o_ref):
    o_ref[0] = _dot(c_ref[...].astype(BF16), w_ref[0].astype(BF16)) + b_ref[0]


def _mod_call(c_all, w_mod, b_mod):
    n = c_all.shape[0]
    tn = 1536
    return pl.pallas_call(
        _mod_kernel,
        out_shape=jax.ShapeDtypeStruct((DEPTH, n, 6 * D_MODEL), F32),
        grid=(DEPTH, 6 * D_MODEL // tn),
        in_specs=[pl.BlockSpec((n, D_MODEL), lambda l, j: (0, 0)),
                  pl.BlockSpec((1, D_MODEL, tn), lambda l, j: (l, 0, j)),
                  pl.BlockSpec((1, 1, tn), lambda l, j: (l, 0, j))],
        out_specs=pl.BlockSpec((1, n, tn), lambda l, j: (l, 0, j)),
        compiler_params=_params(("parallel", "parallel")),
        name="mod",
    )(c_all, w_mod, b_mod.reshape(DEPTH, 1, 6 * D_MODEL))


def _proj_kernel(x_ref, sc_ref, sh_ref, w_ref, o_ref):
    u = _ln(x_ref[0]) * (1.0 + sc_ref[0]) + sh_ref[0]
    o_ref[0] = _dot(u.astype(BF16), w_ref[...])


def _mod_spec(m, tm):
    if m.shape[1] == 1:
        return pl.BlockSpec((1, 1, D_MODEL), lambda g, i: (g, 0, 0))
    return pl.BlockSpec((1, tm, D_MODEL), lambda g, i: (g, i, 0))


def _proj_call(x, sc, sh, w):
    g, r, _ = x.shape
    tm = min(512, r)
    return pl.pallas_call(
        _proj_kernel,
        out_shape=jax.ShapeDtypeStruct((g, r, P_WIDTH), F32),
        grid=(g, r // tm),
        in_specs=[pl.BlockSpec((1, tm, D_MODEL), lambda g_, i: (g_, i, 0)),
                  _mod_spec(sc, tm), _mod_spec(sh, tm),
                  pl.BlockSpec((D_MODEL, P_WIDTH), lambda g_, i: (0, 0))],
        out_specs=pl.BlockSpec((1, tm, P_WIDTH), lambda g_, i: (g_, i, 0)),
        compiler_params=_params(("parallel", "parallel")),
        name="proj",
    )(x, sc, sh, w)


def _nsa_prompt_kernel(q_ref, g_ref, kc_ref, ks_ref, kw_ref, pool_ref,
                       qc_ref, qa_ref, qb_ref, kc_t_ref, ka_t_ref, kb_t_ref,
                       oa_ref, rows_ref, win_ref,
                       comp_ref, ksb_ref, vsb_ref, kwp_ref, vwp_ref, osel_ref):
    qi = pl.program_id(1)
    t = kc_ref.shape[1]
    nb = t // NSA_BLOCK
    qb = q_ref.shape[1]
    wnd = NSA_WINDOW

    @pl.when(qi == 0)
    def _prep():
        kcvc = kc_ref[0]
        ks_rot = _rope(ks_ref[0], kc_t_ref[...], ka_t_ref[...], kb_t_ref[...], ROPE_DIMS // 2)
        kw_rot = _rope(kw_ref[0], kc_t_ref[...], ka_t_ref[...], kb_t_ref[...], ROPE_DIMS // 2)
        rows_ref[0, :, 0:128] = kcvc
        rows_ref[0, :, 128:256] = ks_rot
        win_ref[0] = kw_rot
        ksb_ref[...] = ks_rot[:, 0:64].astype(BF16)
        vsb_ref[...] = ks_rot[:, 64:128].astype(BF16)
        kwp_ref[0:wnd, :] = jnp.zeros((wnd, HEAD_DIM), BF16)
        vwp_ref[0:wnd, :] = jnp.zeros((wnd, HEAD_DIM), BF16)
        kwp_ref[wnd:wnd + t, :] = kw_rot[:, 0:64].astype(BF16)
        vwp_ref[wnd:wnd + t, :] = kw_rot[:, 64:128].astype(BF16)
        wts = _pool_weights(pool_ref, nb)
        same = (_iota((nb, t), 1) >> 6) == _iota((nb, t), 0)
        pk = jnp.where(same, wts[0:1, :], 0.0)
        pv = jnp.where(same, wts[1:2, :], 0.0)
        ck = _dot(pk, kcvc, HIGHEST)
        cv = _dot(pv, kcvc, HIGHEST)
        comp_ref[...] = jnp.where(_iota((nb, 128), 1) < HEAD_DIM, ck, cv)

    s0 = pl.multiple_of(qi * qb, qb)
    q = q_ref[0] * SCALE
    qr = _rope(q, qc_ref[...], qa_ref[...], qb_ref[...], ROPE_DIMS // 2)
    gates = _sigmoid(g_ref[0])
    comp = comp_ref[...]
    compk = comp[:, 0:HEAD_DIM]
    compv = comp[:, HEAD_DIM:128]
    qp = s0 + _iota((qb, 1), 0)
    qp_row = s0 + _iota((1, qb), 1)
    blk = _iota((nb, 1), 0)
    cmask = blk < ((qp_row + 1) >> 6)
    imp = jnp.zeros((nb, qb), F32)
    o_cmp = []
    for h in range(N_HEADS):
        qh = q[:, h * HEAD_DIM:(h + 1) * HEAD_DIM]
        pc = _masked_softmax(_dot_nt(compk, qh, HIGHEST), cmask, 0)
        imp = imp + pc
        o_cmp.append(_dot_tn(pc, compv))
    cur = qp_row >> 6
    imp = jnp.where((blk == cur) | (blk == 0), NSA_FORCE, imp)
    imp = jnp.where(blk <= cur, imp, -1.0)
    sel = _topk_mask(imp, min(NSA_TOPN, nb), 0)
    qr_heads = [qr[:, h * HEAD_DIM:(h + 1) * HEAD_DIM].astype(BF16) for h in range(N_HEADS)]

    def _selected(ext):
        expand = ((_iota((nb, ext), 1) >> 6) == _iota((nb, ext), 0)).astype(F32)
        selk = _dot_tn(sel, expand)
        smask = (selk > 0.5) & (_iota((1, ext), 1) <= qp)
        ksb = ksb_ref[0:ext, :]
        vsb = vsb_ref[0:ext, :]
        for h in range(N_HEADS):
            osel_ref[:, h * HEAD_DIM:(h + 1) * HEAD_DIM] = _softmax_pv(_dot_nt(qr_heads[h], ksb), smask, vsb)

    _causal_branches(qi, t // qb, t, _selected)
    kw = kwp_ref[pl.ds(s0, wnd + qb), :]
    vw = vwp_ref[pl.ds(s0, wnd + qb), :]
    wpos = s0 - wnd + _iota((1, wnd + qb), 1)
    wmask = (wpos >= 0) & (wpos <= qp) & (wpos > qp - wnd)
    for h in range(N_HEADS):
        o_win = _softmax_pv(_dot_nt(qr_heads[h], kw), wmask, vw)
        o_sel = osel_ref[:, h * HEAD_DIM:(h + 1) * HEAD_DIM]
        out = (gates[:, h:h + 1] * o_cmp[h] + gates[:, N_HEADS + h:N_HEADS + h + 1] * o_sel
               + gates[:, 2 * N_HEADS + h:2 * N_HEADS + h + 1] * o_win)
        oa_ref[0, :, h * HEAD_DIM:(h + 1) * HEAD_DIM] = out.astype(oa_ref.dtype)


def _nsa_prompt_call(p, pool_l):
    b, t, _ = p.shape
    qb = QUERY_BLOCK
    nb = t // NSA_BLOCK
    pos = np.arange(t)
    q_tabs = _rope_tables(pos, ROPE_DIMS, ROPE_THETA, N_HEADS)
    k_tabs = _rope_tables(pos, ROPE_DIMS, ROPE_THETA, 1, pad_identity=HEAD_DIM)
    pool_t = jnp.tile(pool_l, (1, nb))
    full = lambda col: pl.BlockSpec((1, t, 128), lambda b_, i: (b_, 0, col))
    qtab = pl.BlockSpec((qb, 256), lambda b_, i: (i, 0))
    ktab = pl.BlockSpec((t, 128), lambda b_, i: (0, 0))
    return pl.pallas_call(
        _nsa_prompt_kernel,
        out_shape=(jax.ShapeDtypeStruct((b, t, 256), BF16),
                   jax.ShapeDtypeStruct((b, t, 256), F32),
                   jax.ShapeDtypeStruct((b, t, 128), F32)),
        grid=(b, t // qb),
        in_specs=[pl.BlockSpec((1, qb, 256), lambda b_, i: (b_, i, P_AQ // 256)),
                  pl.BlockSpec((1, qb, 128), lambda b_, i: (b_, i, P_AG // 128)),
                  full(P_AKC // 128), full(P_AKS // 128), full(P_AKW // 128),
                  pl.BlockSpec((2, t), lambda b_, i: (0, 0)),
                  qtab, qtab, qtab, ktab, ktab, ktab],
        out_specs=(pl.BlockSpec((1, qb, 256), lambda b_, i: (b_, i, 0)),
                   pl.BlockSpec((1, t, 256), lambda b_, i: (b_, 0, 0)),
                   pl.BlockSpec((1, t, 128), lambda b_, i: (b_, 0, 0))),
        scratch_shapes=[pltpu.VMEM((nb, 128), F32),
                        pltpu.VMEM((t, HEAD_DIM), BF16), pltpu.VMEM((t, HEAD_DIM), BF16),
                        pltpu.VMEM((NSA_WINDOW + t, HEAD_DIM), BF16), pltpu.VMEM((NSA_WINDOW + t, HEAD_DIM), BF16),
                        pltpu.VMEM((qb, GROUP_WIDTH), F32)],
        compiler_params=_params(("parallel", "arbitrary")),
        name="nsa_prompt",
    )(p, p, p, p, p, pool_t, *q_tabs, *k_tabs)


def _softmax2(s1, mask1, s2, mask2):
    s1 = jnp.where(mask1, s1, NEG_BIG)
    s2 = jnp.where(mask2, s2, NEG_BIG)
    m = jnp.maximum(jnp.max(s1, -1, keepdims=True), jnp.max(s2, -1, keepdims=True))
    e1 = jnp.where(mask1, jnp.exp(s1 - m), 0.0)
    e2 = jnp.where(mask2, jnp.exp(s2 - m), 0.0)
    den = jnp.maximum(jnp.sum(e1, -1, keepdims=True) + jnp.sum(e2, -1, keepdims=True), 1e-30)
    return e1, e2, den


def _nsa_sample_kernel(pt_ref, *refs, past, pps):
    page_refs = refs[:pps]
    (q_ref, g_ref, kc_ref, ks_ref, kw_ref, hist_ref, pool_ref,
     qc_ref, qa_ref, qb_ref, kc_t_ref, ka_t_ref, kb_t_ref,
     oa_ref, rows_ref, win_ref, comp_ref, kst_ref, vst_ref) = refs[pps:]
    i = pl.program_id(1)
    t = q_ref.shape[1]
    page_sz = page_refs[0].shape[3]
    n_pages = past // page_sz
    cr = comp_ref.shape[0]
    wts = _pool_weights(pool_ref, page_sz // NSA_BLOCK)
    r16 = _iota((16, page_sz), 0)
    half16 = _iota((16, page_sz), 1) >> 6
    pkv = jnp.where((r16 == half16), wts[0:1, :], 0.0) + jnp.where((r16 - 8 == half16), wts[1:2, :], 0.0)
    lane_lo = _iota((8, 128), 1) < HEAD_DIM

    @pl.when(i == 0)
    def _init():
        comp_ref[cr - 16:cr, :] = jnp.zeros((16, 128), F32)

    pkv_hi = pkv.astype(BF16)
    pkv_split = jnp.concatenate([pkv_hi, (pkv - pkv_hi.astype(F32)).astype(BF16)], 0)
    for j in range(pps):
        page_t = page_refs[j][0, 0]
        x = page_t[0:128, :]
        x_hi = x.astype(BF16)
        x_lo = (x - x_hi.astype(F32)).astype(BF16)
        r_hi = _dot_nt(pkv_split, x_hi)
        res = r_hi[0:16] + r_hi[16:32] + _dot_nt(pkv_hi, x_lo)
        pg = i * pps + j
        comp_ref[pl.ds(pl.multiple_of(pg * 8, 8), 8), :] = jnp.where(lane_lo, res[0:8], res[8:16])
        col0 = pl.multiple_of(pg * page_sz, page_sz)
        kst_ref[:, pl.ds(col0, page_sz)] = page_t[128:192, :].astype(BF16)
        vst_ref[:, pl.ds(col0, page_sz)] = page_t[192:256, :].astype(BF16)

    @pl.when(i == pl.num_programs(1) - 1)
    def _finish():
        kcvc = kc_ref[0]
        ks_rot = _rope(ks_ref[0], kc_t_ref[...], ka_t_ref[...], kb_t_ref[...], ROPE_DIMS // 2)
        kw_rot = _rope(kw_ref[0], kc_t_ref[...], ka_t_ref[...], kb_t_ref[...], ROPE_DIMS // 2)
        rows_ref[0, :, 0:128] = kcvc
        rows_ref[0, :, 128:256] = ks_rot
        res_n = _dot(pkv[:, 0:t], kcvc, HIGHEST)
        comp_ref[cr - 16:cr - 8, :] = jnp.where(lane_lo, res_n[0:8], res_n[8:16])

        q = q_ref[0]
        qr = _rope(q, qc_ref[...], qa_ref[...], qb_ref[...], ROPE_DIMS // 2)
        zpad = jnp.zeros((32 - t, HEAD_DIM), F32)
        stack = lambda x: jnp.concatenate(
            [piece for h in range(N_HEADS) for piece in (x[:, h * HEAD_DIM:(h + 1) * HEAD_DIM], zpad)], 0)
        q_all = stack(q)
        qr_all = stack(qr)
        qidx = _iota((128, 1), 0) & 31
        qp = past + qidx

        comp = comp_ref[...]
        compk = comp[:, 0:HEAD_DIM]
        compv = comp[:, HEAD_DIM:128]
        ci = _iota((1, cr), 1)
        blk = 2 * (ci >> 3) + (ci & 7)
        valid = ((ci & 7) < 2) & (blk * NSA_BLOCK < past + t)
        cmask = valid & (blk < ((qp + 1) >> 6))
        pc = _masked_softmax(_dot_nt(q_all, compk, HIGHEST) * SCALE, cmask, -1)
        o_cmp = _dot(pc, compv)
        imp = pc[0:32] + pc[32:64] + pc[64:96] + pc[96:128]
        cur = qp[0:32] >> 6
        imp = jnp.where((blk == cur) | (blk == 0), NSA_FORCE, imp)
        imp = jnp.where(blk <= cur, imp, -1.0)
        imp = jnp.where(valid, imp, -2.0)
        sel32 = _topk_mask(imp, NSA_TOPN, 1)
        sel = jnp.concatenate([sel32] * N_HEADS, 0)

        lane_half = _iota((128, page_sz), 1) < NSA_BLOCK
        selk = jnp.concatenate(
            [jnp.where(lane_half, sel[:, 8 * pg:8 * pg + 1], sel[:, 8 * pg + 1:8 * pg + 2]) for pg in range(n_pages)], 1)
        smask = (selk > 0.5) & (_iota((1, past), 1) <= qp)
        new_idx = _iota((1, t), 1)
        nmask = (sel[:, cr - 16:cr - 15] > 0.5) & (new_idx <= qidx)
        s_past = _dot(qr_all.astype(BF16), kst_ref[...]) * SCALE
        s_new = _dot_nt(qr_all, ks_rot[:, 0:HEAD_DIM]) * SCALE
        e1, e2, den = _softmax2(s_past, smask, s_new, nmask)
        o_sel = (_dot_nt(e1.astype(BF16), vst_ref[...]) + _dot(e2, ks_rot[:, HEAD_DIM:128])) / den

        hist_t = hist_ref[0]
        wb = hist_t.shape[1]
        wpos = past - wb + _iota((1, wb), 1)
        hmask = (wpos >= 0) & (wpos <= qp) & (wpos > qp - NSA_WINDOW)
        wmask = (new_idx <= qidx) & (past + new_idx > qp - NSA_WINDOW)
        s_hist = _dot(qr_all, hist_t[0:HEAD_DIM, :]) * SCALE
        s_wnew = _dot_nt(qr_all, kw_rot[:, 0:HEAD_DIM]) * SCALE
        e1, e2, den = _softmax2(s_hist, hmask, s_wnew, wmask)
        o_win = (_dot_nt(e1, hist_t[HEAD_DIM:128, :]) + _dot(e2, kw_rot[:, HEAD_DIM:128])) / den
        hist_tok = hist_t.T
        win_ref[0, 0:wb - t, :] = hist_tok[t:wb, :]
        win_ref[0, wb - t:wb, :] = kw_rot

        gates = _sigmoid(g_ref[0])
        for h in range(N_HEADS):
            r = slice(h * 32, h * 32 + t)
            out = (gates[:, h:h + 1] * o_cmp[r] + gates[:, N_HEADS + h:N_HEADS + h + 1] * o_sel[r]
                   + gates[:, 2 * N_HEADS + h:2 * N_HEADS + h + 1] * o_win[r])
            oa_ref[0, :, h * HEAD_DIM:(h + 1) * HEAD_DIM] = out.astype(oa_ref.dtype)


def _pages_per_step(n_pages):
    return max(d for d in (8, 4, 2, 1) if n_pages % d == 0)


def _nsa_sample_call(p, cache_t, page_table, layer, hist_t, pool_l, past):
    b, t, _ = p.shape
    n_pages = page_table.shape[1]
    page_sz = cache_t.shape[3]
    wb = hist_t.shape[2]
    assert page_sz == 128 and t == 8 and past == n_pages * page_sz and wb == NSA_WINDOW and past >= wb
    pps = _pages_per_step(n_pages)
    pos = past + np.arange(t)
    q_tabs = _rope_tables(pos, ROPE_DIMS, ROPE_THETA, N_HEADS)
    k_tabs = _rope_tables(pos, ROPE_DIMS, ROPE_THETA, 1, pad_identity=HEAD_DIM)
    pool_t = jnp.tile(pool_l, (1, page_sz // NSA_BLOCK))
    new = lambda col: pl.BlockSpec((1, t, 128), lambda b_, i, pt: (b_, 0, col))
    const = lambda shape: pl.BlockSpec(shape, lambda b_, i, pt: (0,) * len(shape))
    page_spec = lambda j: pl.BlockSpec((1, 1, 256, page_sz), lambda b_, i, pt: (pt[b_, i * pps + j], layer, 0, 0))
    grid_spec = pltpu.PrefetchScalarGridSpec(
        num_scalar_prefetch=1,
        grid=(b, n_pages // pps),
        in_specs=[page_spec(j) for j in range(pps)] + [
            pl.BlockSpec((1, t, 256), lambda b_, i, pt: (b_, 0, P_AQ // 256)),
            new(P_AG // 128), new(P_AKC // 128), new(P_AKS // 128), new(P_AKW // 128),
            pl.BlockSpec((1, 128, wb), lambda b_, i, pt: (b_, 0, 0)),
            const((2, page_sz)),
            const((t, 256)), const((t, 256)), const((t, 256)),
            const((t, 128)), const((t, 128)), const((t, 128))],
        out_specs=(pl.BlockSpec((1, t, 256), lambda b_, i, pt: (b_, 0, 0)),
                   pl.BlockSpec((1, t, 256), lambda b_, i, pt: (b_, 0, 0)),
                   pl.BlockSpec((1, wb, 128), lambda b_, i, pt: (b_, 0, 0))),
        scratch_shapes=[pltpu.VMEM((8 * n_pages + 16, 128), F32),
                        pltpu.VMEM((HEAD_DIM, past), BF16),
                        pltpu.VMEM((HEAD_DIM, past), BF16)])
    return pl.pallas_call(
        functools.partial(_nsa_sample_kernel, past=past, pps=pps),
        out_shape=(jax.ShapeDtypeStruct((b, t, 256), BF16),
                   jax.ShapeDtypeStruct((b, t, 256), F32),
                   jax.ShapeDtypeStruct((b, wb, 128), F32)),
        grid_spec=grid_spec,
        compiler_params=_params(("parallel", "arbitrary")),
        name="nsa_sample",
    )(page_table, *([cache_t] * pps), p, p, p, p, p, hist_t, pool_t, *q_tabs, *k_tabs)


def _ret_kernel(q_ref, k_ref, v_ref, z_ref, s0_ref, c_ref, sa_ref, sb_ref, gng_ref, gnb_ref,
                o_ref, st_ref):
    ci = pl.program_id(1)
    nb, c = q_ref.shape[0], q_ref.shape[1]

    @pl.when(ci == 0)
    def _init():
        st_ref[...] = s0_ref[...]

    tabs = (c_ref[...], sa_ref[...], sb_ref[...])
    qs, ks, vs = [], [], []
    for bi in range(nb):
        q = _rope(q_ref[bi], *tabs, HEAD_DIM // 2)
        k = _rope(k_ref[bi], *tabs, HEAD_DIM // 2) * SCALE
        v = v_ref[bi]
        for h in range(N_HEADS):
            sl = slice(h * HEAD_DIM, (h + 1) * HEAD_DIM)
            qs.append(q[:, sl])
            ks.append(k[:, sl])
            vs.append(v[:, sl])
    q3 = jnp.stack(qs, 0)
    k3 = jnp.stack(ks, 0)
    v3 = jnp.stack(vs, 0)
    lgs = [math.log1p(-2.0 ** (-5.0 - h)) for h in range(N_HEADS)]
    ii = _iota((c, c), 0)
    jj = _iota((c, c), 1)
    diff = (ii - jj).astype(F32)
    rowi = _iota((c, 1), 0).astype(F32)
    dec = jnp.stack([jnp.where(jj <= ii, jnp.exp(jnp.minimum(diff * lg, 0.0)), 0.0) for lg in lgs] * nb, 0)
    ea = jnp.stack([jnp.exp((rowi + 1.0) * lg) for lg in lgs] * nb, 0)
    eb = jnp.stack([jnp.exp((c - 1.0 - rowi) * lg) for lg in lgs] * nb, 0)
    s = st_ref[...].reshape(nb * N_HEADS, HEAD_DIM, HEAD_DIM)
    att = _bdot_nt(q3, k3) * dec
    o = _bdot(att, v3) + _bdot(q3 * ea, s)
    kd = k3 * eb
    mu = jnp.mean(o, -1, keepdims=True)
    oc = o - mu
    var = jnp.mean(oc * oc, -1, keepdims=True)
    on = oc * lax.rsqrt(var + NORM_EPS)
    for bi in range(nb):
        z = z_ref[bi]
        for h in range(N_HEADS):
            gi = bi * N_HEADS + h
            sl = slice(h * HEAD_DIM, (h + 1) * HEAD_DIM)
            st_ref[bi, h] = math.exp(c * lgs[h]) * s[gi] + _dot_tn(_mxu(kd[gi]), _mxu(v3[gi]))
            o_ref[bi, :, sl] = ((on[gi] * gng_ref[:, sl] + gnb_ref[:, sl]) * _silu(z[:, sl])).astype(o_ref.dtype)


def _ret_call(p, state0, gn_g, gn_b, p0):
    b, t, _ = p.shape
    c = min(RET_CHUNK, t)
    assert t % c == 0
    tabs = _rope_tables(p0 + np.arange(t), HEAD_DIM, RET_THETA, N_HEADS)
    nb = 2 if b % 2 == 0 else 1
    blk = lambda col: pl.BlockSpec((nb, c, 256), lambda b_, i: (b_, i, col))
    tab = pl.BlockSpec((c, 256), lambda b_, i: (i, 0))
    st = pl.BlockSpec((nb, N_HEADS, HEAD_DIM, HEAD_DIM), lambda b_, i: (b_, 0, 0, 0))
    vec = pl.BlockSpec((1, 256), lambda b_, i: (0, 0))
    return pl.pallas_call(
        _ret_kernel,
        out_shape=(jax.ShapeDtypeStruct((b, t, 256), BF16),
                   jax.ShapeDtypeStruct((b, N_HEADS, HEAD_DIM, HEAD_DIM), F32)),
        grid=(b // nb, t // c),
        in_specs=[blk(P_BQ // 256), blk(P_BK // 256), blk(P_BV // 256), blk(P_BZ // 256), st,
                  tab, tab, tab, vec, vec],
        out_specs=(pl.BlockSpec((nb, c, 256), lambda b_, i: (b_, i, 0)), st),
        compiler_params=_params(("parallel", "arbitrary")),
        name="ret",
    )(p, p, p, p, state0, *tabs, gn_g.reshape(1, 256), gn_b.reshape(1, 256))


def _fox_prompt_kernel(q_ref, k_ref, v_ref, f_ref, fb_ref, o_ref, lf_ref, cum_ref, cumt_ref, kb_ref, vb_ref):
    qi = pl.program_id(1)
    t = k_ref.shape[1]
    qb = q_ref.shape[1]

    @pl.when(qi == 0)
    def _prep():
        lf = _log_sigmoid(f_ref[0] + fb_ref[...])
        lf_ref[0] = lf
        tri = (_iota((qb, qb), 1) <= _iota((qb, qb), 0)).astype(F32)
        carry = jnp.zeros((1, 128), F32)
        for c in range(t // qb):
            blk = _dot(tri, lf[c * qb:(c + 1) * qb], HIGHEST) + carry
            cum_ref[c * qb:(c + 1) * qb, :] = blk
            carry = blk[qb - 1:qb, :]
        cumt_ref[...] = cum_ref[...].T
        for h in range(N_HEADS):
            kb_ref[h] = k_ref[0, :, h * HEAD_DIM:(h + 1) * HEAD_DIM].astype(BF16)
            vb_ref[h] = v_ref[0, :, h * HEAD_DIM:(h + 1) * HEAD_DIM].astype(BF16)

    s0 = pl.multiple_of(qi * qb, qb)
    q = (q_ref[0] * SCALE).astype(BF16)
    cq = cum_ref[pl.ds(s0, qb), :]
    qrow = s0 + _iota((qb, 1), 0)

    def _attend(ext):
        mask = _iota((1, ext), 1) <= qrow
        for h in range(N_HEADS):
            qh = q[:, h * HEAD_DIM:(h + 1) * HEAD_DIM]
            s = _dot_nt(qh, kb_ref[h, 0:ext, :]) + (cq[:, h:h + 1] - cumt_ref[h:h + 1, 0:ext])
            o = _softmax_pv(s, mask, vb_ref[h, 0:ext, :])
            o_ref[0, :, h * HEAD_DIM:(h + 1) * HEAD_DIM] = o.astype(o_ref.dtype)

    _causal_branches(qi, t // qb, t, _attend)


def _fox_prompt_call(p, f_bias):
    b, t, _ = p.shape
    qb = QUERY_BLOCK
    fb = jnp.zeros((1, 128), F32).at[0, :N_HEADS].set(f_bias)
    return pl.pallas_call(
        _fox_prompt_kernel,
        out_shape=(jax.ShapeDtypeStruct((b, t, 256), BF16),
                   jax.ShapeDtypeStruct((b, t, 128), F32)),
        grid=(b, t // qb),
        in_specs=[pl.BlockSpec((1, qb, 256), lambda b_, i: (b_, i, P_CQ // 256)),
                  pl.BlockSpec((1, t, 256), lambda b_, i: (b_, 0, P_CK // 256)),
                  pl.BlockSpec((1, t, 256), lambda b_, i: (b_, 0, P_CV // 256)),
                  pl.BlockSpec((1, t, 128), lambda b_, i: (b_, 0, P_CF // 128)),
                  pl.BlockSpec((1, 128), lambda b_, i: (0, 0))],
        out_specs=(pl.BlockSpec((1, qb, 256), lambda b_, i: (b_, i, 0)),
                   pl.BlockSpec((1, t, 128), lambda b_, i: (b_, 0, 0))),
        scratch_shapes=[pltpu.VMEM((t, 128), F32), pltpu.VMEM((128, t), F32),
                        pltpu.VMEM((N_HEADS, t, HEAD_DIM), BF16), pltpu.VMEM((N_HEADS, t, HEAD_DIM), BF16)],
        compiler_params=_params(("parallel", "arbitrary")),
        name="fox_prompt",
    )(p, p, p, p, fb)


def _rows_per_head(x, rows):
    return jnp.concatenate([jnp.broadcast_to(x[h:h + 1, :], (rows, x.shape[1])) for h in range(N_HEADS)], 0)


def _fox_sample_kernel(pt_ref, *refs, pps, nbr):
    kv_refs = refs[:nbr * pps]
    lf_refs = refs[nbr * pps:2 * nbr * pps]
    (q_ref, k_ref, v_ref, f_ref, fb_ref, o_ref, lf_ref,
     qbd_ref, m_ref, l_ref, acc_ref, carry_ref, cnew_ref) = refs[2 * nbr * pps:]
    i = pl.program_id(1)
    t = q_ref.shape[1]
    page_sz = kv_refs[0].shape[3]
    qidx = _iota((128, 1), 0) & 31

    @pl.when(i == 0)
    def _init():
        for r in range(nbr):
            q = q_ref[r]
            col_head = _iota((t, 256), 1) >> 6
            zpad = jnp.zeros((32 - t, 256), F32)
            qbd = jnp.concatenate(
                [piece for h in range(N_HEADS) for piece in (jnp.where(col_head == h, q, 0.0), zpad)], 0)
            qbd_ref[r] = qbd * SCALE
            lf = _log_sigmoid(f_ref[r] + fb_ref[...])
            lf_ref[r] = lf
            tri = (_iota((t, t), 1) <= _iota((t, t), 0)).astype(F32)
            cs = _dot(tri, lf, HIGHEST)
            zcol = jnp.zeros((32 - t, 1), F32)
            cnew = jnp.concatenate([piece for h in range(N_HEADS) for piece in (cs[:, h:h + 1], zcol)], 0)
            cnew_ref[r] = jnp.broadcast_to(cnew, (128, 128))
            eye = _iota((t, t), 0) == _iota((t, t), 1)
            cs_rows = jnp.concatenate(
                [jnp.broadcast_to(jnp.sum(jnp.where(eye, cs[:, h:h + 1], 0.0), 0, keepdims=True), (32, t))
                 for h in range(N_HEADS)], 0)
            s = _dot_nt(qbd * SCALE, k_ref[r]) + cnew - cs_rows
            mask = _iota((1, t), 1) <= qidx
            s = jnp.where(mask, s, NEG_BIG)
            m = jnp.max(s, -1, keepdims=True)
            e = jnp.where(mask, jnp.exp(s - m), 0.0)
            m_ref[r] = jnp.broadcast_to(m, (128, 128))
            l_ref[r] = jnp.broadcast_to(jnp.sum(e, -1, keepdims=True), (128, 128))
            acc_ref[r] = _dot(e, v_ref[r])
            carry_ref[r] = jnp.zeros((8, 128), F32)

    lane = _iota((N_HEADS, page_sz), 1)
    for r in range(nbr):
        carry = carry_ref[r, 0:N_HEADS, :]
        qbd_bf = qbd_ref[r].astype(BF16)
        cnew = cnew_ref[r, :, 0:1]
        tiles = []
        for j in range(pps):
            lf_t = lf_refs[r * pps + j][0, 0]
            incl = lf_t
            d = 1
            while d < page_sz:
                incl = incl + jnp.where(lane < page_sz - d, pltpu.roll(incl, page_sz - d, 1), 0.0)
                d *= 2
            bias = _rows_per_head(incl - lf_t + carry, 32) + cnew
            carry = carry + incl[:, 0:1]
            tiles.append(_dot(qbd_bf, kv_refs[r * pps + j][0, 0, 0:256, :].astype(BF16)) + bias)
        carry_ref[r, 0:N_HEADS, :] = carry
        s = jnp.concatenate(tiles, 1)
        m_old = m_ref[r, :, 0:1]
        m_new = jnp.maximum(m_old, jnp.max(s, -1, keepdims=True))
        alpha = jnp.exp(m_old - m_new)
        e = jnp.exp(s - m_new)
        m_ref[r] = jnp.broadcast_to(m_new, (128, 128))
        l_ref[r] = alpha * l_ref[r] + jnp.sum(e, -1, keepdims=True)
        acc = alpha * acc_ref[r]
        for j in range(pps):
            acc = acc + _dot_nt(e[:, j * page_sz:(j + 1) * page_sz].astype(BF16),
                                kv_refs[r * pps + j][0, 0, 256:512, :].astype(BF16))
        acc_ref[r] = acc

    @pl.when(i == pl.num_programs(1) - 1)
    def _finish():
        for r in range(nbr):
            o = acc_ref[r] / jnp.maximum(l_ref[r, :, 0:1], 1e-30)
            for h in range(N_HEADS):
                sl = slice(h * HEAD_DIM, (h + 1) * HEAD_DIM)
                o_ref[r, :, sl] = o[h * 32:h * 32 + t, sl].astype(o_ref.dtype)


def _fox_sample_call(p, kv_t, lf_t, page_table, layer, f_bias):
    b, t, _ = p.shape
    n_pages = page_table.shape[1]
    page_sz = kv_t.shape[3]
    assert t == 8 and page_sz == 128
    pps = _pages_per_step(n_pages)
    nbr = max(d for d in (4, 2, 1) if b % d == 0)
    fb = jnp.zeros((1, 128), F32).at[0, :N_HEADS].set(f_bias)
    rev = lambda r, j: (lambda b_, i, pt: (pt[b_ * nbr + r, n_pages - 1 - (i * pps + j)], layer, 0, 0))
    slots = [(r, j) for r in range(nbr) for j in range(pps)]
    new = lambda width, col: pl.BlockSpec((nbr, t, width), lambda b_, i, pt: (b_, 0, col))
    grid_spec = pltpu.PrefetchScalarGridSpec(
        num_scalar_prefetch=1,
        grid=(b // nbr, n_pages // pps),
        in_specs=[pl.BlockSpec((1, 1, 512, page_sz), rev(r, j)) for r, j in slots]
        + [pl.BlockSpec((1, 1, N_HEADS, page_sz), rev(r, j)) for r, j in slots]
        + [new(256, P_CQ // 256), new(256, P_CK // 256), new(256, P_CV // 256), new(128, P_CF // 128),
           pl.BlockSpec((1, 128), lambda b_, i, pt: (0, 0))],
        out_specs=(pl.BlockSpec((nbr, t, 256), lambda b_, i, pt: (b_, 0, 0)),
                   pl.BlockSpec((nbr, t, 128), lambda b_, i, pt: (b_, 0, 0))),
        scratch_shapes=[pltpu.VMEM((nbr, 128, 256), F32), pltpu.VMEM((nbr, 128, 128), F32),
                        pltpu.VMEM((nbr, 128, 128), F32), pltpu.VMEM((nbr, 128, 256), F32),
                        pltpu.VMEM((nbr, 8, 128), F32), pltpu.VMEM((nbr, 128, 128), F32)])
    return pl.pallas_call(
        functools.partial(_fox_sample_kernel, pps=pps, nbr=nbr),
        out_shape=(jax.ShapeDtypeStruct((b, t, 256), BF16),
                   jax.ShapeDtypeStruct((b, t, 128), F32)),
        grid_spec=grid_spec,
        compiler_params=_params(("parallel", "arbitrary")),
        name="fox_sample",
    )(page_table, *([kv_t] * (nbr * pps)), *([lf_t] * (nbr * pps)), p, p, p, p, fb)


def _mxu(x):
    return x.astype(BF16) if x.shape[-2] % 16 == 0 else x


def _bdot(a, b):
    return lax.dot_general(_mxu(a), _mxu(b), (((2,), (1,)), ((0,), (0,))), preferred_element_type=F32)


def _bdot_nt(a, b):
    return lax.dot_general(_mxu(a), _mxu(b), (((2,), (2,)), ((0,), (0,))), preferred_element_type=F32)


def _same_block(ii, jj, size):
    shift = size.bit_length() - 1
    return (ii >> shift) == (jj >> shift)


def _gdn_kernel(qkv_ref, z_ref, ba_ref, cw_ref, cb_ref, s0_ref, pa_ref, ng_ref, o_ref, st_ref, xb_ref):
    ci = pl.program_id(1)
    nb, c = qkv_ref.shape[0], qkv_ref.shape[1]
    pad = 8

    @pl.when(ci == 0)
    def _init():
        st_ref[...] = s0_ref[...]
        xb_ref[:, pad - (CONV_K - 1):pad, :] = cb_ref[...]

    ii = _iota((1, c, c), 1)
    jj = _iota((1, c, c), 2)
    tri = (_iota((c, c), 1) <= _iota((c, c), 0)).astype(F32)
    eye = ii == jj
    qs, ks, vs, bs, acs = [], [], [], [], []
    for bi in range(nb):
        xb_ref[bi, pad:pad + c, :] = qkv_ref[bi]
        conv = xb_ref[bi, pad - 3:pad - 3 + c, :] * cw_ref[0:1, :]
        for j in range(1, CONV_K):
            conv = conv + xb_ref[bi, pad - 3 + j:pad - 3 + j + c, :] * cw_ref[j:j + 1, :]
        tail = xb_ref[bi, pad + c - (CONV_K - 1):pad + c, :]
        xb_ref[bi, pad - (CONV_K - 1):pad, :] = tail
        conv = _silu(conv)
        ba = ba_ref[bi]
        beta = _sigmoid(ba)
        g = -jnp.exp(pa_ref[0:1, :]) * _softplus(ba + pa_ref[1:2, :])
        acum = _dot(tri, g, HIGHEST)
        for h in range(N_HEADS):
            qs.append(conv[:, h * HEAD_DIM:(h + 1) * HEAD_DIM])
            ks.append(conv[:, 256 + h * HEAD_DIM:256 + (h + 1) * HEAD_DIM])
            vs.append(conv[:, 512 + h * HEAD_DIM:512 + (h + 1) * HEAD_DIM])
            bs.append(beta[:, h:h + 1])
            acs.append(acum[:, N_HEADS + h:N_HEADS + h + 1])
    q = jnp.stack(qs, 0)
    k = jnp.stack(ks, 0)
    v = jnp.stack(vs, 0)
    bcol = jnp.stack(bs, 0)
    acol = jnp.stack(acs, 0)
    q = q * lax.rsqrt(jnp.sum(q * q, -1, keepdims=True) + NORM_EPS) * SCALE
    k = k * lax.rsqrt(jnp.sum(k * k, -1, keepdims=True) + NORM_EPS)
    arow = jnp.sum(jnp.where(eye, acol, 0.0), 1, keepdims=True)
    decay = jnp.exp(jnp.minimum(acol - arow, 0.0))
    kb = k * bcol
    m = _bdot_nt(kb, k) * jnp.where(jj < ii, decay, 0.0)
    base = min(8, c)
    md = jnp.where(_same_block(ii, jj, base), m, 0.0)
    e = -md
    pw = _bdot(md, md)
    n = 2
    while n < base:
        e = e + pw + _bdot(e, pw)
        n *= 2
        if n < base:
            pw = _bdot(pw, pw)
    size = base
    while size < c:
        off = jnp.where(_same_block(ii, jj, 2 * size) & ~_same_block(ii, jj, size), m, 0.0)
        t1 = off + _bdot(e, off)
        e = e - t1 - _bdot(t1, e)
        size *= 2
    ea = jnp.exp(acol)
    rhs = jnp.concatenate([v * bcol, kb * ea], 2)
    sol = rhs + _bdot(e, rhs)
    s = st_ref[...].reshape(nb * N_HEADS, HEAD_DIM, HEAD_DIM)
    v_new = sol[:, :, 0:HEAD_DIM] - _bdot(sol[:, :, HEAD_DIM:128], s)
    att = _bdot_nt(q, k) * jnp.where(jj <= ii, decay, 0.0)
    o = _bdot(q * ea, s) + _bdot(att, v_new)
    o = o * lax.rsqrt(jnp.mean(o * o, -1, keepdims=True) + NORM_EPS) * ng_ref[...]
    a_last = acol[:, c - 1:c, :]
    kd = k * jnp.exp(a_last - acol)
    for bi in range(nb):
        z = z_ref[bi]
        for h in range(N_HEADS):
            gi = bi * N_HEADS + h
            sl = slice(h * HEAD_DIM, (h + 1) * HEAD_DIM)
            st_ref[bi, h] = jnp.exp(a_last[gi]) * s[gi] + _dot_tn(_mxu(kd[gi]), _mxu(v_new[gi]))
            o_ref[bi, :, sl] = (o[gi] * _silu(z[:, sl])).astype(o_ref.dtype)


def _gdn_call(p, conv_buf, state0, conv_w, a_log, dt_bias, norm_g):
    b, t, _ = p.shape
    c = min(GDN_CHUNK, t)
    nb = max(d for d in (4, 2, 1) if b % d == 0)
    assert t % c == 0 and c >= CONV_K - 1
    pa = jnp.zeros((2, 128), F32).at[0, N_HEADS:2 * N_HEADS].set(a_log).at[1, N_HEADS:2 * N_HEADS].set(dt_bias)
    st = pl.BlockSpec((nb, N_HEADS, HEAD_DIM, HEAD_DIM), lambda b_, i: (b_, 0, 0, 0))
    return pl.pallas_call(
        _gdn_kernel,
        out_shape=(jax.ShapeDtypeStruct((b, t, 256), BF16),
                   jax.ShapeDtypeStruct((b, N_HEADS, HEAD_DIM, HEAD_DIM), F32)),
        grid=(b // nb, t // c),
        in_specs=[pl.BlockSpec((nb, c, 768), lambda b_, i: (b_, i, P_DQKV // 768)),
                  pl.BlockSpec((nb, c, 256), lambda b_, i: (b_, i, P_DZ // 256)),
                  pl.BlockSpec((nb, c, 128), lambda b_, i: (b_, i, P_DBA // 128)),
                  pl.BlockSpec((CONV_K, 768), lambda b_, i: (0, 0)),
                  pl.BlockSpec((nb, CONV_K - 1, 768), lambda b_, i: (b_, 0, 0)),
                  st,
                  pl.BlockSpec((2, 128), lambda b_, i: (0, 0)),
                  pl.BlockSpec((1, HEAD_DIM), lambda b_, i: (0, 0))],
        out_specs=(pl.BlockSpec((nb, c, 256), lambda b_, i: (b_, i, 0)), st),
        scratch_shapes=[pltpu.VMEM((nb, 8 + c, 768), F32)],
        compiler_params=_params(("parallel", "arbitrary")),
        name="gdn",
    )(p, p, p, conv_w, conv_buf, state0, pa, norm_g.reshape(1, HEAD_DIM))


def _outproj_kernel(oa_ref, ob_ref, oc_ref, od_ref, x_ref, g1_ref, sc2_ref, sh2_ref, w_ref,
                    l1g_ref, l1b_ref, rw_ref, rb_ref, x1_ref, u2_ref, wc_ref):
    y = _dot(oa_ref[0], w_ref[0:256, :])
    y = y + _dot(ob_ref[0], w_ref[256:512, :])
    y = y + _dot(oc_ref[0], w_ref[512:768, :])
    y = y + _dot(od_ref[0], w_ref[768:1024, :])
    x1 = _ln(DN_ALPHA * x_ref[0] + g1_ref[0] * y) * l1g_ref[...] + l1b_ref[...]
    x1_ref[0] = x1
    u2 = _ln(x1) * (1.0 + sc2_ref[0]) + sh2_ref[0]
    u2_ref[0] = u2.astype(BF16)
    scores = _sigmoid(_dot_nt(rw_ref[...], u2, HIGHEST))
    biased = scores + rb_ref[...]
    tm = biased.shape[1]
    per = N_EXPERTS // N_GROUPS
    idx = _iota((per, tm), 0).astype(F32)
    grp_rows = []
    for g in range(N_GROUPS):
        vals = biased[g * per:(g + 1) * per, :]
        m1 = jnp.max(vals, 0, keepdims=True)
        first = jnp.min(jnp.where(vals == m1, idx, float(per)), 0, keepdims=True)
        m2 = jnp.max(jnp.where(idx == first, -jnp.inf, vals), 0, keepdims=True)
        grp_rows.append(m1 + m2)
    gs = jnp.concatenate(grp_rows, 0)
    gi = _iota((N_GROUPS, tm), 0)
    rank = jnp.zeros((N_GROUPS, tm), F32)
    for g2 in range(N_GROUPS):
        row = gs[g2:g2 + 1, :]
        rank = rank + jnp.where((row > gs) | ((row == gs) & (g2 < gi)), 1.0, 0.0)
    keep = jnp.where(rank < float(TOPK_GROUPS), 1.0, 0.0)
    emask = jnp.concatenate([jnp.broadcast_to(keep[g:g + 1, :], (per, tm)) for g in range(N_GROUPS)], 0) > 0.5
    sel = _topk_mask(jnp.where(emask, biased, -jnp.inf), TOP_K, 0)
    w = sel * scores
    w = w / jnp.sum(w, 0, keepdims=True) * ROUTED_SCALE
    wc_ref[0] = jnp.concatenate([w, jnp.zeros((LANES - N_EXPERTS, tm), F32)], 0).T


def _outproj_call(oa, ob, oc, od, x, g1, sc2, sh2, w_out, ln_g, ln_b, rw_t, rb):
    g, r, _ = x.shape
    tm = min(512, r)
    o_spec = pl.BlockSpec((1, tm, 256), lambda g_, i: (g_, i, 0))
    x_spec = pl.BlockSpec((1, tm, D_MODEL), lambda g_, i: (g_, i, 0))
    vec = pl.BlockSpec((1, D_MODEL), lambda g_, i: (0, 0))
    return pl.pallas_call(
        _outproj_kernel,
        out_shape=(jax.ShapeDtypeStruct((g, r, D_MODEL), F32),
                   jax.ShapeDtypeStruct((g, r, D_MODEL), BF16),
                   jax.ShapeDtypeStruct((g, r, LANES), F32)),
        grid=(g, r // tm),
        in_specs=[o_spec, o_spec, o_spec, o_spec, x_spec,
                  _mod_spec(g1, tm), _mod_spec(sc2, tm), _mod_spec(sh2, tm),
                  pl.BlockSpec((D_MODEL, D_MODEL), lambda g_, i: (0, 0)),
                  vec, vec,
                  pl.BlockSpec((N_EXPERTS, D_MODEL), lambda g_, i: (0, 0)),
                  pl.BlockSpec((N_EXPERTS, 1), lambda g_, i: (0, 0))],
        out_specs=(x_spec, x_spec, pl.BlockSpec((1, tm, LANES), lambda g_, i: (g_, i, 0))),
        compiler_params=_params(("parallel", "parallel")),
        name="outproj",
    )(oa, ob, oc, od, x, g1, sc2, sh2, w_out, ln_g.reshape(1, -1), ln_b.reshape(1, -1), rw_t, rb.reshape(-1, 1))


EXPERTS_PER_STEP = 4


def _swiglu_act(hid):
    return _silu(hid[:, 0:EXPERT_FF]) * hid[:, EXPERT_FF:2 * EXPERT_FF]


def _moe_kernel(u_ref, wc_ref, x_ref, g2_ref, wgu_ref, wdn_ref, sgu_ref, sdn_ref, l2g_ref, l2b_ref,
                o_ref, acc_ref):
    step = pl.program_id(2)
    eps = wgu_ref.shape[0]
    u = u_ref[0]

    @pl.when(step == 0)
    def _shared():
        acc_ref[...] = _dot(_swiglu_act(_dot(u, sgu_ref[...])).astype(BF16), sdn_ref[...])

    wc = wc_ref[0]
    lane = _iota(wc.shape, 1)
    acts = []
    for k in range(eps):
        col = jnp.sum(jnp.where(lane == step * eps + k, wc, 0.0), -1, keepdims=True)
        acts.append((_swiglu_act(_dot(u, wgu_ref[k])) * col).astype(BF16))
    act = jnp.concatenate(acts, 1)
    acc_ref[...] += _dot(act, wdn_ref[...].reshape(eps * EXPERT_FF, D_MODEL))

    @pl.when(step == pl.num_programs(2) - 1)
    def _finish():
        o_ref[0] = _ln(DN_ALPHA * x_ref[0] + g2_ref[0] * acc_ref[...]) * l2g_ref[...] + l2b_ref[...]


def _moe_call(u2, wc, x1, g2, wgu, wdn, layer, sgu, sdn, ln_g, ln_b):
    g, r, _ = x1.shape
    tm = min(1024, r)
    eps = EXPERTS_PER_STEP
    tok = lambda width: pl.BlockSpec((1, tm, width), lambda g_, i, e: (g_, i, 0))
    if g2.shape[1] == 1:
        g2_spec = pl.BlockSpec((1, 1, D_MODEL), lambda g_, i, e: (g_, 0, 0))
    else:
        g2_spec = tok(D_MODEL)
    vec = pl.BlockSpec((1, D_MODEL), lambda g_, i, e: (0, 0))
    return pl.pallas_call(
        _moe_kernel,
        out_shape=jax.ShapeDtypeStruct((g, r, D_MODEL), F32),
        grid=(g, r // tm, N_EXPERTS // eps),
        in_specs=[tok(D_MODEL), tok(LANES), tok(D_MODEL), g2_spec,
                  pl.BlockSpec((None, eps, D_MODEL, 2 * EXPERT_FF), lambda g_, i, e: (layer, e, 0, 0)),
                  pl.BlockSpec((None, eps, EXPERT_FF, D_MODEL), lambda g_, i, e: (layer, e, 0, 0)),
                  pl.BlockSpec((D_MODEL, 2 * EXPERT_FF), lambda g_, i, e: (0, 0)),
                  pl.BlockSpec((EXPERT_FF, D_MODEL), lambda g_, i, e: (0, 0)),
                  vec, vec],
        out_specs=tok(D_MODEL),
        scratch_shapes=[pltpu.VMEM((tm, D_MODEL), F32)],
        compiler_params=_params(("parallel", "parallel", "arbitrary")),
        name="moe",
    )(u2, wc, x1, g2, wgu, wdn, sgu, sdn, ln_g.reshape(1, -1), ln_b.reshape(1, -1))


def _run_trunk(x, mod, p0, weights, stacked, past):
    b, t, _ = x.shape
    per_token = t < 128
    if per_token:
        grp = lambda a: a.reshape(1, b * t, a.shape[-1])
        mod_rows = lambda m: jnp.repeat(m, t, axis=0)[None]
    else:
        grp = lambda a: a
        mod_rows = lambda m: m[:, None, :]
    ungrp = lambda a: a.reshape(b, t, a.shape[-1])

    outs = []
    for l in range(DEPTH):
        w = {k: v[l] for k, v in weights.items()}
        sh1, sc1, g1, sh2, sc2, g2 = [mod_rows(m) for m in jnp.split(mod[l], 6, axis=-1)]
        p = ungrp(_proj_call(grp(x), sc1, sh1, w["w_in"]))
        if past is None:
            o_a, nsa_rows, win_rows = _nsa_prompt_call(p, w["nsa_pool"])
            win_new = win_rows[:, t - min(NSA_WINDOW, t):]
            o_c, logf = _fox_prompt_call(p, w["fox_f_bias"])
            ret_s0 = jnp.zeros((b, N_HEADS, HEAD_DIM, HEAD_DIM), F32)
            gdn_s0 = ret_s0
            conv_buf = jnp.zeros((b, CONV_K - 1, 3 * GROUP_WIDTH), F32)
        else:
            o_a, nsa_rows, win_new = _nsa_sample_call(p, past["nsa_t"], past["page_table"], l,
                                                      past["win_t"][l], w["nsa_pool"], p0)
            o_c, logf = _fox_sample_call(p, past["fox_kv_t"], past["fox_lf_t"], past["page_table"], l,
                                         w["fox_f_bias"])
            ret_s0, gdn_s0, conv_buf = past["state_ret"][l], past["state_gdn"][l], past["state_gdn_conv"][l]
        o_b, ret_s = _ret_call(p, ret_s0, w["ret_gn_g"], w["ret_gn_b"], p0)
        o_d, gdn_s = _gdn_call(p, conv_buf, gdn_s0, w["gdn_conv_w"], w["gdn_A_log"], w["gdn_dt_bias"],
                               w["gdn_norm_g"])
        x1, u2, wc = _outproj_call(grp(o_a), grp(o_b), grp(o_c), grp(o_d), grp(x), g1, sc2, sh2,
                                   w["w_out"], w["ln1_g"], w["ln1_b"], w["router_w_t"], w["router_b"])
        x = ungrp(_moe_call(u2, wc, x1, g2, stacked["exp_w_gu"], stacked["exp_w_down"], l,
                            w["sh_w_gu"], w["sh_w_down"], w["ln2_g"], w["ln2_b"]))
        qkv = p[:, :, P_DQKV:P_DQKV + 768]
        conv_new = jnp.concatenate([conv_buf, qkv], axis=1)[:, t:]
        outs.append((nsa_rows.reshape(b, t, 4, HEAD_DIM),
                     p[:, :, P_CK:P_CK + 512].reshape(b, t, 2, N_HEADS, HEAD_DIM),
                     logf[:, :, :N_HEADS],
                     win_new.reshape(b, win_new.shape[1], 2, HEAD_DIM),
                     ret_s, gdn_s, conv_new))
    nsa, fkv, flf, win, ret, gdn, conv = zip(*outs)
    return x, (jnp.stack(nsa, 1), jnp.stack(fkv, 1), jnp.stack(flf, 1), jnp.stack(win, 0),
               jnp.stack(ret, 0), jnp.stack(gdn, 0), jnp.stack(conv, 0))


def kernel(x_prompt, x_sample, cache_nsa, cache_fox_kv, cache_fox_logf, state_nsa_win, state_ret, state_gdn, state_gdn_conv, page_table, c_prompt, c_sample, w_mod, b_mod, w_in, w_out, nsa_pool, ret_gn_g, ret_gn_b, fox_f_bias, gdn_conv_w, gdn_A_log, gdn_dt_bias, gdn_norm_g, ln1_g, ln1_b, ln2_g, ln2_b, router_w, router_b, exp_w_gu, exp_w_down, sh_w_gu, sh_w_down):
    b = x_prompt.shape[0]
    db = x_sample.shape[0]
    n_pool, _, page_sz = cache_nsa.shape[:3]
    past_len = page_table.shape[1] * page_sz
    w_in_p = _permute_columns(w_in.astype(BF16), _proj_perm())
    weights = dict(
        w_in=w_in_p, w_out=w_out.astype(BF16), nsa_pool=nsa_pool, ret_gn_g=ret_gn_g, ret_gn_b=ret_gn_b,
        fox_f_bias=fox_f_bias, gdn_conv_w=gdn_conv_w, gdn_A_log=gdn_A_log, gdn_dt_bias=gdn_dt_bias,
        gdn_norm_g=gdn_norm_g, ln1_g=ln1_g, ln1_b=ln1_b, ln2_g=ln2_g, ln2_b=ln2_b,
        router_w_t=jnp.swapaxes(router_w, 1, 2), router_b=router_b,
        sh_w_gu=sh_w_gu.astype(BF16), sh_w_down=sh_w_down.astype(BF16))
    stacked = dict(exp_w_gu=exp_w_gu.astype(BF16), exp_w_down=exp_w_down.astype(BF16))
    n_c = b + db
    n_pad = -n_c % 8
    c_all = jnp.concatenate([c_prompt, c_sample, jnp.zeros((n_pad, D_MODEL), F32)], axis=0)
    mod = _mod_call(c_all, w_mod, b_mod)
    past = dict(
        nsa_t=jnp.transpose(cache_nsa, (0, 1, 3, 4, 2)).reshape(n_pool, DEPTH, 4 * HEAD_DIM, page_sz),
        fox_kv_t=jnp.transpose(cache_fox_kv, (0, 1, 3, 4, 5, 2)).reshape(n_pool, DEPTH, 2 * GROUP_WIDTH, page_sz),
        fox_lf_t=jnp.transpose(cache_fox_logf, (0, 1, 3, 2)),
        win_t=jnp.transpose(state_nsa_win, (0, 1, 3, 4, 2)).reshape(DEPTH, db, 2 * HEAD_DIM, state_nsa_win.shape[2]),
        state_ret=state_ret, state_gdn=state_gdn, state_gdn_conv=state_gdn_conv, page_table=page_table)
    y_p, (nsa_p, fkv_p, flf_p, win_p, ret_p, gdn_p, conv_p) = _run_trunk(x_prompt, mod[:, :b], 0, weights, stacked, None)
    y_s, (nsa_s, fkv_s, flf_s, win_s, ret_s, gdn_s, conv_s) = _run_trunk(x_sample, mod[:, b:n_c], past_len, weights, stacked, past)
    return (y_p, y_s, nsa_p, nsa_s, fkv_p, fkv_s, flf_p, flf_s, win_p, win_s,
            ret_p, ret_s, gdn_p, gdn_s, conv_p, conv_s)
```
